```python
import jax, jax.numpy as jnp
from jax import lax
import numpy as np

D_MODEL = 2048
BATCH = 1
SEQ = 8192
DEPTH = 1

CHUNK = 64
EPS = 1e-6

GLA_HEADS = 4
GLA_DK = 128
GLA_DV = 256
GLA_GATE_RANK = 16
GLA_GATE_TAU = 16.0

GDN_HEADS = 8
GDN_DK = 128
GDN_DV = 128
GDN_CONV = 4

MIX_WIDTH = GLA_HEADS * GLA_DV + GDN_HEADS * GDN_DV

GLA_QK = GLA_HEADS * GLA_DK
GLA_V = GLA_HEADS * GLA_DV
GDN_QK = GDN_HEADS * GDN_DK
GDN_V = GDN_HEADS * GDN_DV
SPLIT_SIZES = (GLA_QK, GLA_QK, GLA_V, GLA_GATE_RANK, GLA_V,
               GDN_QK, GDN_QK, GDN_V, GDN_HEADS, GDN_HEADS, GDN_V)
IN_COLS = sum(SPLIT_SIZES)

D_FF = ((8 * D_MODEL // 3 + 255) // 256) * 256

kernel_name = "hybrid_gla_gated_deltanet_block"


def rms_norm(x, w):
    xf = x.astype(jnp.float32)
    y = xf * lax.rsqrt(jnp.mean(xf * xf, axis=-1, keepdims=True) + EPS)
    return y * w.astype(jnp.float32)


def l2_norm(x):
    return x * lax.rsqrt(jnp.sum(x * x, axis=-1, keepdims=True) + EPS)


def to_chunks(t, n_heads):
    b, l, f = t.shape
    d = f // n_heads
    n = l // CHUNK
    return t.reshape(b, n, CHUNK, n_heads, d).transpose(1, 0, 3, 2, 4)


def from_chunks(t):
    n, b, h, c, d = t.shape
    return t.transpose(1, 0, 3, 2, 4).reshape(b, n * c, h, d)


def gla_chunked(q, k, v, log_a):
    _, b, h, c, dk = q.shape
    dv = v.shape[-1]
    causal = jnp.tril(jnp.ones((c, c), bool))

    def step(state, inp):
        qc, kc, vc, gc = inp
        cum = jnp.cumsum(gc, axis=-2)
        diff = cum[..., :, None, :] - cum[..., None, :, :]
        decay = jnp.exp(jnp.where(causal[:, :, None], diff, -jnp.inf))
        scores = jnp.einsum('bhtd,bhsd,bhtsd->bhts', qc, kc, decay)
        o = (jnp.einsum('bhts,bhsv->bhtv', scores, vc)
             + jnp.einsum('bhtd,bhdv->bhtv', qc * jnp.exp(cum), state))
        last = cum[..., -1:, :]
        state = (jnp.exp(last)[..., 0, :, None] * state
                 + jnp.einsum('bhsd,bhsv->bhdv', kc * jnp.exp(last - cum), vc))
        return state, o

    s0 = jnp.zeros((b, h, dk, dv), jnp.float32)
    _, o = lax.scan(step, s0, (q, k, v, log_a))
    return o


def gated_delta_chunked(q, k, v, g, beta):
    _, b, h, c, dk = q.shape
    dv = v.shape[-1]
    cum = jnp.cumsum(g, axis=-1)
    diff = cum[..., :, None] - cum[..., None, :]
    incl = jnp.tril(jnp.ones((c, c), bool))
    strict = jnp.tril(jnp.ones((c, c), bool), -1)
    gamma = jnp.exp(jnp.where(incl, diff, -jnp.inf))
    kb = k * beta[..., None]
    a_low = jnp.where(strict, jnp.einsum('nbhtd,nbhsd->nbhts', kb, k) * gamma, 0.0)
    eye = jnp.eye(c, dtype=a_low.dtype)
    rhs = jnp.concatenate([v * beta[..., None], kb * jnp.exp(cum)[..., None]], axis=-1)
    sol = lax.linalg.triangular_solve(eye + a_low, rhs, left_side=True, lower=True)
    u, w = sol[..., :dv], sol[..., dv:]
    attn = jnp.einsum('nbhtd,nbhsd->nbhts', q, k) * gamma
    qd = q * jnp.exp(cum)[..., None]
    kd = k * jnp.exp(cum[..., -1:] - cum)[..., None]
    g_last = jnp.exp(cum[..., -1])

    def step(state, inp):
        u_c, w_c, attn_c, qd_c, kd_c, gl = inp
        v_new = u_c - jnp.einsum('bhtk,bhkv->bhtv', w_c, state)
        o = (jnp.einsum('bhtk,bhkv->bhtv', qd_c, state)
             + jnp.einsum('bhts,bhsv->bhtv', attn_c, v_new))
        state = gl[..., None, None] * state + jnp.einsum('bhsk,bhsv->bhkv', kd_c, v_new)
        return state, o

    s0 = jnp.zeros((b, h, dk, dv), jnp.float32)
    _, o = lax.scan(step, s0, (u, w, attn, qd, kd, g_last))
    return o


def causal_depthwise_conv(x, w):
    ch = x.shape[-1]
    return lax.conv_general_dilated(
        x, w[:, None, :], window_strides=(1,), padding=[(GDN_CONV - 1, 0)],
        dimension_numbers=('NWC', 'WIO', 'NWC'), feature_group_count=ch)


def setup_inputs(seed: int = 0) -> dict:
    key = jax.random.key(seed)
    ks = jax.random.split(key, 20)
    f32 = jnp.float32

    def nrm(k, shape, scale):
        return jax.random.normal(k, shape, f32) * scale

    def gain(k, n):
        return 1.0 + 0.02 * jax.random.normal(k, (n,), f32)

    dt = jnp.exp(jax.random.uniform(ks[9], (GDN_HEADS,), f32, np.log(1e-3), np.log(1e-1)))
    dt_bias = dt + jnp.log(-jnp.expm1(-dt))
    return {
        "x": jax.random.normal(ks[0], (BATCH, SEQ, D_MODEL), f32),
        "attn_norm_w": gain(ks[1], D_MODEL),
        "w_in": nrm(ks[2], (D_MODEL, IN_COLS), D_MODEL ** -0.5),
        "gla_gate_w2": nrm(ks[3], (GLA_GATE_RANK, GLA_QK), GLA_GATE_RANK ** -0.5),
        "gla_gate_b": nrm(ks[4], (GLA_QK,), 0.01),
        "gla_norm_w": gain(ks[5], GLA_DV),
        "gdn_conv_w": nrm(ks[6], (GDN_CONV, 2 * GDN_QK + GDN_V), GDN_CONV ** -0.5),
        "gdn_a_log": jnp.log(jax.random.uniform(ks[7], (GDN_HEADS,), f32, 1.0, 16.0)),
        "gdn_dt_bias": dt_bias,
        "gdn_norm_w": gain(ks[8], GDN_DV),
        "w_out": nrm(ks[10], (MIX_WIDTH, D_MODEL), MIX_WIDTH ** -0.5),
        "ffn_norm_w": gain(ks[11], D_MODEL),
        "w_gate": nrm(ks[12], (D_MODEL, D_FF), D_MODEL ** -0.5),
        "w_up": nrm(ks[13], (D_MODEL, D_FF), D_MODEL ** -0.5),
        "w_down": nrm(ks[14], (D_FF, D_MODEL), D_FF ** -0.5),
        "final_norm_w": gain(ks[15], D_MODEL),
    }


def reference(x, attn_norm_w, w_in, gla_gate_w2, gla_gate_b, gla_norm_w, gdn_conv_w,
              gdn_a_log, gdn_dt_bias, gdn_norm_w, w_out, ffn_norm_w, w_gate, w_up,
              w_down, final_norm_w):
    f32 = jnp.float32
    bsz, seq, _ = x.shape
    h_res = x.astype(f32)
    for _layer in range(DEPTH):
        h = rms_norm(h_res, attn_norm_w)
        p = h @ w_in.astype(f32)
        idx = np.cumsum(SPLIT_SIZES)[:-1].tolist()
        (a_q, a_k, a_v, a_lr, a_gate,
         b_q, b_k, b_v, b_a, b_b, b_z) = jnp.split(p, idx, axis=-1)

        log_a = jax.nn.log_sigmoid(a_lr @ gla_gate_w2.astype(f32) + gla_gate_b.astype(f32)) / GLA_GATE_TAU
        o_a = gla_chunked(to_chunks(a_q * GLA_DK ** -0.5, GLA_HEADS), to_chunks(a_k, GLA_HEADS),
                          to_chunks(a_v, GLA_HEADS), to_chunks(log_a, GLA_HEADS))
        o_a = rms_norm(from_chunks(o_a), gla_norm_w)
        o_a = o_a.reshape(bsz, seq, GLA_V) * jax.nn.silu(a_gate)

        qkv = jax.nn.silu(causal_depthwise_conv(jnp.concatenate([b_q, b_k, b_v], -1),
                                                gdn_conv_w.astype(f32)))
        cq, ck, cv = jnp.split(qkv, [GDN_QK, 2 * GDN_QK], axis=-1)
        cq = l2_norm(cq.reshape(bsz, seq, GDN_HEADS, GDN_DK)) * GDN_DK ** -0.5
        ck = l2_norm(ck.reshape(bsz, seq, GDN_HEADS, GDN_DK))
        g = -jnp.exp(gdn_a_log.astype(f32)) * jax.nn.softplus(b_a + gdn_dt_bias.astype(f32))
        beta = jax.nn.sigmoid(b_b)
        n_ch = seq // CHUNK
        gc = g.reshape(bsz, n_ch, CHUNK, GDN_HEADS).transpose(1, 0, 3, 2)
        bc = beta.reshape(bsz, n_ch, CHUNK, GDN_HEADS).transpose(1, 0, 3, 2)
        o_b = gated_delta_chunked(to_chunks(cq.reshape(bsz, seq, GDN_QK), GDN_HEADS),
                                  to_chunks(ck.reshape(bsz, seq, GDN_QK), GDN_HEADS),
                                  to_chunks(cv, GDN_HEADS), gc, bc)
        o_b = rms_norm(from_chunks(o_b), gdn_norm_w)
        o_b = o_b.reshape(bsz, seq, GDN_V) * jax.nn.silu(b_z)

        mixed = jnp.concatenate([o_a, o_b], axis=-1)
        h_res = h_res + mixed @ w_out.astype(f32)

        hf = rms_norm(h_res, ffn_norm_w)
        ff = jax.nn.silu(hf @ w_gate.astype(f32)) * (hf @ w_up.astype(f32))
        h_res = h_res + ff @ w_down.astype(f32)

    y = rms_norm(h_res, final_norm_w)
    return y.astype(x.dtype)
```

```python
import functools

import numpy as np
import jax
import jax.numpy as jnp
from jax import lax
from jax.experimental import pallas as pl
from jax.experimental.pallas import tpu as pltpu

F32 = jnp.float32
BF16 = jnp.bfloat16

EPS = 1e-6
CHUNK = 64

GLA_HEADS = 4
GLA_DK = 128
GLA_DV = 256
GLA_GATE_RANK = 16
GLA_GATE_TAU = 16.0
GLA_QK = GLA_HEADS * GLA_DK
GLA_V = GLA_HEADS * GLA_DV

GDN_HEADS = 8
GDN_DK = 128
GDN_DV = 128
GDN_CONV = 4
GDN_QK = GDN_HEADS * GDN_DK
GDN_V = GDN_HEADS * GDN_DV

LANES = 128
SUB_BLOCK = 16
VMEM_LIMIT = 48 * 1024 * 1024

COL_LR = 0
COL_A = GLA_GATE_RANK
COL_B = GLA_GATE_RANK + GDN_HEADS


def _dot(a, b):
    return jnp.dot(a, b, preferred_element_type=F32)


def _dot_nt(a, b):
    return lax.dot_general(a, b, (((1,), (1,)), ((), ())), preferred_element_type=F32)


def _dot_tn(a, b):
    return lax.dot_general(a, b, (((0,), (0,)), ((), ())), preferred_element_type=F32)


def _split3(a):
    hi = a.astype(BF16)
    r1 = a - hi.astype(F32)
    mid = r1.astype(BF16)
    lo = (r1 - mid.astype(F32)).astype(BF16)
    return hi, mid, lo


def _dot_exact_rhs(a, b_exact):
    hi, mid, lo = _split3(a)
    return _dot(hi, b_exact) + _dot(mid, b_exact) + _dot(lo, b_exact)


def _dot_exact_lhs(a_exact, b):
    hi, mid, lo = _split3(b)
    return _dot(a_exact, hi) + _dot(a_exact, mid) + _dot(a_exact, lo)


def _sigmoid(x):
    return 1.0 / (1.0 + jnp.exp(-x))


def _silu(x):
    return x * _sigmoid(x)


def _softplus(x):
    return jnp.maximum(x, 0.0) + jnp.log1p(jnp.exp(-jnp.abs(x)))


def _log_sigmoid(x):
    return -_softplus(-x)


def _inproj_kernel(x_ref, nw_ref, wbig_ref, wsm_ref, wsmt_ref,
                   pbig_ref, psm_ref, psmt_ref, h_ref):
    j = pl.program_id(1)

    @pl.when(j == 0)
    def _():
        x = x_ref[...]
        ms = jnp.mean(x * x, axis=-1, keepdims=True)
        h = x * lax.rsqrt(ms + EPS) * nw_ref[...]
        hb = h.astype(BF16)
        h_ref[...] = hb
        hl = (h - hb.astype(F32)).astype(BF16)
        psm_ref[...] = (_dot(hb, wsm_ref[0]) + _dot(hl, wsm_ref[0])
                        + _dot(hb, wsm_ref[1]))
        psmt_ref[...] = (_dot_nt(wsmt_ref[0], hb) + _dot_nt(wsmt_ref[0], hl)
                         + _dot_nt(wsmt_ref[1], hb))

    pbig_ref[...] = _dot(h_ref[...], wbig_ref[...]).astype(BF16)


def _inproj(x2, norm_w, w_big, w_sm, w_smt, tm, tn):
    L, D = x2.shape
    nbig = w_big.shape[1]
    grid = (L // tm, nbig // tn)
    return pl.pallas_call(
        _inproj_kernel,
        grid=grid,
        in_specs=[
            pl.BlockSpec((tm, D), lambda i, j: (i, 0)),
            pl.BlockSpec((1, D), lambda i, j: (0, 0)),
            pl.BlockSpec((D, tn), lambda i, j: (0, j)),
            pl.BlockSpec((2, D, LANES), lambda i, j: (0, 0, 0)),
            pl.BlockSpec((2, LANES, D), lambda i, j: (0, 0, 0)),
        ],
        out_specs=[
            pl.BlockSpec((tm, tn), lambda i, j: (i, j)),
            pl.BlockSpec((tm, LANES), lambda i, j: (i, 0)),
            pl.BlockSpec((LANES, tm), lambda i, j: (0, i)),
        ],
        out_shape=[
            jax.ShapeDtypeStruct((L, nbig), BF16),
            jax.ShapeDtypeStruct((L, LANES), F32),
            jax.ShapeDtypeStruct((LANES, L), F32),
        ],
        scratch_shapes=[pltpu.VMEM((tm, D), BF16)],
        compiler_params=pltpu.CompilerParams(
            dimension_semantics=("arbitrary", "arbitrary"),
            vmem_limit_bytes=VMEM_LIMIT),
        name="inproj",
    )(x2, norm_w, w_big, w_sm, w_smt)


def _ones_where(mask):
    return jnp.where(mask, 1.0, 0.0).astype(BF16)


def _chunk_tril(rows, upper=False):
    ti = lax.broadcasted_iota(jnp.int32, (rows, rows), 0)
    si = lax.broadcasted_iota(jnp.int32, (rows, rows), 1)
    same = (ti >> 6) == (si >> 6)
    return same & ((ti <= si) if upper else (ti >= si))


def _gla_chunk(q_ref, k_ref, v_ref, gate_ref, nw_ref, sel_ref, o_ref, st_ref,
               c_scr, k_scr, r0):
    C = CHUNK
    rows = pl.ds(r0, C)
    row = lax.broadcasted_iota(jnp.int32, (C, GLA_DK), 0)
    ti = lax.broadcasted_iota(jnp.int32, (C, C), 0)
    si = lax.broadcasted_iota(jnp.int32, (C, C), 1)
    in_hi32 = row >= 32
    right16 = (row & 31) >= 16
    row_lt32 = row < 32
    same_pair = (ti >> 5) == (si >> 5)
    same_blk = (ti >> 4) == (si >> 4)
    trow = lax.broadcasted_iota(jnp.int32, (SUB_BLOCK, GLA_DK), 0)

    for h in range(GLA_HEADS):
        ks = slice(h * GLA_DK, (h + 1) * GLA_DK)
        vs = slice(h * GLA_DV, (h + 1) * GLA_DV)
        q = q_ref[rows, ks].astype(F32) * (GLA_DK ** -0.5)
        k = k_scr[rows, ks]
        v = v_ref[rows, vs]
        c = c_scr[rows, ks]

        ref = c_scr[pl.ds(r0 + 31, 1), ks]
        e = jnp.exp(jnp.where(in_hi32, c - ref, ref - c))
        qt = jnp.where(in_hi32, q * e, 0.0).astype(BF16)
        kt = jnp.where(in_hi32, 0.0, k * e).astype(BF16)
        a = _dot_nt(qt, kt)
        ref = jnp.where(row_lt32, c_scr[pl.ds(r0 + 15, 1), ks], c_scr[pl.ds(r0 + 47, 1), ks])
        e = jnp.exp(jnp.where(right16, c - ref, ref - c))
        qt = jnp.where(right16, q * e, 0.0).astype(BF16)
        kt = jnp.where(right16, 0.0, k * e).astype(BF16)
        a = a + jnp.where(same_pair, _dot_nt(qt, kt), 0.0)
        blocks = []
        for b in range(C // SUB_BLOCK):
            b0 = b * SUB_BLOCK
            cb = c[b0:b0 + SUB_BLOCK]
            qb = q[b0:b0 + SUB_BLOCK]
            parts = []
            for s in range(SUB_BLOCK):
                c_s = c_scr[pl.ds(r0 + b0 + s, 1), ks]
                k_s = k_scr[pl.ds(r0 + b0 + s, 1), ks]
                e = jnp.exp(jnp.where(trow >= s, cb - c_s, -jnp.inf))
                parts.append(((qb * e) * k_s).astype(BF16))
            blocks.append(jnp.concatenate(parts, axis=1))
        p2 = jnp.concatenate(blocks, axis=0)
        a = a + jnp.where(same_blk, _dot(p2, sel_ref[...]), 0.0)

        st = st_ref[h]
        c_last = c_scr[pl.ds(r0 + C - 1, 1), ks]
        qd = (q * jnp.exp(c)).astype(BF16)
        o = _dot(a.astype(BF16), v) + _dot_nt(qd, st.astype(BF16))
        kd = (k * jnp.exp(c_last - c)).astype(BF16)
        st_ref[h] = jnp.exp(c_last) * st + _dot_tn(v, kd)

        ms = jnp.mean(o * o, axis=-1, keepdims=True)
        on = o * lax.rsqrt(ms + EPS) * nw_ref[...]
        o_ref[rows, vs] = (on * _silu(gate_ref[rows, vs].astype(F32))).astype(BF16)


def _gla_kernel(q_ref, k_ref, v_ref, gate_ref, psm_ref, w2_ref, gb_ref, nw_ref,
                sel_ref, o_ref, st_ref, c_scr, k_scr, *, chunks_per_step):
    @pl.when(pl.program_id(0) == 0)
    def _():
        st_ref[...] = jnp.zeros_like(st_ref)

    a_hi, a_mid, _ = _split3(psm_ref[...])
    z = (_dot(a_hi, w2_ref[0]) + _dot(a_mid, w2_ref[0]) + _dot(a_hi, w2_ref[1])
         + gb_ref[...])
    log_a = _log_sigmoid(z) * (1.0 / GLA_GATE_TAU)
    tril = _ones_where(_chunk_tril(CHUNK * chunks_per_step))
    c_scr[...] = _dot_exact_lhs(tril, log_a)
    k_scr[...] = k_ref[...].astype(F32)

    for ci in range(chunks_per_step):
        _gla_chunk(q_ref, k_ref, v_ref, gate_ref, nw_ref, sel_ref, o_ref, st_ref,
                   c_scr, k_scr, ci * CHUNK)


def _gla(p_big, p_sm, w2, gate_b, norm_w, sel, chunks_per_step):
    L = p_big.shape[0]
    R = CHUNK * chunks_per_step
    kern = functools.partial(_gla_kernel, chunks_per_step=chunks_per_step)
    return pl.pallas_call(
        kern,
        grid=(L // R,),
        in_specs=[
            pl.BlockSpec((R, GLA_QK), lambda n: (n, 0)),
            pl.BlockSpec((R, GLA_QK), lambda n: (n, 1)),
            pl.BlockSpec((R, GLA_V), lambda n: (n, 1)),
            pl.BlockSpec((R, GLA_V), lambda n: (n, 2)),
            pl.BlockSpec((R, LANES), lambda n: (n, 0)),
            pl.BlockSpec((2, LANES, GLA_QK), lambda n: (0, 0, 0)),
            pl.BlockSpec((1, GLA_QK), lambda n: (0, 0)),
            pl.BlockSpec((1, GLA_DV), lambda n: (0, 0)),
            pl.BlockSpec((SUB_BLOCK * GLA_DK, CHUNK), lambda n: (0, 0)),
        ],
        out_specs=pl.BlockSpec((R, GLA_V), lambda n: (n, 0)),
        out_shape=jax.ShapeDtypeStruct((L, GLA_V), BF16),
        scratch_shapes=[
            pltpu.VMEM((GLA_HEADS, GLA_DV, GLA_DK), F32),
            pltpu.VMEM((R, GLA_QK), F32),
            pltpu.VMEM((R, GLA_QK), F32),
        ],
        compiler_params=pltpu.CompilerParams(
            dimension_semantics=("arbitrary",),
            vmem_limit_bytes=VMEM_LIMIT),
        name="gla",
    )(p_big, p_big, p_big, p_big, p_sm, w2, gate_b, norm_w, sel)


def _gdn_chunk(z_ref, nw_ref, o_ref, s_ref, qkv, cum_col, cum_row, beta_col, r0):
    C = CHUNK
    rows = pl.ds(r0, C)
    ti = lax.broadcasted_iota(jnp.int32, (C, C), 0)
    si = lax.broadcasted_iota(jnp.int32, (C, C), 1)
    incl = ti >= si
    strict = ti > si
    eye = jnp.where(ti == si, 1.0, 0.0)

    for h in range(GDN_HEADS):
        hs = slice(h * GDN_DK, (h + 1) * GDN_DK)
        cq = qkv[r0:r0 + C, h * GDN_DK:(h + 1) * GDN_DK]
        ck = qkv[r0:r0 + C, GDN_QK + h * GDN_DK:GDN_QK + (h + 1) * GDN_DK]
        cv = qkv[r0:r0 + C, 2 * GDN_QK + h * GDN_DV:2 * GDN_QK + (h + 1) * GDN_DV]
        qn = cq * lax.rsqrt(jnp.sum(cq * cq, axis=-1, keepdims=True) + EPS) * (GDN_DK ** -0.5)
        kn = ck * lax.rsqrt(jnp.sum(ck * ck, axis=-1, keepdims=True) + EPS)

        cc = cum_col[r0:r0 + C, COL_A + h:COL_A + h + 1]
        cr = cum_row[COL_A + h:COL_A + h + 1, r0:r0 + C]
        beta = beta_col[r0:r0 + C, COL_B + h:COL_B + h + 1]
        gamma = jnp.exp(jnp.where(incl, cc - cr, -jnp.inf))

        kb = kn * beta
        knb = kn.astype(BF16)
        kk = _dot_nt(kb.astype(BF16), knb)
        n_low = jnp.where(strict, kk * gamma, 0.0)
        t_inv = eye - n_low
        pw = n_low
        for _ in range(5):
            pwb = pw.astype(BF16)
            pw = _dot(pwb, pwb)
            t_inv = t_inv + _dot(t_inv.astype(BF16), pw.astype(BF16))

        e_c = jnp.exp(cc)
        rhs = jnp.concatenate([cv * beta, kb * e_c], axis=1).astype(BF16)
        sol = _dot(t_inv.astype(BF16), rhs)
        u = sol[:, :GDN_DV]
        w = sol[:, GDN_DV:]
        attn = _dot_nt(qn.astype(BF16), knb) * gamma
        c_last = cc[C - 1:C, :]
        qd = (qn * e_c).astype(BF16)
        kd = (kn * jnp.exp(c_last - cc)).astype(BF16)

        s = s_ref[h]
        sb = s.astype(BF16)
        v_new = u - _dot(w.astype(BF16), sb)
        vnb = v_new.astype(BF16)
        o = _dot(qd, sb) + _dot(attn.astype(BF16), vnb)
        s_ref[h] = jnp.exp(c_last) * s + _dot_tn(kd, vnb)

        ms = jnp.mean(o * o, axis=-1, keepdims=True)
        on = o * lax.rsqrt(ms + EPS) * nw_ref[...]
        o_ref[rows, hs] = (on * _silu(z_ref[rows, hs].astype(F32))).astype(BF16)


def _gdn_kernel(q_ref, k_ref, v_ref, z_ref, psm_ref, psmt_ref, cw_ref, arow_ref,
                acol_ref, nw_ref, o_ref, s_ref, ext_scr, *, chunks_per_step):
    R = CHUNK * chunks_per_step

    @pl.when(pl.program_id(0) == 0)
    def _():
        s_ref[...] = jnp.zeros_like(s_ref)
        ext_scr[pl.ds(0, 8), :] = jnp.zeros((8, ext_scr.shape[1]), F32)

    psm = psm_ref[...]
    g_col = -jnp.exp(arow_ref[0:1, :]) * _softplus(psm + arow_ref[1:2, :])
    beta_col = _sigmoid(psm)
    cum_col = _dot_exact_lhs(_ones_where(_chunk_tril(R)), g_col)
    g_row = -jnp.exp(acol_ref[:, 0:1]) * _softplus(psmt_ref[...] + acol_ref[:, 1:2])
    cum_row = _dot_exact_rhs(g_row, _ones_where(_chunk_tril(R, upper=True)))

    ext_scr[pl.ds(8, R), 0:GDN_QK] = q_ref[...].astype(F32)
    ext_scr[pl.ds(8, R), GDN_QK:2 * GDN_QK] = k_ref[...].astype(F32)
    ext_scr[pl.ds(8, R), 2 * GDN_QK:] = v_ref[...].astype(F32)
    conv = cw_ref[GDN_CONV - 1:GDN_CONV, :] * ext_scr[pl.ds(8, R), :]
    for j in range(GDN_CONV - 1):
        conv = conv + cw_ref[j:j + 1, :] * ext_scr[pl.ds(8 - (GDN_CONV - 1) + j, R), :]
    ext_scr[pl.ds(0, 8), :] = ext_scr[pl.ds(R, 8), :]
    qkv = _silu(conv)

    for ci in range(chunks_per_step):
        _gdn_chunk(z_ref, nw_ref, o_ref, s_ref, qkv, cum_col, cum_row, beta_col, ci * CHUNK)


def _gdn(p_big, p_sm, p_smt, conv_w, a_row, a_col, norm_w, chunks_per_step):
    L = p_big.shape[0]
    R = CHUNK * chunks_per_step
    base = (GLA_QK * 2 + GLA_V * 2) // GDN_QK
    kern = functools.partial(_gdn_kernel, chunks_per_step=chunks_per_step)
    return pl.pallas_call(
        kern,
        grid=(L // R,),
        in_specs=[
            pl.BlockSpec((R, GDN_QK), lambda n: (n, base)),
            pl.BlockSpec((R, GDN_QK), lambda n: (n, base + 1)),
            pl.BlockSpec((R, GDN_V), lambda n: (n, base + 2)),
            pl.BlockSpec((R, GDN_V), lambda n: (n, base + 3)),
            pl.BlockSpec((R, LANES), lambda n: (n, 0)),
            pl.BlockSpec((LANES, R), lambda n: (0, n)),
            pl.BlockSpec((GDN_CONV, 2 * GDN_QK + GDN_V), lambda n: (0, 0)),
            pl.BlockSpec((2, LANES), lambda n: (0, 0)),
            pl.BlockSpec((LANES, 2), lambda n: (0, 0)),
            pl.BlockSpec((1, GDN_DV), lambda n: (0, 0)),
        ],
        out_specs=pl.BlockSpec((R, GDN_V), lambda n: (n, 0)),
        out_shape=jax.ShapeDtypeStruct((L, GDN_V), BF16),
        scratch_shapes=[
            pltpu.VMEM((GDN_HEADS, GDN_DK, GDN_DV), F32),
            pltpu.VMEM((R + 8, 2 * GDN_QK + GDN_V), F32),
        ],
        compiler_params=pltpu.CompilerParams(
            dimension_semantics=("arbitrary",),
            vmem_limit_bytes=VMEM_LIMIT),
        name="gdn",
    )(p_big, p_big, p_big, p_big, p_sm, p_smt, conv_w, a_row, a_col, norm_w)


def _outproj_kernel(x_ref, oa_ref, ob_ref, wa_ref, wb_ref, y_ref):
    y_ref[...] = (x_ref[...] + _dot(oa_ref[...], wa_ref[...])
                  + _dot(ob_ref[...], wb_ref[...]))


def _outproj(x2, o_a, o_b, w_out_b, tm):
    L, D = x2.shape
    return pl.pallas_call(
        _outproj_kernel,
        grid=(L // tm,),
        in_specs=[
            pl.BlockSpec((tm, D), lambda i: (i, 0)),
            pl.BlockSpec((tm, GLA_V), lambda i: (i, 0)),
            pl.BlockSpec((tm, GDN_V), lambda i: (i, 0)),
            pl.BlockSpec((GLA_V, D), lambda i: (0, 0)),
            pl.BlockSpec((GDN_V, D), lambda i: (GLA_V // GDN_V, 0)),
        ],
        out_specs=pl.BlockSpec((tm, D), lambda i: (i, 0)),
        out_shape=jax.ShapeDtypeStruct((L, D), F32),
        compiler_params=pltpu.CompilerParams(
            dimension_semantics=("arbitrary",),
            vmem_limit_bytes=VMEM_LIMIT),
        name="outproj",
    )(x2, o_a, o_b, w_out_b, w_out_b)


def _ffn_kernel(x_ref, fnw_ref, wg_ref, wu_ref, wd_ref, onw_ref, y_ref, hf_ref, acc_ref):
    f = pl.program_id(1)

    @pl.when(f == 0)
    def _():
        x = x_ref[...]
        ms = jnp.mean(x * x, axis=-1, keepdims=True)
        hf_ref[...] = (x * lax.rsqrt(ms + EPS) * fnw_ref[...]).astype(BF16)
        acc_ref[...] = jnp.zeros_like(acc_ref)

    hf = hf_ref[...]
    g = _dot(hf, wg_ref[...])
    u = _dot(hf, wu_ref[...])
    acc_ref[...] += _dot((_silu(g) * u).astype(BF16), wd_ref[...])

    @pl.when(f == pl.num_programs(1) - 1)
    def _():
        r = x_ref[...] + acc_ref[...]
        ms = jnp.mean(r * r, axis=-1, keepdims=True)
        y_ref[...] = r * lax.rsqrt(ms + EPS) * onw_ref[...]


def _ffn(x1, ffn_norm_w, wg, wu, wd, final_norm_w, tm, tf):
    L, D = x1.shape
    F = wg.shape[1]
    return pl.pallas_call(
        _ffn_kernel,
        grid=(L // tm, F // tf),
        in_specs=[
            pl.BlockSpec((tm, D), lambda i, f: (i, 0)),
            pl.BlockSpec((1, D), lambda i, f: (0, 0)),
            pl.BlockSpec((D, tf), lambda i, f: (0, f)),
            pl.BlockSpec((D, tf), lambda i, f: (0, f)),
            pl.BlockSpec((tf, D), lambda i, f: (f, 0)),
            pl.BlockSpec((1, D), lambda i, f: (0, 0)),
        ],
        out_specs=pl.BlockSpec((tm, D), lambda i, f: (i, 0)),
        out_shape=jax.ShapeDtypeStruct((L, D), F32),
        scratch_shapes=[pltpu.VMEM((tm, D), BF16), pltpu.VMEM((tm, D), F32)],
        compiler_params=pltpu.CompilerParams(
            dimension_semantics=("arbitrary", "arbitrary"),
            vmem_limit_bytes=VMEM_LIMIT),
        name="ffn",
    )(x1, ffn_norm_w, wg, wu, wd, final_norm_w)


def _hi_lo(w):
    hi = w.astype(BF16)
    lo = (w - hi.astype(F32)).astype(BF16)
    return jnp.stack([hi, lo])


def _diag_selector():
    s_of_row = np.arange(SUB_BLOCK * GLA_DK) // GLA_DK
    n_mod = np.arange(CHUNK) % SUB_BLOCK
    return jnp.asarray(s_of_row[:, None] == n_mod[None, :], dtype=BF16)


def _pick(n, candidates):
    for c in candidates:
        if n % c == 0:
            return c
    raise ValueError(f"no tile in {candidates} divides {n}")


def kernel(x, attn_norm_w, w_in, gla_gate_w2, gla_gate_b, gla_norm_w, gdn_conv_w,
           gdn_a_log, gdn_dt_bias, gdn_norm_w, w_out, ffn_norm_w, w_gate, w_up,
           w_down, final_norm_w):
    bsz, seq, d_model = x.shape
    assert bsz == 1 and seq % CHUNK == 0
    x2 = x.reshape(seq, d_model).astype(F32)

    sizes = (GLA_QK, GLA_QK, GLA_V, GLA_GATE_RANK, GLA_V,
             GDN_QK, GDN_QK, GDN_V, GDN_HEADS, GDN_HEADS, GDN_V)
    offs = np.concatenate([[0], np.cumsum(sizes)])
    assert w_in.shape == (d_model, offs[-1])
    col = lambda i: w_in[:, offs[i]:offs[i + 1]].astype(F32)
    w_big = jnp.concatenate([col(i) for i in (0, 1, 2, 4, 5, 6, 7, 10)], axis=1).astype(BF16)
    n_small = GLA_GATE_RANK + 2 * GDN_HEADS
    w_small = jnp.concatenate([col(3), col(8), col(9)], axis=1)
    w_small = jnp.pad(w_small, ((0, 0), (0, LANES - n_small)))
    w_sm = _hi_lo(w_small)
    w_smt = _hi_lo(w_small.T)

    tm1 = _pick(seq, (1024, 512, 256, 128, 64))
    p_big, p_sm, p_smt = _inproj(x2, attn_norm_w.reshape(1, -1).astype(F32),
                                 w_big, w_sm, w_smt, tm1, 512)

    cps = 2 if seq % (2 * CHUNK) == 0 else 1

    w2_pad = jnp.pad(gla_gate_w2.astype(F32), ((0, LANES - GLA_GATE_RANK), (0, 0)))
    o_a = _gla(p_big, p_sm, _hi_lo(w2_pad), gla_gate_b.reshape(1, -1).astype(F32),
               gla_norm_w.reshape(1, -1).astype(F32), _diag_selector(), cps)

    a_log_pad = jnp.zeros((LANES,), F32).at[COL_A:COL_A + GDN_HEADS].set(gdn_a_log.astype(F32))
    dt_pad = jnp.zeros((LANES,), F32).at[COL_A:COL_A + GDN_HEADS].set(gdn_dt_bias.astype(F32))
    a_row = jnp.stack([a_log_pad, dt_pad])
    o_b = _gdn(p_big, p_sm, p_smt, gdn_conv_w.astype(F32), a_row, a_row.T,
               gdn_norm_w.reshape(1, -1).astype(F32), cps)

    tm4 = _pick(seq, (512, 256, 128, 64))
    x1 = _outproj(x2, o_a, o_b, w_out.astype(BF16), tm4)

    y = _ffn(x1, ffn_norm_w.reshape(1, -1).astype(F32), w_gate.astype(BF16),
             w_up.astype(BF16), w_down.astype(BF16),
             final_norm_w.reshape(1, -1).astype(F32), tm4, 512)
    return y.reshape(bsz, seq, d_model).astype(x.dtype)
```

```python
import functools

import numpy as np
import jax
import jax.numpy as jnp
from jax import lax
from jax.experimental import pallas as pl
from jax.experimental.pallas import tpu as pltpu

F32 = jnp.float32
BF16 = jnp.bfloat16

EPS = 1e-6
CHUNK = 64

GLA_HEADS = 4
GLA_DK = 128
GLA_DV = 256
GLA_GATE_RANK = 16
GLA_GATE_TAU = 16.0
GLA_QK = GLA_HEADS * GLA_DK
GLA_V = GLA_HEADS * GLA_DV

GDN_HEADS = 8
GDN_DK = 128
GDN_DV = 128
GDN_CONV = 4
GDN_QK = GDN_HEADS * GDN_DK
GDN_V = GDN_HEADS * GDN_DV

LANES = 128
SUB_BLOCK = 16
VMEM_LIMIT = 48 * 1024 * 1024

COL_LR = 0
COL_A = GLA_GATE_RANK
COL_B = GLA_GATE_RANK + GDN_HEADS


def _dot(a, b):
    return jnp.dot(a, b, preferred_element_type=F32)


def _dot_nt(a, b):
    return lax.dot_general(a, b, (((1,), (1,)), ((), ())), preferred_element_type=F32)


def _dot_tn(a, b):
    return lax.dot_general(a, b, (((0,), (0,)), ((), ())), preferred_element_type=F32)


def _split3(a):
    hi = a.astype(BF16)
    r1 = a - hi.astype(F32)
    mid = r1.astype(BF16)
    lo = (r1 - mid.astype(F32)).astype(BF16)
    return hi, mid, lo


def _dot_exact_rhs(a, b_exact):
    hi, mid, lo = _split3(a)
    return _dot(hi, b_exact) + _dot(mid, b_exact) + _dot(lo, b_exact)


def _dot_exact_lhs(a_exact, b):
    hi, mid, lo = _split3(b)
    return _dot(a_exact, hi) + _dot(a_exact, mid) + _dot(a_exact, lo)


def _sigmoid(x):
    return 1.0 / (1.0 + jnp.exp(-x))


def _silu(x):
    return x * _sigmoid(x)


def _softplus(x):
    return jnp.maximum(x, 0.0) + jnp.log1p(jnp.exp(-jnp.abs(x)))


def _log_sigmoid(x):
    return -_softplus(-x)


def _inproj_kernel(x_ref, nw_ref, wbig_ref, wsm_ref, wsmt_ref,
                   pbig_ref, psm_ref, psmt_ref, h_ref):
    j = pl.program_id(1)

    @pl.when(j == 0)
    def _():
        x = x_ref[...]
        ms = jnp.mean(x * x, axis=-1, keepdims=True)
        h = x * lax.rsqrt(ms + EPS) * nw_ref[...]
        hb = h.astype(BF16)
        h_ref[...] = hb
        hl = (h - hb.astype(F32)).astype(BF16)
        psm_ref[...] = (_dot(hb, wsm_ref[0]) + _dot(hl, wsm_ref[0])
                        + _dot(hb, wsm_ref[1]))
        psmt_ref[...] = (_dot_nt(wsmt_ref[0], hb) + _dot_nt(wsmt_ref[0], hl)
                         + _dot_nt(wsmt_ref[1], hb))

    pbig_ref[...] = _dot(h_ref[...], wbig_ref[...]).astype(BF16)


def _inproj(x2, norm_w, w_big, w_sm, w_smt, tm, tn):
    L, D = x2.shape
    nbig = w_big.shape[1]
    grid = (L // tm, nbig // tn)
    return pl.pallas_call(
        _inproj_kernel,
        grid=grid,
        in_specs=[
            pl.BlockSpec((tm, D), lambda i, j: (i, 0)),
            pl.BlockSpec((1, D), lambda i, j: (0, 0)),
            pl.BlockSpec((D, tn), lambda i, j: (0, j)),
            pl.BlockSpec((2, D, LANES), lambda i, j: (0, 0, 0)),
            pl.BlockSpec((2, LANES, D), lambda i, j: (0, 0, 0)),
        ],
        out_specs=[
            pl.BlockSpec((tm, tn), lambda i, j: (i, j)),
            pl.BlockSpec((tm, LANES), lambda i, j: (i, 0)),
            pl.BlockSpec((LANES, tm), lambda i, j: (0, i)),
        ],
        out_shape=[
            jax.ShapeDtypeStruct((L, nbig), BF16),
            jax.ShapeDtypeStruct((L, LANES), F32),
            jax.ShapeDtypeStruct((LANES, L), F32),
        ],
        scratch_shapes=[pltpu.VMEM((tm, D), BF16)],
        compiler_params=pltpu.CompilerParams(
            dimension_semantics=("arbitrary", "arbitrary"),
            vmem_limit_bytes=VMEM_LIMIT),
        name="inproj",
    )(x2, norm_w, w_big, w_sm, w_smt)


def _ones_where(mask):
    return jnp.where(mask, 1.0, 0.0).astype(BF16)


def _chunk_tril(rows, upper=False):
    ti = lax.broadcasted_iota(jnp.int32, (rows, rows), 0)
    si = lax.broadcasted_iota(jnp.int32, (rows, rows), 1)
    same = (ti >> 6) == (si >> 6)
    return same & ((ti <= si) if upper else (ti >= si))


def _gla_chunk(q_ref, k_ref, v_ref, gate_ref, nw_ref, sel_ref, o_ref, st_ref,
               c_scr, k_scr, r0):
    C = CHUNK
    rows = pl.ds(r0, C)
    row = lax.broadcasted_iota(jnp.int32, (C, GLA_DK), 0)
    ti = lax.broadcasted_iota(jnp.int32, (C, C), 0)
    si = lax.broadcasted_iota(jnp.int32, (C, C), 1)
    in_hi32 = row >= 32
    right16 = (row & 31) >= 16
    row_lt32 = row < 32
    same_pair = (ti >> 5) == (si >> 5)
    same_blk = (ti >> 4) == (si >> 4)
    trow = lax.broadcasted_iota(jnp.int32, (SUB_BLOCK, GLA_DK), 0)

    for h in range(GLA_HEADS):
        ks = slice(h * GLA_DK, (h + 1) * GLA_DK)
        vs = slice(h * GLA_DV, (h + 1) * GLA_DV)
        q = q_ref[rows, ks].astype(F32) * (GLA_DK ** -0.5)
        k = k_scr[rows, ks]
        v = v_ref[rows, vs]
        c = c_scr[rows, ks]

        ref = c_scr[pl.ds(r0 + 31, 1), ks]
        e = jnp.exp(jnp.where(in_hi32, c - ref, ref - c))
        qt = jnp.where(in_hi32, q * e, 0.0).astype(BF16)
        kt = jnp.where(in_hi32, 0.0, k * e).astype(BF16)
        a = _dot_nt(qt, kt)
        ref = jnp.where(row_lt32, c_scr[pl.ds(r0 + 15, 1), ks], c_scr[pl.ds(r0 + 47, 1), ks])
        e = jnp.exp(jnp.where(right16, c - ref, ref - c))
        qt = jnp.where(right16, q * e, 0.0).astype(BF16)
        kt = jnp.where(right16, 0.0, k * e).astype(BF16)
        a = a + jnp.where(same_pair, _dot_nt(qt, kt), 0.0)
        blocks = []
        for b in range(C // SUB_BLOCK):
            b0 = b * SUB_BLOCK
            cb = c[b0:b0 + SUB_BLOCK]
            qb = q[b0:b0 + SUB_BLOCK]
            parts = []
            for s in range(SUB_BLOCK):
                c_s = c_scr[pl.ds(r0 + b0 + s, 1), ks]
                k_s = k_scr[pl.ds(r0 + b0 + s, 1), ks]
                e = jnp.exp(jnp.where(trow >= s, cb - c_s, -jnp.inf))
                parts.append(((qb * e) * k_s).astype(BF16))
            blocks.append(jnp.concatenate(parts, axis=1))
        p2 = jnp.concatenate(blocks, axis=0)
        a = a + jnp.where(same_blk, _dot(p2, sel_ref[...]), 0.0)

        st = st_ref[h]
        c_last = c_scr[pl.ds(r0 + C - 1, 1), ks]
        qd = (q * jnp.exp(c)).astype(BF16)
        o = _dot(a.astype(BF16), v) + _dot_nt(qd, st.astype(BF16))
        kd = (k * jnp.exp(c_last - c)).astype(BF16)
        st_ref[h] = jnp.exp(c_last) * st + _dot_tn(v, kd)

        ms = jnp.mean(o * o, axis=-1, keepdims=True)
        on = o * lax.rsqrt(ms + EPS) * nw_ref[...]
        o_ref[rows, vs] = (on * _silu(gate_ref[rows, vs].astype(F32))).astype(BF16)


def _gla_kernel(q_ref, k_ref, v_ref, gate_ref, psm_ref, w2_ref, gb_ref, nw_ref,
                sel_ref, o_ref, st_ref, c_scr, k_scr, *, chunks_per_step):
    @pl.when(pl.program_id(0) == 0)
    def _():
        st_ref[...] = jnp.zeros_like(st_ref)

    a_hi, a_mid, _ = _split3(psm_ref[...])
    z = (_dot(a_hi, w2_ref[0]) + _dot(a_mid, w2_ref[0]) + _dot(a_hi, w2_ref[1])
         + gb_ref[...])
    log_a = _log_sigmoid(z) * (1.0 / GLA_GATE_TAU)
    tril = _ones_where(_chunk_tril(CHUNK * chunks_per_step))
    c_scr[...] = _dot_exact_lhs(tril, log_a)
    k_scr[...] = k_ref[...].astype(F32)

    for ci in range(chunks_per_step):
        _gla_chunk(q_ref, k_ref, v_ref, gate_ref, nw_ref, sel_ref, o_ref, st_ref,
                   c_scr, k_scr, ci * CHUNK)


def _gla(p_big, p_sm, w2, gate_b, norm_w, sel, chunks_per_step):
    L = p_big.shape[0]
    R = CHUNK * chunks_per_step
    kern = functools.partial(_gla_kernel, chunks_per_step=chunks_per_step)
    return pl.pallas_call(
        kern,
        grid=(L // R,),
        in_specs=[
            pl.BlockSpec((R, GLA_QK), lambda n: (n, 0)),
            pl.BlockSpec((R, GLA_QK), lambda n: (n, 1)),
            pl.BlockSpec((R, GLA_V), lambda n: (n, 1)),
            pl.BlockSpec((R, GLA_V), lambda n: (n, 2)),
            pl.BlockSpec((R, LANES), lambda n: (n, 0)),
            pl.BlockSpec((2, LANES, GLA_QK), lambda n: (0, 0, 0)),
            pl.BlockSpec((1, GLA_QK), lambda n: (0, 0)),
            pl.BlockSpec((1, GLA_DV), lambda n: (0, 0)),
            pl.BlockSpec((SUB_BLOCK * GLA_DK, CHUNK), lambda n: (0, 0)),
        ],
        out_specs=pl.BlockSpec((R, GLA_V), lambda n: (n, 0)),
        out_shape=jax.ShapeDtypeStruct((L, GLA_V), BF16),
        scratch_shapes=[
            pltpu.VMEM((GLA_HEADS, GLA_DV, GLA_DK), F32),
            pltpu.VMEM((R, GLA_QK), F32),
            pltpu.VMEM((R, GLA_QK), F32),
        ],
        compiler_params=pltpu.CompilerParams(
            dimension_semantics=("arbitrary",),
            vmem_limit_bytes=VMEM_LIMIT),
        name="gla",
    )(p_big, p_big, p_big, p_big, p_sm, w2, gate_b, norm_w, sel)


def _gdn_chunks(z_ref, nw_ref, o_ref, s_ref, qkv, cum_col, cum_row, beta_col,
                chunks_per_step):
    C = CHUNK
    ti = lax.broadcasted_iota(jnp.int32, (C, C), 0)
    si = lax.broadcasted_iota(jnp.int32, (C, C), 1)
    incl = ti >= si
    strict = ti > si
    eye = jnp.where(ti == si, 1.0, 0.0)
    tasks = [(ci, h) for ci in range(chunks_per_step) for h in range(GDN_HEADS)]

    pw, t_inv, attn, rhs, qd, kd, g_last = {}, {}, {}, {}, {}, {}, {}
    for t in tasks:
        ci, h = t
        r0 = ci * C
        cq = qkv[r0:r0 + C, h * GDN_DK:(h + 1) * GDN_DK]
        ck = qkv[r0:r0 + C, GDN_QK + h * GDN_DK:GDN_QK + (h + 1) * GDN_DK]
        cv = qkv[r0:r0 + C, 2 * GDN_QK + h * GDN_DV:2 * GDN_QK + (h + 1) * GDN_DV]
        qn = cq * lax.rsqrt(jnp.sum(cq * cq, axis=-1, keepdims=True) + EPS) * (GDN_DK ** -0.5)
        kn = ck * lax.rsqrt(jnp.sum(ck * ck, axis=-1, keepdims=True) + EPS)
        cc = cum_col[r0:r0 + C, COL_A + h:COL_A + h + 1]
        cr = cum_row[COL_A + h:COL_A + h + 1, r0:r0 + C]
        beta = beta_col[r0:r0 + C, COL_B + h:COL_B + h + 1]
        gamma = jnp.exp(jnp.where(incl, cc - cr, -jnp.inf))
        kb = kn * beta
        knb = kn.astype(BF16)
        n_low = jnp.where(strict, _dot_nt(kb.astype(BF16), knb) * gamma, 0.0)
        attn[t] = (_dot_nt(qn.astype(BF16), knb) * gamma).astype(BF16)
        pw[t] = n_low
        t_inv[t] = eye - n_low
        e_c = jnp.exp(cc)
        c_last = cc[C - 1:C, :]
        rhs[t] = jnp.concatenate([cv * beta, kb * e_c], axis=1).astype(BF16)
        qd[t] = (qn * e_c).astype(BF16)
        kd[t] = (kn * jnp.exp(c_last - cc)).astype(BF16)
        g_last[t] = jnp.exp(c_last)

    for _ in range(5):
        for t in tasks:
            pwb = pw[t].astype(BF16)
            pw[t] = _dot(pwb, pwb)
        for t in tasks:
            t_inv[t] = t_inv[t] + _dot(t_inv[t].astype(BF16), pw[t].astype(BF16))
    sol = {t: _dot(t_inv[t].astype(BF16), rhs[t]) for t in tasks}

    for ci in range(chunks_per_step):
        rows = pl.ds(ci * C, C)
        heads = [(ci, h) for h in range(GDN_HEADS)]
        s_old = {t: s_ref[t[1]] for t in heads}
        sb = {t: s_old[t].astype(BF16) for t in heads}
        vnb = {t: (sol[t][:, :GDN_DV] - _dot(sol[t][:, GDN_DV:].astype(BF16), sb[t])).astype(BF16)
               for t in heads}
        for t in heads:
            h = t[1]
            hs = slice(h * GDN_DK, (h + 1) * GDN_DK)
            o = _dot(qd[t], sb[t]) + _dot(attn[t], vnb[t])
            s_ref[h] = g_last[t] * s_old[t] + _dot_tn(kd[t], vnb[t])
            ms = jnp.mean(o * o, axis=-1, keepdims=True)
            on = o * lax.rsqrt(ms + EPS) * nw_ref[...]
            o_ref[rows, hs] = (on * _silu(z_ref[rows, hs].astype(F32))).astype(BF16)


def _gdn_kernel(q_ref, k_ref, v_ref, z_ref, psm_ref, psmt_ref, cw_ref, arow_ref,
                acol_ref, nw_ref, o_ref, s_ref, ext_scr, *, chunks_per_step):
    R = CHUNK * chunks_per_step

    @pl.when(pl.program_id(0) == 0)
    def _():
        s_ref[...] = jnp.zeros_like(s_ref)
        ext_scr[pl.ds(0, 8), :] = jnp.zeros((8, ext_scr.shape[1]), F32)

    psm = psm_ref[...]
    g_col = -jnp.exp(arow_ref[0:1, :]) * _softplus(psm + arow_ref[1:2, :])
    beta_col = _sigmoid(psm)
    cum_col = _dot_exact_lhs(_ones_where(_chunk_tril(R)), g_col)
    g_row = -jnp.exp(acol_ref[:, 0:1]) * _softplus(psmt_ref[...] + acol_ref[:, 1:2])
    cum_row = _dot_exact_rhs(g_row, _ones_where(_chunk_tril(R, upper=True)))

    ext_scr[pl.ds(8, R), 0:GDN_QK] = q_ref[...].astype(F32)
    ext_scr[pl.ds(8, R), GDN_QK:2 * GDN_QK] = k_ref[...].astype(F32)
    ext_scr[pl.ds(8, R), 2 * GDN_QK:] = v_ref[...].astype(F32)
    conv = cw_ref[GDN_CONV - 1:GDN_CONV, :] * ext_scr[pl.ds(8, R), :]
    for j in range(GDN_CONV - 1):
        conv = conv + cw_ref[j:j + 1, :] * ext_scr[pl.ds(8 - (GDN_CONV - 1) + j, R), :]
    ext_scr[pl.ds(0, 8), :] = ext_scr[pl.ds(R, 8), :]
    qkv = _silu(conv)

    _gdn_chunks(z_ref, nw_ref, o_ref, s_ref, qkv, cum_col, cum_row, beta_col, chunks_per_step)


def _gdn(p_big, p_sm, p_smt, conv_w, a_row, a_col, norm_w, chunks_per_step):
    L = p_big.shape[0]
    R = CHUNK * chunks_per_step
    base = (GLA_QK * 2 + GLA_V * 2) // GDN_QK
    kern = functools.partial(_gdn_kernel, chunks_per_step=chunks_per_step)
    return pl.pallas_call(
        kern,
        grid=(L // R,),
        in_specs=[
            pl.BlockSpec((R, GDN_QK), lambda n: (n, base)),
            pl.BlockSpec((R, GDN_QK), lambda n: (n, base + 1)),
            pl.BlockSpec((R, GDN_V), lambda n: (n, base + 2)),
            pl.BlockSpec((R, GDN_V), lambda n: (n, base + 3)),
            pl.BlockSpec((R, LANES), lambda n: (n, 0)),
            pl.BlockSpec((LANES, R), lambda n: (0, n)),
            pl.BlockSpec((GDN_CONV, 2 * GDN_QK + GDN_V), lambda n: (0, 0)),
            pl.BlockSpec((2, LANES), lambda n: (0, 0)),
            pl.BlockSpec((LANES, 2), lambda n: (0, 0)),
            pl.BlockSpec((1, GDN_DV), lambda n: (0, 0)),
        ],
        out_specs=pl.BlockSpec((R, GDN_V), lambda n: (n, 0)),
        out_shape=jax.ShapeDtypeStruct((L, GDN_V), BF16),
        scratch_shapes=[
            pltpu.VMEM((GDN_HEADS, GDN_DK, GDN_DV), F32),
            pltpu.VMEM((R + 8, 2 * GDN_QK + GDN_V), F32),
        ],
        compiler_params=pltpu.CompilerParams(
            dimension_semantics=("arbitrary",),
            vmem_limit_bytes=VMEM_LIMIT),
        name="gdn",
    )(p_big, p_big, p_big, p_big, p_sm, p_smt, conv_w, a_row, a_col, norm_w)


def _outproj_kernel(x_ref, oa_ref, ob_ref, wa_ref, wb_ref, y_ref):
    y_ref[...] = (x_ref[...] + _dot(oa_ref[...], wa_ref[...])
                  + _dot(ob_ref[...], wb_ref[...]))


def _outproj(x2, o_a, o_b, w_out_b, tm):
    L, D = x2.shape
    return pl.pallas_call(
        _outproj_kernel,
        grid=(L // tm,),
        in_specs=[
            pl.BlockSpec((tm, D), lambda i: (i, 0)),
            pl.BlockSpec((tm, GLA_V), lambda i: (i, 0)),
            pl.BlockSpec((tm, GDN_V), lambda i: (i, 0)),
            pl.BlockSpec((GLA_V, D), lambda i: (0, 0)),
            pl.BlockSpec((GDN_V, D), lambda i: (GLA_V // GDN_V, 0)),
        ],
        out_specs=pl.BlockSpec((tm, D), lambda i: (i, 0)),
        out_shape=jax.ShapeDtypeStruct((L, D), F32),
        compiler_params=pltpu.CompilerParams(
            dimension_semantics=("arbitrary",),
            vmem_limit_bytes=VMEM_LIMIT),
        name="outproj",
    )(x2, o_a, o_b, w_out_b, w_out_b)


def _ffn_kernel(x_ref, fnw_ref, wg_ref, wu_ref, wd_ref, onw_ref, y_ref, hf_ref, acc_ref):
    f = pl.program_id(1)

    @pl.when(f == 0)
    def _():
        x = x_ref[...]
        ms = jnp.mean(x * x, axis=-1, keepdims=True)
        hf_ref[...] = (x * lax.rsqrt(ms + EPS) * fnw_ref[...]).astype(BF16)
        acc_ref[...] = jnp.zeros_like(acc_ref)

    hf = hf_ref[...]
    g = _dot(hf, wg_ref[...])
    u = _dot(hf, wu_ref[...])
    acc_ref[...] += _dot((_silu(g) * u).astype(BF16), wd_ref[...])

    @pl.when(f == pl.num_programs(1) - 1)
    def _():
        r = x_ref[...] + acc_ref[...]
        ms = jnp.mean(r * r, axis=-1, keepdims=True)
        y_ref[...] = r * lax.rsqrt(ms + EPS) * onw_ref[...]


def _ffn(x1, ffn_norm_w, wg, wu, wd, final_norm_w, tm, tf):
    L, D = x1.shape
    F = wg.shape[1]
    return pl.pallas_call(
        _ffn_kernel,
        grid=(L // tm, F // tf),
        in_specs=[
            pl.BlockSpec((tm, D), lambda i, f: (i, 0)),
            pl.BlockSpec((1, D), lambda i, f: (0, 0)),
            pl.BlockSpec((D, tf), lambda i, f: (0, f)),
            pl.BlockSpec((D, tf), lambda i, f: (0, f)),
            pl.BlockSpec((tf, D), lambda i, f: (f, 0)),
            pl.BlockSpec((1, D), lambda i, f: (0, 0)),
        ],
        out_specs=pl.BlockSpec((tm, D), lambda i, f: (i, 0)),
        out_shape=jax.ShapeDtypeStruct((L, D), F32),
        scratch_shapes=[pltpu.VMEM((tm, D), BF16), pltpu.VMEM((tm, D), F32)],
        compiler_params=pltpu.CompilerParams(
            dimension_semantics=("arbitrary", "arbitrary"),
            vmem_limit_bytes=VMEM_LIMIT),
        name="ffn",
    )(x1, ffn_norm_w, wg, wu, wd, final_norm_w)


def _hi_lo(w):
    hi = w.astype(BF16)
    lo = (w - hi.astype(F32)).astype(BF16)
    return jnp.stack([hi, lo])


def _diag_selector():
    s_of_row = np.arange(SUB_BLOCK * GLA_DK) // GLA_DK
    n_mod = np.arange(CHUNK) % SUB_BLOCK
    return jnp.asarray(s_of_row[:, None] == n_mod[None, :], dtype=BF16)


def _pick(n, candidates):
    for c in candidates:
        if n % c == 0:
            return c
    raise ValueError(f"no tile in {candidates} divides {n}")


def kernel(x, attn_norm_w, w_in, gla_gate_w2, gla_gate_b, gla_norm_w, gdn_conv_w,
           gdn_a_log, gdn_dt_bias, gdn_norm_w, w_out, ffn_norm_w, w_gate, w_up,
           w_down, final_norm_w):
    bsz, seq, d_model = x.shape
    assert bsz == 1 and seq % CHUNK == 0
    x2 = x.reshape(seq, d_model).astype(F32)

    sizes = (GLA_QK, GLA_QK, GLA_V, GLA_GATE_RANK, GLA_V,
             GDN_QK, GDN_QK, GDN_V, GDN_HEADS, GDN_HEADS, GDN_V)
    offs = np.concatenate([[0], np.cumsum(sizes)])
    assert w_in.shape == (d_model, offs[-1])
    col = lambda i: w_in[:, offs[i]:offs[i + 1]].astype(F32)
    w_big = jnp.concatenate([col(i) for i in (0, 1, 2, 4, 5, 6, 7, 10)], axis=1).astype(BF16)
    n_small = GLA_GATE_RANK + 2 * GDN_HEADS
    w_small = jnp.concatenate([col(3), col(8), col(9)], axis=1)
    w_small = jnp.pad(w_small, ((0, 0), (0, LANES - n_small)))
    w_sm = _hi_lo(w_small)
    w_smt = _hi_lo(w_small.T)

    tm1 = _pick(seq, (1024, 512, 256, 128, 64))
    p_big, p_sm, p_smt = _inproj(x2, attn_norm_w.reshape(1, -1).astype(F32),
                                 w_big, w_sm, w_smt, tm1, 512)

    cps = 2 if seq % (2 * CHUNK) == 0 else 1

    w2_pad = jnp.pad(gla_gate_w2.astype(F32), ((0, LANES - GLA_GATE_RANK), (0, 0)))
    o_a = _gla(p_big, p_sm, _hi_lo(w2_pad), gla_gate_b.reshape(1, -1).astype(F32),
               gla_norm_w.reshape(1, -1).astype(F32), _diag_selector(), cps)

    a_log_pad = jnp.zeros((LANES,), F32).at[COL_A:COL_A + GDN_HEADS].set(gdn_a_log.astype(F32))
    dt_pad = jnp.zeros((LANES,), F32).at[COL_A:COL_A + GDN_HEADS].set(gdn_dt_bias.astype(F32))
    a_row = jnp.stack([a_log_pad, dt_pad])
    o_b = _gdn(p_big, p_sm, p_smt, gdn_conv_w.astype(F32), a_row, a_row.T,
               gdn_norm_w.reshape(1, -1).astype(F32), cps)

    tm4 = _pick(seq, (512, 256, 128, 64))
    x1 = _outproj(x2, o_a, o_b, w_out.astype(BF16), tm4)

    y = _ffn(x1, ffn_norm_w.reshape(1, -1).astype(F32), w_gate.astype(BF16),
             w_up.astype(BF16), w_down.astype(BF16),
             final_norm_w.reshape(1, -1).astype(F32), tm4, 512)
    return y.reshape(bsz, seq, d_model).astype(x.dtype)
```

```python
import functools

import numpy as np
import jax
import jax.numpy as jnp
from jax import lax
from jax.experimental import pallas as pl
from jax.experimental.pallas import tpu as pltpu

F32 = jnp.float32
BF16 = jnp.bfloat16

EPS = 1e-6
CHUNK = 64

GLA_HEADS = 4
GLA_DK = 128
GLA_DV = 256
GLA_GATE_RANK = 16
GLA_GATE_TAU = 16.0
GLA_QK = GLA_HEADS * GLA_DK
GLA_V = GLA_HEADS * GLA_DV

GDN_HEADS = 8
GDN_DK = 128
GDN_DV = 128
GDN_CONV = 4
GDN_QK = GDN_HEADS * GDN_DK
GDN_V = GDN_HEADS * GDN_DV

LANES = 128
SUB_BLOCK = 16
VMEM_LIMIT = 48 * 1024 * 1024

COL_LR = 0
COL_A = GLA_GATE_RANK
COL_B = GLA_GATE_RANK + GDN_HEADS


def _dot(a, b):
    return jnp.dot(a, b, preferred_element_type=F32)


def _dot_nt(a, b):
    return lax.dot_general(a, b, (((1,), (1,)), ((), ())), preferred_element_type=F32)


def _dot_tn(a, b):
    return lax.dot_general(a, b, (((0,), (0,)), ((), ())), preferred_element_type=F32)


def _split3(a):
    hi = a.astype(BF16)
    r1 = a - hi.astype(F32)
    mid = r1.astype(BF16)
    lo = (r1 - mid.astype(F32)).astype(BF16)
    return hi, mid, lo


def _dot_exact_rhs(a, b_exact):
    hi, mid, lo = _split3(a)
    return _dot(hi, b_exact) + _dot(mid, b_exact) + _dot(lo, b_exact)


def _dot_exact_lhs(a_exact, b):
    hi, mid, lo = _split3(b)
    return _dot(a_exact, hi) + _dot(a_exact, mid) + _dot(a_exact, lo)


def _sigmoid(x):
    return 1.0 / (1.0 + jnp.exp(-x))


def _silu(x):
    return x * _sigmoid(x)


def _softplus(x):
    return jnp.maximum(x, 0.0) + jnp.log1p(jnp.exp(-jnp.abs(x)))


def _log_sigmoid(x):
    return -_softplus(-x)


def _inproj_kernel(x_ref, nw_ref, wbig_ref, wsmt_ref,
                   pbig_ref, psm_ref, psmt_ref, h_ref):
    j = pl.program_id(1)

    @pl.when(j == 0)
    def _():
        x = x_ref[...]
        ms = jnp.mean(x * x, axis=-1, keepdims=True)
        h = x * lax.rsqrt(ms + EPS) * nw_ref[...]
        hb = h.astype(BF16)
        h_ref[...] = hb
        hl = (h - hb.astype(F32)).astype(BF16)
        r = _dot_nt(wsmt_ref[...], hb)
        pt = r[:LANES] + r[LANES:] + _dot_nt(wsmt_ref[pl.ds(0, LANES), :], hl)
        psmt_ref[...] = pt
        psm_ref[...] = pt.T

    pbig_ref[...] = _dot(h_ref[...], wbig_ref[...]).astype(BF16)


def _inproj(x2, norm_w, w_big, w_smt, tm, tn):
    L, D = x2.shape
    nbig = w_big.shape[1]
    grid = (L // tm, nbig // tn)
    return pl.pallas_call(
        _inproj_kernel,
        grid=grid,
        in_specs=[
            pl.BlockSpec((tm, D), lambda i, j: (i, 0)),
            pl.BlockSpec((1, D), lambda i, j: (0, 0)),
            pl.BlockSpec((D, tn), lambda i, j: (0, j)),
            pl.BlockSpec((2 * LANES, D), lambda i, j: (0, 0)),
        ],
        out_specs=[
            pl.BlockSpec((tm, tn), lambda i, j: (i, j)),
            pl.BlockSpec((tm, LANES), lambda i, j: (i, 0)),
            pl.BlockSpec((LANES, tm), lambda i, j: (0, i)),
        ],
        out_shape=[
            jax.ShapeDtypeStruct((L, nbig), BF16),
            jax.ShapeDtypeStruct((L, LANES), F32),
            jax.ShapeDtypeStruct((LANES, L), F32),
        ],
        scratch_shapes=[pltpu.VMEM((tm, D), BF16)],
        compiler_params=pltpu.CompilerParams(
            dimension_semantics=("arbitrary", "arbitrary"),
            vmem_limit_bytes=VMEM_LIMIT),
        name="inproj",
    )(x2, norm_w, w_big, w_smt)


def _ones_where(mask):
    return jnp.where(mask, 1.0, 0.0).astype(BF16)


def _chunk_tril(rows, upper=False):
    ti = lax.broadcasted_iota(jnp.int32, (rows, rows), 0)
    si = lax.broadcasted_iota(jnp.int32, (rows, rows), 1)
    same = (ti >> 6) == (si >> 6)
    return same & ((ti <= si) if upper else (ti >= si))


def _gla_chunk(q_ref, k_ref, v_ref, gate_ref, nw_ref, sel_ref, o_ref, st_ref,
               c_scr, k_scr, r0):
    C = CHUNK
    rows = pl.ds(r0, C)
    row = lax.broadcasted_iota(jnp.int32, (C, GLA_DK), 0)
    ti = lax.broadcasted_iota(jnp.int32, (C, C), 0)
    si = lax.broadcasted_iota(jnp.int32, (C, C), 1)
    in_hi32 = row >= 32
    right16 = (row & 31) >= 16
    row_lt32 = row < 32
    same_pair = (ti >> 5) == (si >> 5)
    same_blk = (ti >> 4) == (si >> 4)
    trow = lax.broadcasted_iota(jnp.int32, (SUB_BLOCK, GLA_DK), 0)

    for h in range(GLA_HEADS):
        ks = slice(h * GLA_DK, (h + 1) * GLA_DK)
        vs = slice(h * GLA_DV, (h + 1) * GLA_DV)
        q = q_ref[rows, ks].astype(F32) * (GLA_DK ** -0.5)
        k = k_scr[rows, ks]
        v = v_ref[rows, vs]
        c = c_scr[rows, ks]

        ref = c_scr[pl.ds(r0 + 31, 1), ks]
        e = jnp.exp(jnp.where(in_hi32, c - ref, ref - c))
        qt = jnp.where(in_hi32, q * e, 0.0).astype(BF16)
        kt = jnp.where(in_hi32, 0.0, k * e).astype(BF16)
        a = _dot_nt(qt, kt)
        ref = jnp.where(row_lt32, c_scr[pl.ds(r0 + 15, 1), ks], c_scr[pl.ds(r0 + 47, 1), ks])
        e = jnp.exp(jnp.where(right16, c - ref, ref - c))
        qt = jnp.where(right16, q * e, 0.0).astype(BF16)
        kt = jnp.where(right16, 0.0, k * e).astype(BF16)
        a = a + jnp.where(same_pair, _dot_nt(qt, kt), 0.0)
        blocks = []
        for b in range(C // SUB_BLOCK):
            b0 = b * SUB_BLOCK
            cb = c[b0:b0 + SUB_BLOCK]
            qb = q[b0:b0 + SUB_BLOCK]
            parts = []
            for s in range(SUB_BLOCK):
                c_s = c_scr[pl.ds(r0 + b0 + s, 1), ks]
                k_s = k_scr[pl.ds(r0 + b0 + s, 1), ks]
                e = jnp.exp(jnp.where(trow >= s, cb - c_s, -jnp.inf))
                parts.append(((qb * e) * k_s).astype(BF16))
            blocks.append(jnp.concatenate(parts, axis=1))
        p2 = jnp.concatenate(blocks, axis=0)
        a = a + jnp.where(same_blk, _dot(p2, sel_ref[...]), 0.0)

        st = st_ref[h]
        c_last = c_scr[pl.ds(r0 + C - 1, 1), ks]
        qd = (q * jnp.exp(c)).astype(BF16)
        o = _dot(a.astype(BF16), v) + _dot_nt(qd, st.astype(BF16))
        kd = (k * jnp.exp(c_last - c)).astype(BF16)
        st_ref[h] = jnp.exp(c_last) * st + _dot_tn(v, kd)

        ms = jnp.mean(o * o, axis=-1, keepdims=True)
        on = o * lax.rsqrt(ms + EPS) * nw_ref[...]
        o_ref[rows, vs] = (on * _silu(gate_ref[rows, vs].astype(F32))).astype(BF16)


def _gla_kernel(q_ref, k_ref, v_ref, gate_ref, psm_ref, w2_ref, gb_ref, nw_ref,
                sel_ref, o_ref, st_ref, c_scr, k_scr, *, chunks_per_step):
    @pl.when(pl.program_id(0) == 0)
    def _():
        st_ref[...] = jnp.zeros_like(st_ref)

    a_hi, a_mid, _ = _split3(psm_ref[...])
    z = (_dot(a_hi, w2_ref[0]) + _dot(a_mid, w2_ref[0]) + _dot(a_hi, w2_ref[1])
         + gb_ref[...])
    log_a = _log_sigmoid(z) * (1.0 / GLA_GATE_TAU)
    tril = _ones_where(_chunk_tril(CHUNK * chunks_per_step))
    c_scr[...] = _dot_exact_lhs(tril, log_a)
    k_scr[...] = k_ref[...].astype(F32)

    for ci in range(chunks_per_step):
        _gla_chunk(q_ref, k_ref, v_ref, gate_ref, nw_ref, sel_ref, o_ref, st_ref,
                   c_scr, k_scr, ci * CHUNK)


def _gla(p_big, p_sm, w2, gate_b, norm_w, sel, chunks_per_step):
    L = p_big.shape[0]
    R = CHUNK * chunks_per_step
    kern = functools.partial(_gla_kernel, chunks_per_step=chunks_per_step)
    return pl.pallas_call(
        kern,
        grid=(L // R,),
        in_specs=[
            pl.BlockSpec((R, GLA_QK), lambda n: (n, 0)),
            pl.BlockSpec((R, GLA_QK), lambda n: (n, 1)),
            pl.BlockSpec((R, GLA_V), lambda n: (n, 1)),
            pl.BlockSpec((R, GLA_V), lambda n: (n, 2)),
            pl.BlockSpec((R, LANES), lambda n: (n, 0)),
            pl.BlockSpec((2, LANES, GLA_QK), lambda n: (0, 0, 0)),
            pl.BlockSpec((1, GLA_QK), lambda n: (0, 0)),
            pl.BlockSpec((1, GLA_DV), lambda n: (0, 0)),
            pl.BlockSpec((SUB_BLOCK * GLA_DK, CHUNK), lambda n: (0, 0)),
        ],
        out_specs=pl.BlockSpec((R, GLA_V), lambda n: (n, 0)),
        out_shape=jax.ShapeDtypeStruct((L, GLA_V), BF16),
        scratch_shapes=[
            pltpu.VMEM((GLA_HEADS, GLA_DV, GLA_DK), F32),
            pltpu.VMEM((R, GLA_QK), F32),
            pltpu.VMEM((R, GLA_QK), F32),
        ],
        compiler_params=pltpu.CompilerParams(
            dimension_semantics=("arbitrary",),
            vmem_limit_bytes=VMEM_LIMIT),
        name="gla",
    )(p_big, p_big, p_big, p_big, p_sm, w2, gate_b, norm_w, sel)


def _gdn_chunks(z_ref, nw_ref, o_ref, s_ref, qkv, cum_col, cum_row, beta_col,
                chunks_per_step):
    C = CHUNK
    ti = lax.broadcasted_iota(jnp.int32, (C, 2 * C), 0)
    li = lax.broadcasted_iota(jnp.int32, (C, 2 * C), 1)
    si = li & (C - 1)
    right = li >= C
    incl = ti >= si
    strict = ti > si
    eye_right = jnp.where((ti == si) & right, 1.0, 0.0)
    tasks = [(ci, h) for ci in range(chunks_per_step) for h in range(GDN_HEADS)]

    wmat, attn, rhs, wq, kd, g_last = {}, {}, {}, {}, {}, {}
    for t in tasks:
        ci, h = t
        r0 = ci * C
        cq = qkv[r0:r0 + C, h * GDN_DK:(h + 1) * GDN_DK]
        ck = qkv[r0:r0 + C, GDN_QK + h * GDN_DK:GDN_QK + (h + 1) * GDN_DK]
        cv = qkv[r0:r0 + C, 2 * GDN_QK + h * GDN_DV:2 * GDN_QK + (h + 1) * GDN_DV]
        qn = cq * lax.rsqrt(jnp.sum(cq * cq, axis=-1, keepdims=True) + EPS) * (GDN_DK ** -0.5)
        kn = ck * lax.rsqrt(jnp.sum(ck * ck, axis=-1, keepdims=True) + EPS)
        cc = cum_col[r0:r0 + C, COL_A + h:COL_A + h + 1]
        cr = cum_row[COL_A + h:COL_A + h + 1, 2 * r0:2 * r0 + 2 * C]
        beta = beta_col[r0:r0 + C, COL_B + h:COL_B + h + 1]
        gamma = jnp.exp(jnp.where(incl, cc - cr, -jnp.inf))
        kb = kn * beta
        knb = kn.astype(BF16)
        x = _dot_nt(jnp.concatenate([qn.astype(BF16), kb.astype(BF16)], axis=0),
                    jnp.concatenate([knb, knb], axis=0))
        attn[t] = (x[:C, :C] * gamma[:, :C]).astype(BF16)
        wmat[t] = jnp.where(strict, x[C:] * gamma, 0.0)
        e_c = jnp.exp(cc)
        c_last = cc[C - 1:C, :]
        rhs1 = jnp.concatenate([cv * beta, kb * e_c], axis=1).astype(BF16)
        rhs[t] = jnp.concatenate([rhs1, rhs1], axis=0)
        wq[t] = (qn * e_c).astype(BF16)
        kd[t] = (kn * jnp.exp(c_last - cc)).astype(BF16)
        g_last[t] = jnp.exp(c_last)

    for t in tasks:
        n2b = wmat[t].astype(BF16)
        wmat[t] = jnp.where(right, eye_right - wmat[t], _dot(n2b[:, :C], n2b))
    for _ in range(5):
        for t in tasks:
            wb = wmat[t].astype(BF16)
            wmat[t] = _dot(wb[:, :C], wb) + jnp.where(right, wmat[t], 0.0)
    sol = {t: _dot(wmat[t].astype(BF16), rhs[t]) for t in tasks}

    for ci in range(chunks_per_step):
        rows = pl.ds(ci * C, C)
        heads = [(ci, h) for h in range(GDN_HEADS)]
        s_old = {t: s_ref[t[1]] for t in heads}
        ws = {t: _dot(jnp.concatenate([sol[t][:, GDN_DV:].astype(BF16), wq[t]], axis=0),
                      s_old[t].astype(BF16)) for t in heads}
        vnb = {t: (sol[t][:, :GDN_DV] - ws[t][:C]).astype(BF16) for t in heads}
        for t in heads:
            h = t[1]
            hs = slice(h * GDN_DK, (h + 1) * GDN_DK)
            o = ws[t][C:] + _dot(attn[t], vnb[t])
            s_ref[h] = g_last[t] * s_old[t] + _dot_tn(kd[t], vnb[t])
            ms = jnp.mean(o * o, axis=-1, keepdims=True)
            on = o * lax.rsqrt(ms + EPS) * nw_ref[...]
            o_ref[rows, hs] = (on * _silu(z_ref[rows, hs].astype(F32))).astype(BF16)


def _gdn_kernel(q_ref, k_ref, v_ref, z_ref, psm_ref, psmt_ref, cw_ref, arow_ref,
                acol_ref, nw_ref, o_ref, s_ref, ext_scr, *, chunks_per_step):
    R = CHUNK * chunks_per_step

    @pl.when(pl.program_id(0) == 0)
    def _():
        s_ref[...] = jnp.zeros_like(s_ref)
        ext_scr[pl.ds(0, 8), :] = jnp.zeros((8, ext_scr.shape[1]), F32)

    psm = psm_ref[...]
    g_col = -jnp.exp(arow_ref[0:1, :]) * _softplus(psm + arow_ref[1:2, :])
    beta_col = _sigmoid(psm)
    cum_col = _dot_exact_lhs(_ones_where(_chunk_tril(R)), g_col)
    g_row = -jnp.exp(acol_ref[:, 0:1]) * _softplus(psmt_ref[...] + acol_ref[:, 1:2])
    ji = lax.broadcasted_iota(jnp.int32, (R, 2 * R), 0)
    li = lax.broadcasted_iota(jnp.int32, (R, 2 * R), 1)
    dup = ((ji >> 6) == (li >> 7)) & ((ji & (CHUNK - 1)) <= (li & (CHUNK - 1)))
    cum_row = _dot_exact_rhs(g_row, _ones_where(dup))

    ext_scr[pl.ds(8, R), 0:GDN_QK] = q_ref[...].astype(F32)
    ext_scr[pl.ds(8, R), GDN_QK:2 * GDN_QK] = k_ref[...].astype(F32)
    ext_scr[pl.ds(8, R), 2 * GDN_QK:] = v_ref[...].astype(F32)
    conv = cw_ref[GDN_CONV - 1:GDN_CONV, :] * ext_scr[pl.ds(8, R), :]
    for j in range(GDN_CONV - 1):
        conv = conv + cw_ref[j:j + 1, :] * ext_scr[pl.ds(8 - (GDN_CONV - 1) + j, R), :]
    ext_scr[pl.ds(0, 8), :] = ext_scr[pl.ds(R, 8), :]
    qkv = _silu(conv)

    _gdn_chunks(z_ref, nw_ref, o_ref, s_ref, qkv, cum_col, cum_row, beta_col, chunks_per_step)


def _gdn(p_big, p_sm, p_smt, conv_w, a_row, a_col, norm_w, chunks_per_step):
    L = p_big.shape[0]
    R = CHUNK * chunks_per_step
    base = (GLA_QK * 2 + GLA_V * 2) // GDN_QK
    kern = functools.partial(_gdn_kernel, chunks_per_step=chunks_per_step)
    return pl.pallas_call(
        kern,
        grid=(L // R,),
        in_specs=[
            pl.BlockSpec((R, GDN_QK), lambda n: (n, base)),
            pl.BlockSpec((R, GDN_QK), lambda n: (n, base + 1)),
            pl.BlockSpec((R, GDN_V), lambda n: (n, base + 2)),
            pl.BlockSpec((R, GDN_V), lambda n: (n, base + 3)),
            pl.BlockSpec((R, LANES), lambda n: (n, 0)),
            pl.BlockSpec((LANES, R), lambda n: (0, n)),
            pl.BlockSpec((GDN_CONV, 2 * GDN_QK + GDN_V), lambda n: (0, 0)),
            pl.BlockSpec((2, LANES), lambda n: (0, 0)),
            pl.BlockSpec((LANES, 2), lambda n: (0, 0)),
            pl.BlockSpec((1, GDN_DV), lambda n: (0, 0)),
        ],
        out_specs=pl.BlockSpec((R, GDN_V), lambda n: (n, 0)),
        out_shape=jax.ShapeDtypeStruct((L, GDN_V), BF16),
        scratch_shapes=[
            pltpu.VMEM((GDN_HEADS, GDN_DK, GDN_DV), F32),
            pltpu.VMEM((R + 8, 2 * GDN_QK + GDN_V), F32),
        ],
        compiler_params=pltpu.CompilerParams(
            dimension_semantics=("arbitrary",),
            vmem_limit_bytes=VMEM_LIMIT),
        name="gdn",
    )(p_big, p_big, p_big, p_big, p_sm, p_smt, conv_w, a_row, a_col, norm_w)


def _outproj_kernel(x_ref, oa_ref, ob_ref, wa_ref, wb_ref, y_ref):
    y_ref[...] = (x_ref[...] + _dot(oa_ref[...], wa_ref[...])
                  + _dot(ob_ref[...], wb_ref[...]))


def _outproj(x2, o_a, o_b, w_out_b, tm):
    L, D = x2.shape
    return pl.pallas_call(
        _outproj_kernel,
        grid=(L // tm,),
        in_specs=[
            pl.BlockSpec((tm, D), lambda i: (i, 0)),
            pl.BlockSpec((tm, GLA_V), lambda i: (i, 0)),
            pl.BlockSpec((tm, GDN_V), lambda i: (i, 0)),
            pl.BlockSpec((GLA_V, D), lambda i: (0, 0)),
            pl.BlockSpec((GDN_V, D), lambda i: (GLA_V // GDN_V, 0)),
        ],
        out_specs=pl.BlockSpec((tm, D), lambda i: (i, 0)),
        out_shape=jax.ShapeDtypeStruct((L, D), F32),
        compiler_params=pltpu.CompilerParams(
            dimension_semantics=("arbitrary",),
            vmem_limit_bytes=VMEM_LIMIT),
        name="outproj",
    )(x2, o_a, o_b, w_out_b, w_out_b)


def _ffn_kernel(x_ref, fnw_ref, wg_ref, wu_ref, wd_ref, onw_ref, y_ref, hf_ref, acc_ref):
    f = pl.program_id(1)

    @pl.when(f == 0)
    def _():
        x = x_ref[...]
        ms = jnp.mean(x * x, axis=-1, keepdims=True)
        hf_ref[...] = (x * lax.rsqrt(ms + EPS) * fnw_ref[...]).astype(BF16)
        acc_ref[...] = jnp.zeros_like(acc_ref)

    hf = hf_ref[...]
    g = _dot(hf, wg_ref[...])
    u = _dot(hf, wu_ref[...])
    acc_ref[...] += _dot((_silu(g) * u).astype(BF16), wd_ref[...])

    @pl.when(f == pl.num_programs(1) - 1)
    def _():
        r = x_ref[...] + acc_ref[...]
        ms = jnp.mean(r * r, axis=-1, keepdims=True)
        y_ref[...] = r * lax.rsqrt(ms + EPS) * onw_ref[...]


def _ffn(x1, ffn_norm_w, wg, wu, wd, final_norm_w, tm, tf):
    L, D = x1.shape
    F = wg.shape[1]
    return pl.pallas_call(
        _ffn_kernel,
        grid=(L // tm, F // tf),
        in_specs=[
            pl.BlockSpec((tm, D), lambda i, f: (i, 0)),
            pl.BlockSpec((1, D), lambda i, f: (0, 0)),
            pl.BlockSpec((D, tf), lambda i, f: (0, f)),
            pl.BlockSpec((D, tf), lambda i, f: (0, f)),
            pl.BlockSpec((tf, D), lambda i, f: (f, 0)),
            pl.BlockSpec((1, D), lambda i, f: (0, 0)),
        ],
        out_specs=pl.BlockSpec((tm, D), lambda i, f: (i, 0)),
        out_shape=jax.ShapeDtypeStruct((L, D), F32),
        scratch_shapes=[pltpu.VMEM((tm, D), BF16), pltpu.VMEM((tm, D), F32)],
        compiler_params=pltpu.CompilerParams(
            dimension_semantics=("arbitrary", "arbitrary"),
            vmem_limit_bytes=VMEM_LIMIT),
        name="ffn",
    )(x1, ffn_norm_w, wg, wu, wd, final_norm_w)


def _hi_lo(w):
    hi = w.astype(BF16)
    lo = (w - hi.astype(F32)).astype(BF16)
    return jnp.stack([hi, lo])


def _diag_selector():
    s_of_row = np.arange(SUB_BLOCK * GLA_DK) // GLA_DK
    n_mod = np.arange(CHUNK) % SUB_BLOCK
    return jnp.asarray(s_of_row[:, None] == n_mod[None, :], dtype=BF16)


def _pick(n, candidates):
    for c in candidates:
        if n % c == 0:
            return c
    raise ValueError(f"no tile in {candidates} divides {n}")


def kernel(x, attn_norm_w, w_in, gla_gate_w2, gla_gate_b, gla_norm_w, gdn_conv_w,
           gdn_a_log, gdn_dt_bias, gdn_norm_w, w_out, ffn_norm_w, w_gate, w_up,
           w_down, final_norm_w):
    bsz, seq, d_model = x.shape
    assert bsz == 1 and seq % CHUNK == 0
    x2 = x.reshape(seq, d_model).astype(F32)

    sizes = (GLA_QK, GLA_QK, GLA_V, GLA_GATE_RANK, GLA_V,
             GDN_QK, GDN_QK, GDN_V, GDN_HEADS, GDN_HEADS, GDN_V)
    offs = np.concatenate([[0], np.cumsum(sizes)])
    assert w_in.shape == (d_model, offs[-1])
    col = lambda i: w_in[:, offs[i]:offs[i + 1]].astype(F32)
    w_in_b = w_in.astype(BF16)
    w_big = jnp.concatenate([w_in_b[:, offs[0]:offs[3]], w_in_b[:, offs[4]:offs[8]],
                             w_in_b[:, offs[10]:offs[11]]], axis=1)
    n_small = GLA_GATE_RANK + 2 * GDN_HEADS
    w_small_t = jnp.concatenate([col(3), col(8), col(9)], axis=1).T
    w_small_t = jnp.pad(w_small_t, ((0, LANES - n_small), (0, 0)))
    w_smt = _hi_lo(w_small_t).reshape(2 * LANES, d_model)

    tm1 = _pick(seq, (1024, 512, 256, 128, 64))
    p_big, p_sm, p_smt = _inproj(x2, attn_norm_w.reshape(1, -1).astype(F32),
                                 w_big, w_smt, tm1, 512)

    cps = 2 if seq % (2 * CHUNK) == 0 else 1

    w2_pad = jnp.pad(gla_gate_w2.astype(F32), ((0, LANES - GLA_GATE_RANK), (0, 0)))
    o_a = _gla(p_big, p_sm, _hi_lo(w2_pad), gla_gate_b.reshape(1, -1).astype(F32),
               gla_norm_w.reshape(1, -1).astype(F32), _diag_selector(), cps)

    a_log_pad = jnp.zeros((LANES,), F32).at[COL_A:COL_A + GDN_HEADS].set(gdn_a_log.astype(F32))
    dt_pad = jnp.zeros((LANES,), F32).at[COL_A:COL_A + GDN_HEADS].set(gdn_dt_bias.astype(F32))
    a_row = jnp.stack([a_log_pad, dt_pad])
    o_b = _gdn(p_big, p_sm, p_smt, gdn_conv_w.astype(F32), a_row, a_row.T,
               gdn_norm_w.reshape(1, -1).astype(F32), 4 if seq % (4 * CHUNK) == 0 else cps)

    tm4 = _pick(seq, (512, 256, 128, 64))
    x1 = _outproj(x2, o_a, o_b, w_out.astype(BF16), tm4)

    y = _ffn(x1, ffn_norm_w.reshape(1, -1).astype(F32), w_gate.astype(BF16),
             w_up.astype(BF16), w_down.astype(BF16),
             final_norm_w.reshape(1, -1).astype(F32), tm4, 512)
    return y.reshape(bsz, seq, d_model).astype(x.dtype)
```

```python
import functools

import numpy as np
import jax
import jax.numpy as jnp
from jax import lax
from jax.experimental import pallas as pl
from jax.experimental.pallas import tpu as pltpu

F32 = jnp.float32
BF16 = jnp.bfloat16

EPS = 1e-6
CHUNK = 64

GLA_HEADS = 4
GLA_DK = 128
GLA_DV = 256
GLA_GATE_RANK = 16
GLA_GATE_TAU = 16.0
GLA_QK = GLA_HEADS * GLA_DK
GLA_V = GLA_HEADS * GLA_DV

GDN_HEADS = 8
GDN_DK = 128
GDN_DV = 128
GDN_CONV = 4
GDN_QK = GDN_HEADS * GDN_DK
GDN_V = GDN_HEADS * GDN_DV

LANES = 128
SUB_BLOCK = 16
VMEM_LIMIT = 48 * 1024 * 1024
N_BIG = 2 * GLA_QK + 2 * GLA_V + 2 * GDN_QK + 2 * GDN_V

COL_LR = 0
COL_A = GLA_GATE_RANK
COL_B = GLA_GATE_RANK + GDN_HEADS


def _dot(a, b):
    return jnp.dot(a, b, preferred_element_type=F32)


def _dot_nt(a, b):
    return lax.dot_general(a, b, (((1,), (1,)), ((), ())), preferred_element_type=F32)


def _dot_tn(a, b):
    return lax.dot_general(a, b, (((0,), (0,)), ((), ())), preferred_element_type=F32)


def _split3(a):
    hi = a.astype(BF16)
    r1 = a - hi.astype(F32)
    mid = r1.astype(BF16)
    lo = (r1 - mid.astype(F32)).astype(BF16)
    return hi, mid, lo


def _dot_exact_rhs(a, b_exact):
    hi, mid, lo = _split3(a)
    return _dot(hi, b_exact) + _dot(mid, b_exact) + _dot(lo, b_exact)


def _dot_exact_lhs(a_exact, b):
    hi, mid, lo = _split3(b)
    return _dot(a_exact, hi) + _dot(a_exact, mid) + _dot(a_exact, lo)


def _sigmoid(x):
    return 1.0 / (1.0 + jnp.exp(-x))


def _silu(x):
    return x * _sigmoid(x)


def _softplus(x):
    return jnp.maximum(x, 0.0) + jnp.log1p(jnp.exp(-jnp.abs(x)))


def _log_sigmoid(x):
    return -_softplus(-x)


def _wprep_kernel(wa_ref, wb_ref, o_ref, *, shift_steps):
    j = pl.program_id(0)
    j1, j2 = shift_steps

    def realigned(delta):
        a = wa_ref[...].astype(BF16)
        if delta == 0:
            return a
        return jnp.concatenate([a[:, delta:], wb_ref[:, :delta].astype(BF16)], axis=1)

    @pl.when(j < j1)
    def _():
        o_ref[...] = realigned(0)

    @pl.when((j >= j1) & (j < j2))
    def _():
        o_ref[...] = realigned(GLA_GATE_RANK)

    @pl.when(j >= j2)
    def _():
        o_ref[...] = realigned(GLA_GATE_RANK + 2 * GDN_HEADS)


def _wprep(w_in, tn):
    D = w_in.shape[0]
    g1 = GLA_QK * 2 + GLA_V
    g2 = g1 + GLA_V + GDN_QK * 2 + GDN_V
    assert g1 % tn == 0 and g2 % tn == 0 and N_BIG % tn == 0
    return pl.pallas_call(
        functools.partial(_wprep_kernel, shift_steps=(g1 // tn, g2 // tn)),
        grid=(N_BIG // tn,),
        in_specs=[
            pl.BlockSpec((D, tn), lambda j: (0, j)),
            pl.BlockSpec((D, LANES), lambda j: (0, (tn // LANES) * (j + 1))),
        ],
        out_specs=pl.BlockSpec((D, tn), lambda j: (0, j)),
        out_shape=jax.ShapeDtypeStruct((D, N_BIG), BF16),
        compiler_params=pltpu.CompilerParams(
            dimension_semantics=("arbitrary",),
            vmem_limit_bytes=VMEM_LIMIT),
        name="wprep",
    )(w_in, w_in)


def _inproj_kernel(x_ref, nw_ref, wbig_ref, wsmt_ref, *rest, n_side):
    side_in = rest[:n_side]
    pbig_ref, psm_ref, psmt_ref = rest[n_side:n_side + 3]
    side_out = rest[n_side + 3:2 * n_side + 3]
    (h_ref,) = rest[2 * n_side + 3:]
    j = pl.program_id(1)

    for src, dst in zip(side_in, side_out):
        dst[...] = src[...].astype(BF16)

    @pl.when(j == 0)
    def _():
        x = x_ref[...]
        ms = jnp.mean(x * x, axis=-1, keepdims=True)
        h = x * lax.rsqrt(ms + EPS) * nw_ref[...]
        hb = h.astype(BF16)
        h_ref[...] = hb
        hl = (h - hb.astype(F32)).astype(BF16)
        r = _dot_nt(wsmt_ref[...], hb)
        pt = r[:LANES] + r[LANES:] + _dot_nt(wsmt_ref[pl.ds(0, LANES), :], hl)
        psmt_ref[...] = pt
        psm_ref[...] = pt.T

    pbig_ref[...] = _dot(h_ref[...], wbig_ref[...]).astype(BF16)


def _max_split(n_steps, size, quantum):
    for nb in range(n_steps, 0, -1):
        if size % (nb * quantum) == 0:
            return nb
    raise ValueError(f"{size} is not a multiple of {quantum}")


def _inproj(x2, norm_w, w_big, w_smt, side_weights, tm, tn):
    L, D = x2.shape
    ni, nj = L // tm, N_BIG // tn
    side_specs, side_shapes = [], []
    for w in side_weights:
        rb = _max_split(ni, w.shape[0], 16)
        cb = _max_split(nj, w.shape[1], LANES)
        imap = functools.partial(
            lambda i, j, rb, cb: (jnp.minimum(i, rb - 1), jnp.minimum(j, cb - 1)), rb=rb, cb=cb)
        side_specs.append(pl.BlockSpec((w.shape[0] // rb, w.shape[1] // cb), imap))
        side_shapes.append(jax.ShapeDtypeStruct(w.shape, BF16))
    outs = pl.pallas_call(
        functools.partial(_inproj_kernel, n_side=len(side_weights)),
        grid=(ni, nj),
        in_specs=[
            pl.BlockSpec((tm, D), lambda i, j: (i, 0)),
            pl.BlockSpec((1, D), lambda i, j: (0, 0)),
            pl.BlockSpec((D, tn), lambda i, j: (0, j)),
            pl.BlockSpec((2 * LANES, D), lambda i, j: (0, 0)),
        ] + side_specs,
        out_specs=[
            pl.BlockSpec((tm, tn), lambda i, j: (i, j)),
            pl.BlockSpec((tm, LANES), lambda i, j: (i, 0)),
            pl.BlockSpec((LANES, tm), lambda i, j: (0, i)),
        ] + side_specs,
        out_shape=[
            jax.ShapeDtypeStruct((L, N_BIG), BF16),
            jax.ShapeDtypeStruct((L, LANES), F32),
            jax.ShapeDtypeStruct((LANES, L), F32),
        ] + side_shapes,
        scratch_shapes=[pltpu.VMEM((tm, D), BF16)],
        compiler_params=pltpu.CompilerParams(
            dimension_semantics=("arbitrary", "arbitrary"),
            vmem_limit_bytes=VMEM_LIMIT),
        name="inproj",
    )(x2, norm_w, w_big, w_smt, *side_weights)
    return outs[0], outs[1], outs[2], outs[3:]


def _ones_where(mask):
    return jnp.where(mask, 1.0, 0.0).astype(BF16)


def _chunk_tril(rows, upper=False):
    ti = lax.broadcasted_iota(jnp.int32, (rows, rows), 0)
    si = lax.broadcasted_iota(jnp.int32, (rows, rows), 1)
    same = (ti >> 6) == (si >> 6)
    return same & ((ti <= si) if upper else (ti >= si))


def _gla_chunk(q_ref, k_ref, v_ref, gate_ref, nw_ref, sel_ref, o_ref, st_ref,
               c_scr, k_scr, r0):
    C = CHUNK
    rows = pl.ds(r0, C)
    row = lax.broadcasted_iota(jnp.int32, (C, GLA_DK), 0)
    ti = lax.broadcasted_iota(jnp.int32, (C, C), 0)
    si = lax.broadcasted_iota(jnp.int32, (C, C), 1)
    in_hi32 = row >= 32
    right16 = (row & 31) >= 16
    row_lt32 = row < 32
    same_pair = (ti >> 5) == (si >> 5)
    same_blk = (ti >> 4) == (si >> 4)
    trow = lax.broadcasted_iota(jnp.int32, (SUB_BLOCK, GLA_DK), 0)

    for h in range(GLA_HEADS):
        ks = slice(h * GLA_DK, (h + 1) * GLA_DK)
        vs = slice(h * GLA_DV, (h + 1) * GLA_DV)
        q = q_ref[rows, ks].astype(F32) * (GLA_DK ** -0.5)
        k = k_scr[rows, ks]
        v = v_ref[rows, vs]
        c = c_scr[rows, ks]

        ref = c_scr[pl.ds(r0 + 31, 1), ks]
        e = jnp.exp(jnp.where(in_hi32, c - ref, ref - c))
        qt = jnp.where(in_hi32, q * e, 0.0).astype(BF16)
        kt = jnp.where(in_hi32, 0.0, k * e).astype(BF16)
        a = _dot_nt(qt, kt)
        ref = jnp.where(row_lt32, c_scr[pl.ds(r0 + 15, 1), ks], c_scr[pl.ds(r0 + 47, 1), ks])
        e = jnp.exp(jnp.where(right16, c - ref, ref - c))
        qt = jnp.where(right16, q * e, 0.0).astype(BF16)
        kt = jnp.where(right16, 0.0, k * e).astype(BF16)
        a = a + jnp.where(same_pair, _dot_nt(qt, kt), 0.0)
        blocks = []
        for b in range(C // SUB_BLOCK):
            b0 = b * SUB_BLOCK
            cb = c[b0:b0 + SUB_BLOCK]
            qb = q[b0:b0 + SUB_BLOCK]
            parts = []
            for s in range(SUB_BLOCK):
                c_s = c_scr[pl.ds(r0 + b0 + s, 1), ks]
                k_s = k_scr[pl.ds(r0 + b0 + s, 1), ks]
                e = jnp.exp(jnp.where(trow >= s, cb - c_s, -jnp.inf))
                parts.append(((qb * e) * k_s).astype(BF16))
            blocks.append(jnp.concatenate(parts, axis=1))
        p2 = jnp.concatenate(blocks, axis=0)
        a = a + jnp.where(same_blk, _dot(p2, sel_ref[...]), 0.0)

        st = st_ref[h]
        c_last = c_scr[pl.ds(r0 + C - 1, 1), ks]
        qd = (q * jnp.exp(c)).astype(BF16)
        o = _dot(a.astype(BF16), v) + _dot_nt(qd, st.astype(BF16))
        kd = (k * jnp.exp(c_last - c)).astype(BF16)
        st_ref[h] = jnp.exp(c_last) * st + _dot_tn(v, kd)

        ms = jnp.mean(o * o, axis=-1, keepdims=True)
        on = o * lax.rsqrt(ms + EPS) * nw_ref[...]
        o_ref[rows, vs] = (on * _silu(gate_ref[rows, vs].astype(F32))).astype(BF16)


def _gla_kernel(q_ref, k_ref, v_ref, gate_ref, psm_ref, w2_ref, gb_ref, nw_ref,
                sel_ref, o_ref, st_ref, c_scr, k_scr, *, chunks_per_step):
    @pl.when(pl.program_id(0) == 0)
    def _():
        st_ref[...] = jnp.zeros_like(st_ref)

    a_hi, a_mid, _ = _split3(psm_ref[...])
    z = (_dot(a_hi, w2_ref[0]) + _dot(a_mid, w2_ref[0]) + _dot(a_hi, w2_ref[1])
         + gb_ref[...])
    log_a = _log_sigmoid(z) * (1.0 / GLA_GATE_TAU)
    tril = _ones_where(_chunk_tril(CHUNK * chunks_per_step))
    c_scr[...] = _dot_exact_lhs(tril, log_a)
    k_scr[...] = k_ref[...].astype(F32)

    for ci in range(chunks_per_step):
        _gla_chunk(q_ref, k_ref, v_ref, gate_ref, nw_ref, sel_ref, o_ref, st_ref,
                   c_scr, k_scr, ci * CHUNK)


def _gla(p_big, p_sm, w2, gate_b, norm_w, sel, chunks_per_step):
    L = p_big.shape[0]
    R = CHUNK * chunks_per_step
    kern = functools.partial(_gla_kernel, chunks_per_step=chunks_per_step)
    return pl.pallas_call(
        kern,
        grid=(L // R,),
        in_specs=[
            pl.BlockSpec((R, GLA_QK), lambda n: (n, 0)),
            pl.BlockSpec((R, GLA_QK), lambda n: (n, 1)),
            pl.BlockSpec((R, GLA_V), lambda n: (n, 1)),
            pl.BlockSpec((R, GLA_V), lambda n: (n, 2)),
            pl.BlockSpec((R, LANES), lambda n: (n, 0)),
            pl.BlockSpec((2, LANES, GLA_QK), lambda n: (0, 0, 0)),
            pl.BlockSpec((1, GLA_QK), lambda n: (0, 0)),
            pl.BlockSpec((1, GLA_DV), lambda n: (0, 0)),
            pl.BlockSpec((SUB_BLOCK * GLA_DK, CHUNK), lambda n: (0, 0)),
        ],
        out_specs=pl.BlockSpec((R, GLA_V), lambda n: (n, 0)),
        out_shape=jax.ShapeDtypeStruct((L, GLA_V), BF16),
        scratch_shapes=[
            pltpu.VMEM((GLA_HEADS, GLA_DV, GLA_DK), F32),
            pltpu.VMEM((R, GLA_QK), F32),
            pltpu.VMEM((R, GLA_QK), F32),
        ],
        compiler_params=pltpu.CompilerParams(
            dimension_semantics=("arbitrary",),
            vmem_limit_bytes=VMEM_LIMIT),
        name="gla",
    )(p_big, p_big, p_big, p_big, p_sm, w2, gate_b, norm_w, sel)


def _gdn_chunks(z_ref, nw_ref, o_ref, s_ref, qkv, cum_col, cum_row, beta_col,
                chunks_per_step):
    C = CHUNK
    ti = lax.broadcasted_iota(jnp.int32, (C, 2 * C), 0)
    li = lax.broadcasted_iota(jnp.int32, (C, 2 * C), 1)
    si = li & (C - 1)
    right = li >= C
    incl = ti >= si
    strict = ti > si
    eye_right = jnp.where((ti == si) & right, 1.0, 0.0)
    tasks = [(ci, h) for ci in range(chunks_per_step) for h in range(GDN_HEADS)]

    wmat, attn, rhs, wq, kd, g_last = {}, {}, {}, {}, {}, {}
    for t in tasks:
        ci, h = t
        r0 = ci * C
        cq = qkv[r0:r0 + C, h * GDN_DK:(h + 1) * GDN_DK]
        ck = qkv[r0:r0 + C, GDN_QK + h * GDN_DK:GDN_QK + (h + 1) * GDN_DK]
        cv = qkv[r0:r0 + C, 2 * GDN_QK + h * GDN_DV:2 * GDN_QK + (h + 1) * GDN_DV]
        qn = cq * lax.rsqrt(jnp.sum(cq * cq, axis=-1, keepdims=True) + EPS) * (GDN_DK ** -0.5)
        kn = ck * lax.rsqrt(jnp.sum(ck * ck, axis=-1, keepdims=True) + EPS)
        cc = cum_col[r0:r0 + C, COL_A + h:COL_A + h + 1]
        cr = cum_row[COL_A + h:COL_A + h + 1, 2 * r0:2 * r0 + 2 * C]
        beta = beta_col[r0:r0 + C, COL_B + h:COL_B + h + 1]
        gamma = jnp.exp(jnp.where(incl, cc - cr, -jnp.inf))
        kb = kn * beta
        knb = kn.astype(BF16)
        x = _dot_nt(jnp.concatenate([qn.astype(BF16), kb.astype(BF16)], axis=0),
                    jnp.concatenate([knb, knb], axis=0))
        attn[t] = (x[:C, :C] * gamma[:, :C]).astype(BF16)
        wmat[t] = jnp.where(strict, x[C:] * gamma, 0.0)
        e_c = jnp.exp(cc)
        c_last = cc[C - 1:C, :]
        rhs1 = jnp.concatenate([cv * beta, kb * e_c], axis=1).astype(BF16)
        rhs[t] = jnp.concatenate([rhs1, rhs1], axis=0)
        wq[t] = (qn * e_c).astype(BF16)
        kd[t] = (kn * jnp.exp(c_last - cc)).astype(BF16)
        g_last[t] = jnp.exp(c_last)

    for t in tasks:
        n2b = wmat[t].astype(BF16)
        wmat[t] = jnp.where(right, eye_right - wmat[t], _dot(n2b[:, :C], n2b))
    for _ in range(5):
        for t in tasks:
            wb = wmat[t].astype(BF16)
            wmat[t] = _dot(wb[:, :C], wb) + jnp.where(right, wmat[t], 0.0)
    sol = {t: _dot(wmat[t].astype(BF16), rhs[t]) for t in tasks}

    for ci in range(chunks_per_step):
        rows = pl.ds(ci * C, C)
        heads = [(ci, h) for h in range(GDN_HEADS)]
        s_old = {t: s_ref[t[1]] for t in heads}
        ws = {t: _dot(jnp.concatenate([sol[t][:, GDN_DV:].astype(BF16), wq[t]], axis=0),
                      s_old[t].astype(BF16)) for t in heads}
        vnb = {t: (sol[t][:, :GDN_DV] - ws[t][:C]).astype(BF16) for t in heads}
        for t in heads:
            h = t[1]
            hs = slice(h * GDN_DK, (h + 1) * GDN_DK)
            o = ws[t][C:] + _dot(attn[t], vnb[t])
            s_ref[h] = g_last[t] * s_old[t] + _dot_tn(kd[t], vnb[t])
            ms = jnp.mean(o * o, axis=-1, keepdims=True)
            on = o * lax.rsqrt(ms + EPS) * nw_ref[...]
            o_ref[rows, hs] = (on * _silu(z_ref[rows, hs].astype(F32))).astype(BF16)


def _gdn_kernel(q_ref, k_ref, v_ref, z_ref, psm_ref, psmt_ref, cw_ref, arow_ref,
                acol_ref, nw_ref, o_ref, s_ref, ext_scr, *, chunks_per_step):
    R = CHUNK * chunks_per_step

    @pl.when(pl.program_id(0) == 0)
    def _():
        s_ref[...] = jnp.zeros_like(s_ref)
        ext_scr[pl.ds(0, 8), :] = jnp.zeros((8, ext_scr.shape[1]), F32)

    psm = psm_ref[...]
    g_col = -jnp.exp(arow_ref[0:1, :]) * _softplus(psm + arow_ref[1:2, :])
    beta_col = _sigmoid(psm)
    cum_col = _dot_exact_lhs(_ones_where(_chunk_tril(R)), g_col)
    g_row = -jnp.exp(acol_ref[:, 0:1]) * _softplus(psmt_ref[...] + acol_ref[:, 1:2])
    ji = lax.broadcasted_iota(jnp.int32, (R, 2 * R), 0)
    li = lax.broadcasted_iota(jnp.int32, (R, 2 * R), 1)
    dup = ((ji >> 6) == (li >> 7)) & ((ji & (CHUNK - 1)) <= (li & (CHUNK - 1)))
    cum_row = _dot_exact_rhs(g_row, _ones_where(dup))

    ext_scr[pl.ds(8, R), 0:GDN_QK] = q_ref[...].astype(F32)
    ext_scr[pl.ds(8, R), GDN_QK:2 * GDN_QK] = k_ref[...].astype(F32)
    ext_scr[pl.ds(8, R), 2 * GDN_QK:] = v_ref[...].astype(F32)
    conv = cw_ref[GDN_CONV - 1:GDN_CONV, :] * ext_scr[pl.ds(8, R), :]
    for j in range(GDN_CONV - 1):
        conv = conv + cw_ref[j:j + 1, :] * ext_scr[pl.ds(8 - (GDN_CONV - 1) + j, R), :]
    ext_scr[pl.ds(0, 8), :] = ext_scr[pl.ds(R, 8), :]
    qkv = _silu(conv)

    _gdn_chunks(z_ref, nw_ref, o_ref, s_ref, qkv, cum_col, cum_row, beta_col, chunks_per_step)


def _gdn(p_big, p_sm, p_smt, conv_w, a_row, a_col, norm_w, chunks_per_step):
    L = p_big.shape[0]
    R = CHUNK * chunks_per_step
    base = (GLA_QK * 2 + GLA_V * 2) // GDN_QK
    kern = functools.partial(_gdn_kernel, chunks_per_step=chunks_per_step)
    return pl.pallas_call(
        kern,
        grid=(L // R,),
        in_specs=[
            pl.BlockSpec((R, GDN_QK), lambda n: (n, base)),
            pl.BlockSpec((R, GDN_QK), lambda n: (n, base + 1)),
            pl.BlockSpec((R, GDN_V), lambda n: (n, base + 2)),
            pl.BlockSpec((R, GDN_V), lambda n: (n, base + 3)),
            pl.BlockSpec((R, LANES), lambda n: (n, 0)),
            pl.BlockSpec((LANES, R), lambda n: (0, n)),
            pl.BlockSpec((GDN_CONV, 2 * GDN_QK + GDN_V), lambda n: (0, 0)),
            pl.BlockSpec((2, LANES), lambda n: (0, 0)),
            pl.BlockSpec((LANES, 2), lambda n: (0, 0)),
            pl.BlockSpec((1, GDN_DV), lambda n: (0, 0)),
        ],
        out_specs=pl.BlockSpec((R, GDN_V), lambda n: (n, 0)),
        out_shape=jax.ShapeDtypeStruct((L, GDN_V), BF16),
        scratch_shapes=[
            pltpu.VMEM((GDN_HEADS, GDN_DK, GDN_DV), F32),
            pltpu.VMEM((R + 8, 2 * GDN_QK + GDN_V), F32),
        ],
        compiler_params=pltpu.CompilerParams(
            dimension_semantics=("arbitrary",),
            vmem_limit_bytes=VMEM_LIMIT),
        name="gdn",
    )(p_big, p_big, p_big, p_big, p_sm, p_smt, conv_w, a_row, a_col, norm_w)


def _outproj_kernel(x_ref, oa_ref, ob_ref, wa_ref, wb_ref, y_ref):
    y_ref[...] = (x_ref[...] + _dot(oa_ref[...], wa_ref[...])
                  + _dot(ob_ref[...], wb_ref[...]))


def _outproj(x2, o_a, o_b, w_out_b, tm):
    L, D = x2.shape
    return pl.pallas_call(
        _outproj_kernel,
        grid=(L // tm,),
        in_specs=[
            pl.BlockSpec((tm, D), lambda i: (i, 0)),
            pl.BlockSpec((tm, GLA_V), lambda i: (i, 0)),
            pl.BlockSpec((tm, GDN_V), lambda i: (i, 0)),
            pl.BlockSpec((GLA_V, D), lambda i: (0, 0)),
            pl.BlockSpec((GDN_V, D), lambda i: (GLA_V // GDN_V, 0)),
        ],
        out_specs=pl.BlockSpec((tm, D), lambda i: (i, 0)),
        out_shape=jax.ShapeDtypeStruct((L, D), F32),
        compiler_params=pltpu.CompilerParams(
            dimension_semantics=("arbitrary",),
            vmem_limit_bytes=VMEM_LIMIT),
        name="outproj",
    )(x2, o_a, o_b, w_out_b, w_out_b)


def _ffn_kernel(x_ref, fnw_ref, wg_ref, wu_ref, wd_ref, onw_ref, y_ref, hf_ref, acc_ref):
    f = pl.program_id(1)

    @pl.when(f == 0)
    def _():
        x = x_ref[...]
        ms = jnp.mean(x * x, axis=-1, keepdims=True)
        hf_ref[...] = (x * lax.rsqrt(ms + EPS) * fnw_ref[...]).astype(BF16)
        acc_ref[...] = jnp.zeros_like(acc_ref)

    hf = hf_ref[...]
    g = _dot(hf, wg_ref[...])
    u = _dot(hf, wu_ref[...])
    acc_ref[...] += _dot((_silu(g) * u).astype(BF16), wd_ref[...])

    @pl.when(f == pl.num_programs(1) - 1)
    def _():
        r = x_ref[...] + acc_ref[...]
        ms = jnp.mean(r * r, axis=-1, keepdims=True)
        y_ref[...] = r * lax.rsqrt(ms + EPS) * onw_ref[...]


def _ffn(x1, ffn_norm_w, wg, wu, wd, final_norm_w, tm, tf):
    L, D = x1.shape
    F = wg.shape[1]
    return pl.pallas_call(
        _ffn_kernel,
        grid=(L // tm, F // tf),
        in_specs=[
            pl.BlockSpec((tm, D), lambda i, f: (i, 0)),
            pl.BlockSpec((1, D), lambda i, f: (0, 0)),
            pl.BlockSpec((D, tf), lambda i, f: (0, f)),
            pl.BlockSpec((D, tf), lambda i, f: (0, f)),
            pl.BlockSpec((tf, D), lambda i, f: (f, 0)),
            pl.BlockSpec((1, D), lambda i, f: (0, 0)),
        ],
        out_specs=pl.BlockSpec((tm, D), lambda i, f: (i, 0)),
        out_shape=jax.ShapeDtypeStruct((L, D), F32),
        scratch_shapes=[pltpu.VMEM((tm, D), BF16), pltpu.VMEM((tm, D), F32)],
        compiler_params=pltpu.CompilerParams(
            dimension_semantics=("arbitrary", "arbitrary"),
            vmem_limit_bytes=VMEM_LIMIT),
        name="ffn",
    )(x1, ffn_norm_w, wg, wu, wd, final_norm_w)


def _hi_lo(w):
    hi = w.astype(BF16)
    lo = (w - hi.astype(F32)).astype(BF16)
    return jnp.stack([hi, lo])


def _diag_selector():
    s_of_row = np.arange(SUB_BLOCK * GLA_DK) // GLA_DK
    n_mod = np.arange(CHUNK) % SUB_BLOCK
    return jnp.asarray(s_of_row[:, None] == n_mod[None, :], dtype=BF16)


def _pick(n, candidates):
    for c in candidates:
        if n % c == 0:
            return c
    raise ValueError(f"no tile in {candidates} divides {n}")


def kernel(x, attn_norm_w, w_in, gla_gate_w2, gla_gate_b, gla_norm_w, gdn_conv_w,
           gdn_a_log, gdn_dt_bias, gdn_norm_w, w_out, ffn_norm_w, w_gate, w_up,
           w_down, final_norm_w):
    bsz, seq, d_model = x.shape
    assert bsz == 1 and seq % CHUNK == 0
    x2 = x.reshape(seq, d_model).astype(F32)

    sizes = (GLA_QK, GLA_QK, GLA_V, GLA_GATE_RANK, GLA_V,
             GDN_QK, GDN_QK, GDN_V, GDN_HEADS, GDN_HEADS, GDN_V)
    offs = np.concatenate([[0], np.cumsum(sizes)])
    assert w_in.shape == (d_model, offs[-1])
    col = lambda i: w_in[:, offs[i]:offs[i + 1]].astype(F32)
    n_small = GLA_GATE_RANK + 2 * GDN_HEADS
    w_small_t = jnp.concatenate([col(3), col(8), col(9)], axis=1).T
    w_small_t = jnp.pad(w_small_t, ((0, LANES - n_small), (0, 0)))
    w_smt = _hi_lo(w_small_t).reshape(2 * LANES, d_model)

    tm1 = _pick(seq, (1024, 512, 256, 128, 64))
    p_big, p_sm, p_smt, (w_out_b, w_gate_b, w_up_b, w_down_b) = _inproj(
        x2, attn_norm_w.reshape(1, -1).astype(F32), _wprep(w_in.astype(F32), 512), w_smt,
        [w.astype(F32) for w in (w_out, w_gate, w_up, w_down)], tm1, 512)

    cps = 2 if seq % (2 * CHUNK) == 0 else 1

    w2_pad = jnp.pad(gla_gate_w2.astype(F32), ((0, LANES - GLA_GATE_RANK), (0, 0)))
    o_a = _gla(p_big, p_sm, _hi_lo(w2_pad), gla_gate_b.reshape(1, -1).astype(F32),
               gla_norm_w.reshape(1, -1).astype(F32), _diag_selector(), cps)

    a_log_pad = jnp.zeros((LANES,), F32).at[COL_A:COL_A + GDN_HEADS].set(gdn_a_log.astype(F32))
    dt_pad = jnp.zeros((LANES,), F32).at[COL_A:COL_A + GDN_HEADS].set(gdn_dt_bias.astype(F32))
    a_row = jnp.stack([a_log_pad, dt_pad])
    o_b = _gdn(p_big, p_sm, p_smt, gdn_conv_w.astype(F32), a_row, a_row.T,
               gdn_norm_w.reshape(1, -1).astype(F32), 4 if seq % (4 * CHUNK) == 0 else cps)

    tm4 = _pick(seq, (512, 256, 128, 64))
    x1 = _outproj(x2, o_a, o_b, w_out_b, tm4)

    y = _ffn(x1, ffn_norm_w.reshape(1, -1).astype(F32), w_gate_b, w_up_b, w_down_b,
             final_norm_w.reshape(1, -1).astype(F32), tm4, 512)
    return y.reshape(bsz, seq, d_model).astype(x.dtype)
```

```python
import functools

import numpy as np
import jax
import jax.numpy as jnp
from jax import lax
from jax.experimental import pallas as pl
from jax.experimental.pallas import tpu as pltpu

F32 = jnp.float32
BF16 = jnp.bfloat16

EPS = 1e-6
CHUNK = 64

GLA_HEADS = 4
GLA_DK = 128
GLA_DV = 256
GLA_GATE_RANK = 16
GLA_GATE_TAU = 16.0
GLA_QK = GLA_HEADS * GLA_DK
GLA_V = GLA_HEADS * GLA_DV

GDN_HEADS = 8
GDN_DK = 128
GDN_DV = 128
GDN_CONV = 4
GDN_QK = GDN_HEADS * GDN_DK
GDN_V = GDN_HEADS * GDN_DV

LANES = 128
SUB_BLOCK = 16
VMEM_LIMIT = 48 * 1024 * 1024
N_BIG = 2 * GLA_QK + 2 * GLA_V + 2 * GDN_QK + 2 * GDN_V

COL_LR = 0
COL_A = GLA_GATE_RANK
COL_B = GLA_GATE_RANK + GDN_HEADS


def _dot(a, b):
    return jnp.dot(a, b, preferred_element_type=F32)


def _dot_nt(a, b):
    return lax.dot_general(a, b, (((1,), (1,)), ((), ())), preferred_element_type=F32)


def _dot_tn(a, b):
    return lax.dot_general(a, b, (((0,), (0,)), ((), ())), preferred_element_type=F32)


def _split3(a):
    hi = a.astype(BF16)
    r1 = a - hi.astype(F32)
    mid = r1.astype(BF16)
    lo = (r1 - mid.astype(F32)).astype(BF16)
    return hi, mid, lo


def _dot_exact_rhs(a, b_exact):
    hi, mid, lo = _split3(a)
    return _dot(hi, b_exact) + _dot(mid, b_exact) + _dot(lo, b_exact)


def _dot_exact_lhs(a_exact, b):
    hi, mid, lo = _split3(b)
    return _dot(a_exact, hi) + _dot(a_exact, mid) + _dot(a_exact, lo)


def _sigmoid(x):
    return 1.0 / (1.0 + jnp.exp(-x))


def _silu(x):
    return x * _sigmoid(x)


def _softplus(x):
    return jnp.maximum(x, 0.0) + jnp.log1p(jnp.exp(-jnp.abs(x)))


def _log_sigmoid(x):
    return -_softplus(-x)


def _wprep_kernel(wa_ref, wb_ref, o_ref, *, shift_steps):
    j = pl.program_id(0)
    j1, j2 = shift_steps

    def realigned(delta):
        if delta == 0:
            return wa_ref[...].astype(BF16)
        return jnp.concatenate([wa_ref[pl.ds(delta, wa_ref.shape[0] - delta), :],
                                wb_ref[pl.ds(0, delta), :]], axis=0).astype(BF16)

    @pl.when(j < j1)
    def _():
        o_ref[...] = realigned(0)

    @pl.when((j >= j1) & (j < j2))
    def _():
        o_ref[...] = realigned(GLA_GATE_RANK)

    @pl.when(j >= j2)
    def _():
        o_ref[...] = realigned(GLA_GATE_RANK + 2 * GDN_HEADS)


def _wprep(w_in_t, tn):
    D = w_in_t.shape[1]
    n_small = GLA_GATE_RANK + 2 * GDN_HEADS
    g1 = GLA_QK * 2 + GLA_V
    g2 = g1 + GLA_V + GDN_QK * 2 + GDN_V
    assert g1 % tn == 0 and g2 % tn == 0 and N_BIG % tn == 0 and tn % n_small == 0
    return pl.pallas_call(
        functools.partial(_wprep_kernel, shift_steps=(g1 // tn, g2 // tn)),
        grid=(N_BIG // tn,),
        in_specs=[
            pl.BlockSpec((tn, D), lambda j: (j, 0)),
            pl.BlockSpec((n_small, D), lambda j: ((tn // n_small) * (j + 1), 0)),
        ],
        out_specs=pl.BlockSpec((tn, D), lambda j: (j, 0)),
        out_shape=jax.ShapeDtypeStruct((N_BIG, D), BF16),
        compiler_params=pltpu.CompilerParams(
            dimension_semantics=("arbitrary",),
            vmem_limit_bytes=VMEM_LIMIT),
        name="wprep",
    )(w_in_t, w_in_t)


def _inproj_kernel(x_ref, nw_ref, wbig_ref, wsmt_ref, *rest, n_side):
    side_in = rest[:n_side]
    pbig_ref, psm_ref, psmt_ref = rest[n_side:n_side + 3]
    side_out = rest[n_side + 3:2 * n_side + 3]
    (h_ref,) = rest[2 * n_side + 3:]
    j = pl.program_id(1)

    for src, dst in zip(side_in, side_out):
        dst[...] = src[...].astype(BF16)

    @pl.when(j == 0)
    def _():
        x = x_ref[...]
        ms = jnp.mean(x * x, axis=-1, keepdims=True)
        h = x * lax.rsqrt(ms + EPS) * nw_ref[...]
        hb = h.astype(BF16)
        h_ref[...] = hb
        hl = (h - hb.astype(F32)).astype(BF16)
        r = _dot_nt(wsmt_ref[...], hb)
        pt = r[:LANES] + r[LANES:] + _dot_nt(wsmt_ref[pl.ds(0, LANES), :], hl)
        psmt_ref[...] = pt
        psm_ref[...] = pt.T

    pbig_ref[...] = _dot_nt(h_ref[...], wbig_ref[...]).astype(BF16)


def _max_split(n_steps, size, quantum):
    for nb in range(n_steps, 0, -1):
        if size % (nb * quantum) == 0:
            return nb
    raise ValueError(f"{size} is not a multiple of {quantum}")


def _inproj(x2, norm_w, w_big, w_smt, side_weights, tm, tn):
    L, D = x2.shape
    ni, nj = L // tm, N_BIG // tn
    side_specs, side_shapes = [], []
    for w in side_weights:
        rb = _max_split(ni, w.shape[0], 16)
        cb = _max_split(nj, w.shape[1], LANES)
        imap = functools.partial(
            lambda i, j, rb, cb: (jnp.minimum(i, rb - 1), jnp.minimum(j, cb - 1)), rb=rb, cb=cb)
        side_specs.append(pl.BlockSpec((w.shape[0] // rb, w.shape[1] // cb), imap))
        side_shapes.append(jax.ShapeDtypeStruct(w.shape, BF16))
    outs = pl.pallas_call(
        functools.partial(_inproj_kernel, n_side=len(side_weights)),
        grid=(ni, nj),
        in_specs=[
            pl.BlockSpec((tm, D), lambda i, j: (i, 0)),
            pl.BlockSpec((1, D), lambda i, j: (0, 0)),
            pl.BlockSpec((tn, D), lambda i, j: (j, 0)),
            pl.BlockSpec((2 * LANES, D), lambda i, j: (0, 0)),
        ] + side_specs,
        out_specs=[
            pl.BlockSpec((tm, tn), lambda i, j: (i, j)),
            pl.BlockSpec((tm, LANES), lambda i, j: (i, 0)),
            pl.BlockSpec((LANES, tm), lambda i, j: (0, i)),
        ] + side_specs,
        out_shape=[
            jax.ShapeDtypeStruct((L, N_BIG), BF16),
            jax.ShapeDtypeStruct((L, LANES), F32),
            jax.ShapeDtypeStruct((LANES, L), F32),
        ] + side_shapes,
        scratch_shapes=[pltpu.VMEM((tm, D), BF16)],
        compiler_params=pltpu.CompilerParams(
            dimension_semantics=("arbitrary", "arbitrary"),
            vmem_limit_bytes=VMEM_LIMIT),
        name="inproj",
    )(x2, norm_w, w_big, w_smt, *side_weights)
    return outs[0], outs[1], outs[2], outs[3:]


def _ones_where(mask):
    return jnp.where(mask, 1.0, 0.0).astype(BF16)


def _chunk_tril(rows, upper=False):
    ti = lax.broadcasted_iota(jnp.int32, (rows, rows), 0)
    si = lax.broadcasted_iota(jnp.int32, (rows, rows), 1)
    same = (ti >> 6) == (si >> 6)
    return same & ((ti <= si) if upper else (ti >= si))


def _gla_chunk(q_ref, k_ref, v_ref, gate_ref, nw_ref, sel_ref, o_ref, st_ref,
               c_scr, k_scr, r0):
    C = CHUNK
    rows = pl.ds(r0, C)
    row = lax.broadcasted_iota(jnp.int32, (C, GLA_DK), 0)
    ti = lax.broadcasted_iota(jnp.int32, (C, C), 0)
    si = lax.broadcasted_iota(jnp.int32, (C, C), 1)
    in_hi32 = row >= 32
    right16 = (row & 31) >= 16
    row_lt32 = row < 32
    same_pair = (ti >> 5) == (si >> 5)
    same_blk = (ti >> 4) == (si >> 4)
    trow = lax.broadcasted_iota(jnp.int32, (SUB_BLOCK, GLA_DK), 0)

    for h in range(GLA_HEADS):
        ks = slice(h * GLA_DK, (h + 1) * GLA_DK)
        vs = slice(h * GLA_DV, (h + 1) * GLA_DV)
        q = q_ref[rows, ks].astype(F32) * (GLA_DK ** -0.5)
        k = k_scr[rows, ks]
        v = v_ref[rows, vs]
        c = c_scr[rows, ks]

        ref = c_scr[pl.ds(r0 + 31, 1), ks]
        e = jnp.exp(jnp.where(in_hi32, c - ref, ref - c))
        qt = jnp.where(in_hi32, q * e, 0.0).astype(BF16)
        kt = jnp.where(in_hi32, 0.0, k * e).astype(BF16)
        a = _dot_nt(qt, kt)
        ref = jnp.where(row_lt32, c_scr[pl.ds(r0 + 15, 1), ks], c_scr[pl.ds(r0 + 47, 1), ks])
        e = jnp.exp(jnp.where(right16, c - ref, ref - c))
        qt = jnp.where(right16, q * e, 0.0).astype(BF16)
        kt = jnp.where(right16, 0.0, k * e).astype(BF16)
        a = a + jnp.where(same_pair, _dot_nt(qt, kt), 0.0)
        blocks = []
        for b in range(C // SUB_BLOCK):
            b0 = b * SUB_BLOCK
            cb = c[b0:b0 + SUB_BLOCK]
            qb = q[b0:b0 + SUB_BLOCK]
            parts = []
            for s in range(SUB_BLOCK):
                c_s = c_scr[pl.ds(r0 + b0 + s, 1), ks]
                k_s = k_scr[pl.ds(r0 + b0 + s, 1), ks]
                e = jnp.exp(jnp.where(trow >= s, cb - c_s, -jnp.inf))
                parts.append(((qb * e) * k_s).astype(BF16))
            blocks.append(jnp.concatenate(parts, axis=1))
        p2 = jnp.concatenate(blocks, axis=0)
        a = a + jnp.where(same_blk, _dot(p2, sel_ref[...]), 0.0)

        st = st_ref[h]
        c_last = c_scr[pl.ds(r0 + C - 1, 1), ks]
        qd = (q * jnp.exp(c)).astype(BF16)
        o = _dot(a.astype(BF16), v) + _dot_nt(qd, st.astype(BF16))
        kd = (k * jnp.exp(c_last - c)).astype(BF16)
        st_ref[h] = jnp.exp(c_last) * st + _dot_tn(v, kd)

        ms = jnp.mean(o * o, axis=-1, keepdims=True)
        on = o * lax.rsqrt(ms + EPS) * nw_ref[...]
        o_ref[rows, vs] = (on * _silu(gate_ref[rows, vs].astype(F32))).astype(BF16)


def _gla_kernel(q_ref, k_ref, v_ref, gate_ref, psm_ref, w2_ref, gb_ref, nw_ref,
                sel_ref, o_ref, st_ref, c_scr, k_scr, *, chunks_per_step):
    @pl.when(pl.program_id(0) == 0)
    def _():
        st_ref[...] = jnp.zeros_like(st_ref)

    a_hi, a_mid, _ = _split3(psm_ref[...])
    z = (_dot(a_hi, w2_ref[0]) + _dot(a_mid, w2_ref[0]) + _dot(a_hi, w2_ref[1])
         + gb_ref[...])
    log_a = _log_sigmoid(z) * (1.0 / GLA_GATE_TAU)
    tril = _ones_where(_chunk_tril(CHUNK * chunks_per_step))
    c_scr[...] = _dot_exact_lhs(tril, log_a)
    k_scr[...] = k_ref[...].astype(F32)

    for ci in range(chunks_per_step):
        _gla_chunk(q_ref, k_ref, v_ref, gate_ref, nw_ref, sel_ref, o_ref, st_ref,
                   c_scr, k_scr, ci * CHUNK)


def _gla(p_big, p_sm, w2, gate_b, norm_w, sel, chunks_per_step):
    L = p_big.shape[0]
    R = CHUNK * chunks_per_step
    kern = functools.partial(_gla_kernel, chunks_per_step=chunks_per_step)
    return pl.pallas_call(
        kern,
        grid=(L // R,),
        in_specs=[
            pl.BlockSpec((R, GLA_QK), lambda n: (n, 0)),
            pl.BlockSpec((R, GLA_QK), lambda n: (n, 1)),
            pl.BlockSpec((R, GLA_V), lambda n: (n, 1)),
            pl.BlockSpec((R, GLA_V), lambda n: (n, 2)),
            pl.BlockSpec((R, LANES), lambda n: (n, 0)),
            pl.BlockSpec((2, LANES, GLA_QK), lambda n: (0, 0, 0)),
            pl.BlockSpec((1, GLA_QK), lambda n: (0, 0)),
            pl.BlockSpec((1, GLA_DV), lambda n: (0, 0)),
            pl.BlockSpec((SUB_BLOCK * GLA_DK, CHUNK), lambda n: (0, 0)),
        ],
        out_specs=pl.BlockSpec((R, GLA_V), lambda n: (n, 0)),
        out_shape=jax.ShapeDtypeStruct((L, GLA_V), BF16),
        scratch_shapes=[
            pltpu.VMEM((GLA_HEADS, GLA_DV, GLA_DK), F32),
            pltpu.VMEM((R, GLA_QK), F32),
            pltpu.VMEM((R, GLA_QK), F32),
        ],
        compiler_params=pltpu.CompilerParams(
            dimension_semantics=("arbitrary",),
            vmem_limit_bytes=VMEM_LIMIT),
        name="gla",
    )(p_big, p_big, p_big, p_big, p_sm, w2, gate_b, norm_w, sel)


def _gdn_chunks(z_ref, nw_ref, o_ref, s_ref, qkv, cum_col, cum_row, beta_col,
                chunks_per_step):
    C = CHUNK
    ti = lax.broadcasted_iota(jnp.int32, (C, 2 * C), 0)
    li = lax.broadcasted_iota(jnp.int32, (C, 2 * C), 1)
    si = li & (C - 1)
    right = li >= C
    incl = ti >= si
    strict = ti > si
    eye_right = jnp.where((ti == si) & right, 1.0, 0.0)
    tasks = [(ci, h) for ci in range(chunks_per_step) for h in range(GDN_HEADS)]

    wmat, attn, rhs, wq, kd, g_last = {}, {}, {}, {}, {}, {}
    for t in tasks:
        ci, h = t
        r0 = ci * C
        cq = qkv[r0:r0 + C, h * GDN_DK:(h + 1) * GDN_DK]
        ck = qkv[r0:r0 + C, GDN_QK + h * GDN_DK:GDN_QK + (h + 1) * GDN_DK]
        cv = qkv[r0:r0 + C, 2 * GDN_QK + h * GDN_DV:2 * GDN_QK + (h + 1) * GDN_DV]
        qn = cq * lax.rsqrt(jnp.sum(cq * cq, axis=-1, keepdims=True) + EPS) * (GDN_DK ** -0.5)
        kn = ck * lax.rsqrt(jnp.sum(ck * ck, axis=-1, keepdims=True) + EPS)
        cc = cum_col[r0:r0 + C, COL_A + h:COL_A + h + 1]
        cr = cum_row[COL_A + h:COL_A + h + 1, 2 * r0:2 * r0 + 2 * C]
        beta = beta_col[r0:r0 + C, COL_B + h:COL_B + h + 1]
        gamma = jnp.exp(jnp.where(incl, cc - cr, -jnp.inf))
        kb = kn * beta
        knb = kn.astype(BF16)
        x = _dot_nt(jnp.concatenate([qn.astype(BF16), kb.astype(BF16)], axis=0),
                    jnp.concatenate([knb, knb], axis=0))
        attn[t] = (x[:C, :C] * gamma[:, :C]).astype(BF16)
        wmat[t] = jnp.where(strict, x[C:] * gamma, 0.0)
        e_c = jnp.exp(cc)
        c_last = cc[C - 1:C, :]
        rhs1 = jnp.concatenate([cv * beta, kb * e_c], axis=1).astype(BF16)
        rhs[t] = jnp.concatenate([rhs1, rhs1], axis=0)
        wq[t] = (qn * e_c).astype(BF16)
        kd[t] = (kn * jnp.exp(c_last - cc)).astype(BF16)
        g_last[t] = jnp.exp(c_last)

    for t in tasks:
        n2b = wmat[t].astype(BF16)
        wmat[t] = jnp.where(right, eye_right - wmat[t], _dot(n2b[:, :C], n2b))
    for _ in range(5):
        for t in tasks:
            wb = wmat[t].astype(BF16)
            wmat[t] = _dot(wb[:, :C], wb) + jnp.where(right, wmat[t], 0.0)
    sol = {t: _dot(wmat[t].astype(BF16), rhs[t]) for t in tasks}

    for ci in range(chunks_per_step):
        rows = pl.ds(ci * C, C)
        heads = [(ci, h) for h in range(GDN_HEADS)]
        s_old = {t: s_ref[t[1]] for t in heads}
        ws = {t: _dot(jnp.concatenate([sol[t][:, GDN_DV:].astype(BF16), wq[t]], axis=0),
                      s_old[t].astype(BF16)) for t in heads}
        vnb = {t: (sol[t][:, :GDN_DV] - ws[t][:C]).astype(BF16) for t in heads}
        for t in heads:
            h = t[1]
            hs = slice(h * GDN_DK, (h + 1) * GDN_DK)
            o = ws[t][C:] + _dot(attn[t], vnb[t])
            s_ref[h] = g_last[t] * s_old[t] + _dot_tn(kd[t], vnb[t])
            ms = jnp.mean(o * o, axis=-1, keepdims=True)
            on = o * lax.rsqrt(ms + EPS) * nw_ref[...]
            o_ref[rows, hs] = (on * _silu(z_ref[rows, hs].astype(F32))).astype(BF16)


def _gdn_kernel(q_ref, k_ref, v_ref, z_ref, psm_ref, psmt_ref, cw_ref, arow_ref,
                acol_ref, nw_ref, o_ref, s_ref, ext_scr, *, chunks_per_step):
    R = CHUNK * chunks_per_step

    @pl.when(pl.program_id(0) == 0)
    def _():
        s_ref[...] = jnp.zeros_like(s_ref)
        ext_scr[pl.ds(0, 8), :] = jnp.zeros((8, ext_scr.shape[1]), F32)

    psm = psm_ref[...]
    g_col = -jnp.exp(arow_ref[0:1, :]) * _softplus(psm + arow_ref[1:2, :])
    beta_col = _sigmoid(psm)
    cum_col = _dot_exact_lhs(_ones_where(_chunk_tril(R)), g_col)
    g_row = -jnp.exp(acol_ref[:, 0:1]) * _softplus(psmt_ref[...] + acol_ref[:, 1:2])
    ji = lax.broadcasted_iota(jnp.int32, (R, 2 * R), 0)
    li = lax.broadcasted_iota(jnp.int32, (R, 2 * R), 1)
    dup = ((ji >> 6) == (li >> 7)) & ((ji & (CHUNK - 1)) <= (li & (CHUNK - 1)))
    cum_row = _dot_exact_rhs(g_row, _ones_where(dup))

    ext_scr[pl.ds(8, R), 0:GDN_QK] = q_ref[...].astype(F32)
    ext_scr[pl.ds(8, R), GDN_QK:2 * GDN_QK] = k_ref[...].astype(F32)
    ext_scr[pl.ds(8, R), 2 * GDN_QK:] = v_ref[...].astype(F32)
    conv = cw_ref[GDN_CONV - 1:GDN_CONV, :] * ext_scr[pl.ds(8, R), :]
    for j in range(GDN_CONV - 1):
        conv = conv + cw_ref[j:j + 1, :] * ext_scr[pl.ds(8 - (GDN_CONV - 1) + j, R), :]
    ext_scr[pl.ds(0, 8), :] = ext_scr[pl.ds(R, 8), :]
    qkv = _silu(conv)

    _gdn_chunks(z_ref, nw_ref, o_ref, s_ref, qkv, cum_col, cum_row, beta_col, chunks_per_step)


def _gdn(p_big, p_sm, p_smt, conv_w, a_row, a_col, norm_w, chunks_per_step):
    L = p_big.shape[0]
    R = CHUNK * chunks_per_step
    base = (GLA_QK * 2 + GLA_V * 2) // GDN_QK
    kern = functools.partial(_gdn_kernel, chunks_per_step=chunks_per_step)
    return pl.pallas_call(
        kern,
        grid=(L // R,),
        in_specs=[
            pl.BlockSpec((R, GDN_QK), lambda n: (n, base)),
            pl.BlockSpec((R, GDN_QK), lambda n: (n, base + 1)),
            pl.BlockSpec((R, GDN_V), lambda n: (n, base + 2)),
            pl.BlockSpec((R, GDN_V), lambda n: (n, base + 3)),
            pl.BlockSpec((R, LANES), lambda n: (n, 0)),
            pl.BlockSpec((LANES, R), lambda n: (0, n)),
            pl.BlockSpec((GDN_CONV, 2 * GDN_QK + GDN_V), lambda n: (0, 0)),
            pl.BlockSpec((2, LANES), lambda n: (0, 0)),
            pl.BlockSpec((LANES, 2), lambda n: (0, 0)),
            pl.BlockSpec((1, GDN_DV), lambda n: (0, 0)),
        ],
        out_specs=pl.BlockSpec((R, GDN_V), lambda n: (n, 0)),
        out_shape=jax.ShapeDtypeStruct((L, GDN_V), BF16),
        scratch_shapes=[
            pltpu.VMEM((GDN_HEADS, GDN_DK, GDN_DV), F32),
            pltpu.VMEM((R + 8, 2 * GDN_QK + GDN_V), F32),
        ],
        compiler_params=pltpu.CompilerParams(
            dimension_semantics=("arbitrary",),
            vmem_limit_bytes=VMEM_LIMIT),
        name="gdn",
    )(p_big, p_big, p_big, p_big, p_sm, p_smt, conv_w, a_row, a_col, norm_w)


def _outproj_kernel(x_ref, oa_ref, ob_ref, wa_ref, wb_ref, y_ref):
    y_ref[...] = (x_ref[...] + _dot(oa_ref[...], wa_ref[...])
                  + _dot(ob_ref[...], wb_ref[...]))


def _outproj(x2, o_a, o_b, w_out_b, tm):
    L, D = x2.shape
    return pl.pallas_call(
        _outproj_kernel,
        grid=(L // tm,),
        in_specs=[
            pl.BlockSpec((tm, D), lambda i: (i, 0)),
            pl.BlockSpec((tm, GLA_V), lambda i: (i, 0)),
            pl.BlockSpec((tm, GDN_V), lambda i: (i, 0)),
            pl.BlockSpec((GLA_V, D), lambda i: (0, 0)),
            pl.BlockSpec((GDN_V, D), lambda i: (GLA_V // GDN_V, 0)),
        ],
        out_specs=pl.BlockSpec((tm, D), lambda i: (i, 0)),
        out_shape=jax.ShapeDtypeStruct((L, D), F32),
        compiler_params=pltpu.CompilerParams(
            dimension_semantics=("arbitrary",),
            vmem_limit_bytes=VMEM_LIMIT),
        name="outproj",
    )(x2, o_a, o_b, w_out_b, w_out_b)


def _ffn_kernel(x_ref, fnw_ref, wg_ref, wu_ref, wd_ref, onw_ref, y_ref, hf_ref, acc_ref):
    f = pl.program_id(1)

    @pl.when(f == 0)
    def _():
        x = x_ref[...]
        ms = jnp.mean(x * x, axis=-1, keepdims=True)
        hf_ref[...] = (x * lax.rsqrt(ms + EPS) * fnw_ref[...]).astype(BF16)
        acc_ref[...] = jnp.zeros_like(acc_ref)

    hf = hf_ref[...]
    g = _dot(hf, wg_ref[...])
    u = _dot(hf, wu_ref[...])
    acc_ref[...] += _dot((_silu(g) * u).astype(BF16), wd_ref[...])

    @pl.when(f == pl.num_programs(1) - 1)
    def _():
        r = x_ref[...] + acc_ref[...]
        ms = jnp.mean(r * r, axis=-1, keepdims=True)
        y_ref[...] = r * lax.rsqrt(ms + EPS) * onw_ref[...]


def _ffn(x1, ffn_norm_w, wg, wu, wd, final_norm_w, tm, tf):
    L, D = x1.shape
    F = wg.shape[1]
    return pl.pallas_call(
        _ffn_kernel,
        grid=(L // tm, F // tf),
        in_specs=[
            pl.BlockSpec((tm, D), lambda i, f: (i, 0)),
            pl.BlockSpec((1, D), lambda i, f: (0, 0)),
            pl.BlockSpec((D, tf), lambda i, f: (0, f)),
            pl.BlockSpec((D, tf), lambda i, f: (0, f)),
            pl.BlockSpec((tf, D), lambda i, f: (f, 0)),
            pl.BlockSpec((1, D), lambda i, f: (0, 0)),
        ],
        out_specs=pl.BlockSpec((tm, D), lambda i, f: (i, 0)),
        out_shape=jax.ShapeDtypeStruct((L, D), F32),
        scratch_shapes=[pltpu.VMEM((tm, D), BF16), pltpu.VMEM((tm, D), F32)],
        compiler_params=pltpu.CompilerParams(
            dimension_semantics=("arbitrary", "arbitrary"),
            vmem_limit_bytes=VMEM_LIMIT),
        name="ffn",
    )(x1, ffn_norm_w, wg, wu, wd, final_norm_w)


def _hi_lo(w):
    hi = w.astype(BF16)
    lo = (w - hi.astype(F32)).astype(BF16)
    return jnp.stack([hi, lo])


def _diag_selector():
    s_of_row = np.arange(SUB_BLOCK * GLA_DK) // GLA_DK
    n_mod = np.arange(CHUNK) % SUB_BLOCK
    return jnp.asarray(s_of_row[:, None] == n_mod[None, :], dtype=BF16)


def _pick(n, candidates):
    for c in candidates:
        if n % c == 0:
            return c
    raise ValueError(f"no tile in {candidates} divides {n}")


def kernel(x, attn_norm_w, w_in, gla_gate_w2, gla_gate_b, gla_norm_w, gdn_conv_w,
           gdn_a_log, gdn_dt_bias, gdn_norm_w, w_out, ffn_norm_w, w_gate, w_up,
           w_down, final_norm_w):
    bsz, seq, d_model = x.shape
    assert bsz == 1 and seq % CHUNK == 0
    x2 = x.reshape(seq, d_model).astype(F32)

    sizes = (GLA_QK, GLA_QK, GLA_V, GLA_GATE_RANK, GLA_V,
             GDN_QK, GDN_QK, GDN_V, GDN_HEADS, GDN_HEADS, GDN_V)
    offs = np.concatenate([[0], np.cumsum(sizes)])
    assert w_in.shape == (d_model, offs[-1])
    w_in_t = w_in.astype(F32).T
    row = lambda i: w_in_t[offs[i]:offs[i + 1]]
    n_small = GLA_GATE_RANK + 2 * GDN_HEADS
    w_small_t = jnp.concatenate([row(3), row(8), row(9)], axis=0)
    w_small_t = jnp.pad(w_small_t, ((0, LANES - n_small), (0, 0)))
    w_smt = _hi_lo(w_small_t).reshape(2 * LANES, d_model)

    tm1 = _pick(seq, (1024, 512, 256, 128, 64))
    p_big, p_sm, p_smt, (w_out_b, w_gate_b, w_up_b, w_down_b) = _inproj(
        x2, attn_norm_w.reshape(1, -1).astype(F32), _wprep(w_in_t, 512), w_smt,
        [w.astype(F32) for w in (w_out, w_gate, w_up, w_down)], tm1, 512)

    cps = 2 if seq % (2 * CHUNK) == 0 else 1

    w2_pad = jnp.pad(gla_gate_w2.astype(F32), ((0, LANES - GLA_GATE_RANK), (0, 0)))
    o_a = _gla(p_big, p_sm, _hi_lo(w2_pad), gla_gate_b.reshape(1, -1).astype(F32),
               gla_norm_w.reshape(1, -1).astype(F32), _diag_selector(), cps)

    a_log_pad = jnp.zeros((LANES,), F32).at[COL_A:COL_A + GDN_HEADS].set(gdn_a_log.astype(F32))
    dt_pad = jnp.zeros((LANES,), F32).at[COL_A:COL_A + GDN_HEADS].set(gdn_dt_bias.astype(F32))
    a_row = jnp.stack([a_log_pad, dt_pad])
    o_b = _gdn(p_big, p_sm, p_smt, gdn_conv_w.astype(F32), a_row, a_row.T,
               gdn_norm_w.reshape(1, -1).astype(F32), 4 if seq % (4 * CHUNK) == 0 else cps)

    tm4 = _pick(seq, (512, 256, 128, 64))
    x1 = _outproj(x2, o_a, o_b, w_out_b, tm4)

    y = _ffn(x1, ffn_norm_w.reshape(1, -1).astype(F32), w_gate_b, w_up_b, w_down_b,
             final_norm_w.reshape(1, -1).astype(F32), tm4, 512)
    return y.reshape(bsz, seq, d_model).astype(x.dtype)
```

```python
import functools

import numpy as np
import jax
import jax.numpy as jnp
from jax import lax
from jax.experimental import pallas as pl
from jax.experimental.pallas import tpu as pltpu

F32 = jnp.float32
BF16 = jnp.bfloat16

EPS = 1e-6
CHUNK = 64

GLA_HEADS = 4
GLA_DK = 128
GLA_DV = 256
GLA_GATE_RANK = 16
GLA_GATE_TAU = 16.0
GLA_QK = GLA_HEADS * GLA_DK
GLA_V = GLA_HEADS * GLA_DV

GDN_HEADS = 8
GDN_DK = 128
GDN_DV = 128
GDN_CONV = 4
GDN_QK = GDN_HEADS * GDN_DK
GDN_V = GDN_HEADS * GDN_DV

LANES = 128
SUB_BLOCK = 16
VMEM_LIMIT = 48 * 1024 * 1024
N_BIG = 2 * GLA_QK + 2 * GLA_V + 2 * GDN_QK + 2 * GDN_V

COL_LR = 0
COL_A = GLA_GATE_RANK
COL_B = GLA_GATE_RANK + GDN_HEADS


def _dot(a, b):
    return jnp.dot(a, b, preferred_element_type=F32)


def _dot_nt(a, b):
    return lax.dot_general(a, b, (((1,), (1,)), ((), ())), preferred_element_type=F32)


def _dot_tn(a, b):
    return lax.dot_general(a, b, (((0,), (0,)), ((), ())), preferred_element_type=F32)


def _split3(a):
    hi = a.astype(BF16)
    r1 = a - hi.astype(F32)
    mid = r1.astype(BF16)
    lo = (r1 - mid.astype(F32)).astype(BF16)
    return hi, mid, lo


def _dot_exact_rhs(a, b_exact):
    hi, mid, lo = _split3(a)
    return _dot(hi, b_exact) + _dot(mid, b_exact) + _dot(lo, b_exact)


def _dot_exact_lhs(a_exact, b):
    hi, mid, lo = _split3(b)
    return _dot(a_exact, hi) + _dot(a_exact, mid) + _dot(a_exact, lo)


def _sigmoid(x):
    return 1.0 / (1.0 + jnp.exp(-x))


def _silu(x):
    return x * _sigmoid(x)


def _softplus(x):
    return jnp.maximum(x, 0.0) + jnp.log1p(jnp.exp(-jnp.abs(x)))


def _log_sigmoid(x):
    return -_softplus(-x)


def _wprep_kernel(wa_ref, wb_ref, o_ref, *, shift_steps):
    j = pl.program_id(0)
    j1, j2 = shift_steps

    def realigned(delta):
        if delta == 0:
            return wa_ref[...].astype(BF16)
        return jnp.concatenate([wa_ref[pl.ds(delta, wa_ref.shape[0] - delta), :],
                                wb_ref[pl.ds(0, delta), :]], axis=0).astype(BF16)

    @pl.when(j < j1)
    def _():
        o_ref[...] = realigned(0)

    @pl.when((j >= j1) & (j < j2))
    def _():
        o_ref[...] = realigned(GLA_GATE_RANK)

    @pl.when(j >= j2)
    def _():
        o_ref[...] = realigned(GLA_GATE_RANK + 2 * GDN_HEADS)


def _wprep(w_in_t, tn):
    D = w_in_t.shape[1]
    n_small = GLA_GATE_RANK + 2 * GDN_HEADS
    g1 = GLA_QK * 2 + GLA_V
    g2 = g1 + GLA_V + GDN_QK * 2 + GDN_V
    assert g1 % tn == 0 and g2 % tn == 0 and N_BIG % tn == 0 and tn % n_small == 0
    return pl.pallas_call(
        functools.partial(_wprep_kernel, shift_steps=(g1 // tn, g2 // tn)),
        grid=(N_BIG // tn,),
        in_specs=[
            pl.BlockSpec((tn, D), lambda j: (j, 0)),
            pl.BlockSpec((n_small, D), lambda j: ((tn // n_small) * (j + 1), 0)),
        ],
        out_specs=pl.BlockSpec((tn, D), lambda j: (j, 0)),
        out_shape=jax.ShapeDtypeStruct((N_BIG, D), BF16),
        compiler_params=pltpu.CompilerParams(
            dimension_semantics=("arbitrary",),
            vmem_limit_bytes=VMEM_LIMIT),
        name="wprep",
    )(w_in_t, w_in_t)


def _inproj_kernel(x_ref, nw_ref, wbig_ref, wsmt_ref, cw_ref, *rest, n_side, tile_kinds):
    side_in = rest[:n_side]
    pbig_ref, psm_ref, psmt_ref = rest[n_side:n_side + 3]
    side_out = rest[n_side + 3:2 * n_side + 3]
    h_ref, ext_scr, tail_scr = rest[2 * n_side + 3:]
    i = pl.program_id(0)
    j = pl.program_id(1)
    tm = h_ref.shape[0]
    tn = wbig_ref.shape[0]
    piece = min(tm, 256)

    for src, dst in zip(side_in, side_out):
        dst[...] = src[...].astype(BF16)

    @pl.when(j == 0)
    def _():
        x = x_ref[...]
        ms = jnp.mean(x * x, axis=-1, keepdims=True)
        h = x * lax.rsqrt(ms + EPS) * nw_ref[...]
        hb = h.astype(BF16)
        h_ref[...] = hb
        hl = (h - hb.astype(F32)).astype(BF16)
        r = _dot_nt(wsmt_ref[...], hb)
        pt = r[:LANES] + r[LANES:] + _dot_nt(wsmt_ref[pl.ds(0, LANES), :], hl)
        psmt_ref[...] = pt
        psm_ref[...] = pt.T

    def piece_dot(r0):
        return _dot_nt(h_ref[pl.ds(r0, piece), :], wbig_ref[...])

    def steps_of(*kinds):
        pred = None
        for jj, kind in enumerate(tile_kinds):
            if kind in kinds:
                pred = (j == jj) if pred is None else (pred | (j == jj))
        return pred

    @pl.when(steps_of('plain'))
    def _():
        pbig_ref[...] = _dot_nt(h_ref[...], wbig_ref[...]).astype(BF16)

    @pl.when(steps_of('silu'))
    def _():
        for r0 in range(0, tm, piece):
            pbig_ref[pl.ds(r0, piece), :] = _silu(piece_dot(r0)).astype(BF16)

    conv_kinds = ('conv_norm_q', 'conv_norm_k', 'conv')
    first_conv = min(jj for jj, kind in enumerate(tile_kinds) if kind in conv_kinds)

    def conv_steps(normalise):
        jc = j - first_conv
        ext_scr[pl.ds(0, 8), :] = jnp.where(i == 0, 0.0, tail_scr[jc])
        scale = jnp.where(steps_of('conv_norm_q'), GDN_DK ** -0.5, 1.0) if normalise else None
        for r0 in range(0, tm, piece):
            ext_scr[pl.ds(8 + r0, piece), :] = piece_dot(r0)
            acc = cw_ref[GDN_CONV - 1:GDN_CONV, :] * ext_scr[pl.ds(8 + r0, piece), :]
            for t in range(GDN_CONV - 1):
                acc = acc + cw_ref[t:t + 1, :] * ext_scr[pl.ds(8 - (GDN_CONV - 1) + t + r0, piece), :]
            y = _silu(acc)
            if normalise:
                heads = []
                for hh in range(tn // GDN_DK):
                    yh = y[:, hh * GDN_DK:(hh + 1) * GDN_DK]
                    ssq = jnp.sum(yh * yh, axis=-1, keepdims=True)
                    heads.append(yh * (lax.rsqrt(ssq + EPS) * scale))
                y = jnp.concatenate(heads, axis=1)
            pbig_ref[pl.ds(r0, piece), :] = y.astype(BF16)
        tail_scr[jc] = ext_scr[pl.ds(tm, 8), :]

    @pl.when(steps_of('conv_norm_q', 'conv_norm_k'))
    def _():
        conv_steps(True)

    @pl.when(steps_of('conv'))
    def _():
        conv_steps(False)


def _max_split(n_steps, size, quantum):
    for nb in range(n_steps, 0, -1):
        if size % (nb * quantum) == 0:
            return nb
    raise ValueError(f"{size} is not a multiple of {quantum}")


def _inproj(x2, norm_w, w_big, w_smt, conv_w, side_weights, tm, tn):
    L, D = x2.shape
    ni, nj = L // tm, N_BIG // tn
    widths = (('plain', 2 * GLA_QK + GLA_V), ('silu', GLA_V), ('conv_norm_q', GDN_QK),
              ('conv_norm_k', GDN_QK), ('conv', GDN_V), ('silu', GDN_V))
    assert all(width % tn == 0 for _, width in widths) and tn % GDN_DK == 0
    tile_kinds = tuple(kind for kind, width in widths for _ in range(width // tn))
    conv_tiles = [jj for jj, kind in enumerate(tile_kinds) if kind.startswith('conv')]
    assert conv_tiles == list(range(conv_tiles[0], conv_tiles[-1] + 1))
    side_specs, side_shapes = [], []
    for w in side_weights:
        rb = _max_split(ni, w.shape[0], 16)
        cb = _max_split(nj, w.shape[1], LANES)
        imap = functools.partial(
            lambda i, j, rb, cb: (jnp.minimum(i, rb - 1), jnp.minimum(j, cb - 1)), rb=rb, cb=cb)
        side_specs.append(pl.BlockSpec((w.shape[0] // rb, w.shape[1] // cb), imap))
        side_shapes.append(jax.ShapeDtypeStruct(w.shape, BF16))
    outs = pl.pallas_call(
        functools.partial(_inproj_kernel, n_side=len(side_weights), tile_kinds=tile_kinds),
        grid=(ni, nj),
        in_specs=[
            pl.BlockSpec((tm, D), lambda i, j: (i, 0)),
            pl.BlockSpec((1, D), lambda i, j: (0, 0)),
            pl.BlockSpec((tn, D), lambda i, j: (j, 0)),
            pl.BlockSpec((2 * LANES, D), lambda i, j: (0, 0)),
            pl.BlockSpec((GDN_CONV, tn),
                         lambda i, j: (0, jnp.clip(j - conv_tiles[0], 0, len(conv_tiles) - 1))),
        ] + side_specs,
        out_specs=[
            pl.BlockSpec((tm, tn), lambda i, j: (i, j)),
            pl.BlockSpec((tm, LANES), lambda i, j: (i, 0)),
            pl.BlockSpec((LANES, tm), lambda i, j: (0, i)),
        ] + side_specs,
        out_shape=[
            jax.ShapeDtypeStruct((L, N_BIG), BF16),
            jax.ShapeDtypeStruct((L, LANES), F32),
            jax.ShapeDtypeStruct((LANES, L), F32),
        ] + side_shapes,
        scratch_shapes=[pltpu.VMEM((tm, D), BF16),
                        pltpu.VMEM((tm + 8, tn), F32),
                        pltpu.VMEM((len(conv_tiles), 8, tn), F32)],
        compiler_params=pltpu.CompilerParams(
            dimension_semantics=("arbitrary", "arbitrary"),
            vmem_limit_bytes=VMEM_LIMIT),
        name="inproj",
    )(x2, norm_w, w_big, w_smt, conv_w, *side_weights)
    return outs[0], outs[1], outs[2], outs[3:]


def _ones_where(mask):
    return jnp.where(mask, 1.0, 0.0).astype(BF16)


def _chunk_tril(rows, upper=False):
    ti = lax.broadcasted_iota(jnp.int32, (rows, rows), 0)
    si = lax.broadcasted_iota(jnp.int32, (rows, rows), 1)
    same = (ti >> 6) == (si >> 6)
    return same & ((ti <= si) if upper else (ti >= si))


def _gla_chunk(q_ref, k_ref, v_ref, gate_ref, nw_ref, sel_ref, o_ref, st_ref,
               c_scr, k_scr, r0):
    C = CHUNK
    rows = pl.ds(r0, C)
    row = lax.broadcasted_iota(jnp.int32, (C, GLA_DK), 0)
    ti = lax.broadcasted_iota(jnp.int32, (C, C), 0)
    si = lax.broadcasted_iota(jnp.int32, (C, C), 1)
    in_hi32 = row >= 32
    right16 = (row & 31) >= 16
    row_lt32 = row < 32
    same_pair = (ti >> 5) == (si >> 5)
    same_blk = (ti >> 4) == (si >> 4)
    trow = lax.broadcasted_iota(jnp.int32, (SUB_BLOCK, GLA_DK), 0)

    for h in range(GLA_HEADS):
        ks = slice(h * GLA_DK, (h + 1) * GLA_DK)
        vs = slice(h * GLA_DV, (h + 1) * GLA_DV)
        q = q_ref[rows, ks].astype(F32) * (GLA_DK ** -0.5)
        k = k_scr[rows, ks]
        v = v_ref[rows, vs]
        c = c_scr[rows, ks]

        ref = c_scr[pl.ds(r0 + 31, 1), ks]
        e = jnp.exp(jnp.where(in_hi32, c - ref, ref - c))
        qt = jnp.where(in_hi32, q * e, 0.0).astype(BF16)
        kt = jnp.where(in_hi32, 0.0, k * e).astype(BF16)
        a = _dot_nt(qt, kt)
        ref = jnp.where(row_lt32, c_scr[pl.ds(r0 + 15, 1), ks], c_scr[pl.ds(r0 + 47, 1), ks])
        e = jnp.exp(jnp.where(right16, c - ref, ref - c))
        qt = jnp.where(right16, q * e, 0.0).astype(BF16)
        kt = jnp.where(right16, 0.0, k * e).astype(BF16)
        a = a + jnp.where(same_pair, _dot_nt(qt, kt), 0.0)
        blocks = []
        for b in range(C // SUB_BLOCK):
            b0 = b * SUB_BLOCK
            cb = c[b0:b0 + SUB_BLOCK]
            qb = q[b0:b0 + SUB_BLOCK]
            parts = []
            for s in range(SUB_BLOCK):
                c_s = c_scr[pl.ds(r0 + b0 + s, 1), ks]
                k_s = k_scr[pl.ds(r0 + b0 + s, 1), ks]
                e = jnp.exp(jnp.where(trow >= s, cb - c_s, -jnp.inf))
                parts.append(((qb * e) * k_s).astype(BF16))
            blocks.append(jnp.concatenate(parts, axis=1))
        p2 = jnp.concatenate(blocks, axis=0)
        a = a + jnp.where(same_blk, _dot(p2, sel_ref[...]), 0.0)

        st = st_ref[h]
        c_last = c_scr[pl.ds(r0 + C - 1, 1), ks]
        qd = (q * jnp.exp(c)).astype(BF16)
        o = _dot(a.astype(BF16), v) + _dot_nt(qd, st.astype(BF16))
        kd = (k * jnp.exp(c_last - c)).astype(BF16)
        st_ref[h] = jnp.exp(c_last) * st + _dot_tn(v, kd)

        ms = jnp.mean(o * o, axis=-1, keepdims=True)
        on = o * lax.rsqrt(ms + EPS) * nw_ref[...]
        o_ref[rows, vs] = (on * gate_ref[rows, vs].astype(F32)).astype(BF16)


def _gla_kernel(q_ref, k_ref, v_ref, gate_ref, psm_ref, w2_ref, gb_ref, nw_ref,
                sel_ref, o_ref, st_ref, c_scr, k_scr, *, chunks_per_step):
    @pl.when(pl.program_id(0) == 0)
    def _():
        st_ref[...] = jnp.zeros_like(st_ref)

    a_hi, a_mid, _ = _split3(psm_ref[...])
    z = (_dot(a_hi, w2_ref[0]) + _dot(a_mid, w2_ref[0]) + _dot(a_hi, w2_ref[1])
         + gb_ref[...])
    log_a = _log_sigmoid(z) * (1.0 / GLA_GATE_TAU)
    tril = _ones_where(_chunk_tril(CHUNK * chunks_per_step))
    c_scr[...] = _dot_exact_lhs(tril, log_a)
    k_scr[...] = k_ref[...].astype(F32)

    for ci in range(chunks_per_step):
        _gla_chunk(q_ref, k_ref, v_ref, gate_ref, nw_ref, sel_ref, o_ref, st_ref,
                   c_scr, k_scr, ci * CHUNK)


def _gla(p_big, p_sm, w2, gate_b, norm_w, sel, chunks_per_step):
    L = p_big.shape[0]
    R = CHUNK * chunks_per_step
    kern = functools.partial(_gla_kernel, chunks_per_step=chunks_per_step)
    return pl.pallas_call(
        kern,
        grid=(L // R,),
        in_specs=[
            pl.BlockSpec((R, GLA_QK), lambda n: (n, 0)),
            pl.BlockSpec((R, GLA_QK), lambda n: (n, 1)),
            pl.BlockSpec((R, GLA_V), lambda n: (n, 1)),
            pl.BlockSpec((R, GLA_V), lambda n: (n, 2)),
            pl.BlockSpec((R, LANES), lambda n: (n, 0)),
            pl.BlockSpec((2, LANES, GLA_QK), lambda n: (0, 0, 0)),
            pl.BlockSpec((1, GLA_QK), lambda n: (0, 0)),
            pl.BlockSpec((1, GLA_DV), lambda n: (0, 0)),
            pl.BlockSpec((SUB_BLOCK * GLA_DK, CHUNK), lambda n: (0, 0)),
        ],
        out_specs=pl.BlockSpec((R, GLA_V), lambda n: (n, 0)),
        out_shape=jax.ShapeDtypeStruct((L, GLA_V), BF16),
        scratch_shapes=[
            pltpu.VMEM((GLA_HEADS, GLA_DV, GLA_DK), F32),
            pltpu.VMEM((R, GLA_QK), F32),
            pltpu.VMEM((R, GLA_QK), F32),
        ],
        compiler_params=pltpu.CompilerParams(
            dimension_semantics=("arbitrary",),
            vmem_limit_bytes=VMEM_LIMIT),
        name="gla",
    )(p_big, p_big, p_big, p_big, p_sm, w2, gate_b, norm_w, sel)


def _gdn_chunks(q_ref, k_ref, v_ref, z_ref, nw_ref, o_ref, s_ref, cum_col, cum_row, beta_col,
                chunks_per_step):
    C = CHUNK
    ti = lax.broadcasted_iota(jnp.int32, (C, 2 * C), 0)
    li = lax.broadcasted_iota(jnp.int32, (C, 2 * C), 1)
    si = li & (C - 1)
    right = li >= C
    incl = ti >= si
    strict = ti > si
    eye_right = jnp.where((ti == si) & right, 1.0, 0.0)
    tasks = [(ci, h) for ci in range(chunks_per_step) for h in range(GDN_HEADS)]

    wmat, attn, rhs, wq, kd, g_last = {}, {}, {}, {}, {}, {}
    for t in tasks:
        ci, h = t
        r0 = ci * C
        hs = slice(h * GDN_DK, (h + 1) * GDN_DK)
        qnb = q_ref[pl.ds(r0, C), hs]
        knb = k_ref[pl.ds(r0, C), hs]
        kn = knb.astype(F32)
        cv = v_ref[pl.ds(r0, C), hs].astype(F32)
        cc = cum_col[r0:r0 + C, COL_A + h:COL_A + h + 1]
        cr = cum_row[COL_A + h:COL_A + h + 1, 2 * r0:2 * r0 + 2 * C]
        beta = beta_col[r0:r0 + C, COL_B + h:COL_B + h + 1]
        gamma = jnp.exp(jnp.where(incl, cc - cr, -jnp.inf))
        kb = kn * beta
        x = _dot_nt(jnp.concatenate([qnb, kb.astype(BF16)], axis=0),
                    jnp.concatenate([knb, knb], axis=0))
        attn[t] = (x[:C, :C] * gamma[:, :C]).astype(BF16)
        wmat[t] = jnp.where(strict, x[C:] * gamma, 0.0)
        e_c = jnp.exp(cc)
        c_last = cc[C - 1:C, :]
        rhs1 = jnp.concatenate([cv * beta, kb * e_c], axis=1).astype(BF16)
        rhs[t] = jnp.concatenate([rhs1, rhs1], axis=0)
        wq[t] = (qnb.astype(F32) * e_c).astype(BF16)
        kd[t] = (kn * jnp.exp(c_last - cc)).astype(BF16)
        g_last[t] = jnp.exp(c_last)

    for t in tasks:
        n2b = wmat[t].astype(BF16)
        wmat[t] = jnp.where(right, eye_right - wmat[t], _dot(n2b[:, :C], n2b))
    for _ in range(5):
        for t in tasks:
            wb = wmat[t].astype(BF16)
            wmat[t] = _dot(wb[:, :C], wb) + jnp.where(right, wmat[t], 0.0)
    sol = {t: _dot(wmat[t].astype(BF16), rhs[t]) for t in tasks}

    for ci in range(chunks_per_step):
        rows = pl.ds(ci * C, C)
        heads = [(ci, h) for h in range(GDN_HEADS)]
        s_old = {t: s_ref[t[1]] for t in heads}
        ws = {t: _dot(jnp.concatenate([sol[t][:, GDN_DV:].astype(BF16), wq[t]], axis=0),
                      s_old[t].astype(BF16)) for t in heads}
        vnb = {t: (sol[t][:, :GDN_DV] - ws[t][:C]).astype(BF16) for t in heads}
        for t in heads:
            h = t[1]
            hs = slice(h * GDN_DK, (h + 1) * GDN_DK)
            o = ws[t][C:] + _dot(attn[t], vnb[t])
            s_ref[h] = g_last[t] * s_old[t] + _dot_tn(kd[t], vnb[t])
            ms = jnp.mean(o * o, axis=-1, keepdims=True)
            on = o * lax.rsqrt(ms + EPS) * nw_ref[...]
            o_ref[rows, hs] = (on * z_ref[rows, hs].astype(F32)).astype(BF16)


def _gdn_kernel(q_ref, k_ref, v_ref, z_ref, psm_ref, psmt_ref, arow_ref,
                acol_ref, nw_ref, o_ref, s_ref, *, chunks_per_step):
    R = CHUNK * chunks_per_step

    @pl.when(pl.program_id(0) == 0)
    def _():
        s_ref[...] = jnp.zeros_like(s_ref)

    psm = psm_ref[...]
    g_col = -jnp.exp(arow_ref[0:1, :]) * _softplus(psm + arow_ref[1:2, :])
    beta_col = _sigmoid(psm)
    cum_col = _dot_exact_lhs(_ones_where(_chunk_tril(R)), g_col)
    g_row = -jnp.exp(acol_ref[:, 0:1]) * _softplus(psmt_ref[...] + acol_ref[:, 1:2])
    ji = lax.broadcasted_iota(jnp.int32, (R, 2 * R), 0)
    li = lax.broadcasted_iota(jnp.int32, (R, 2 * R), 1)
    dup = ((ji >> 6) == (li >> 7)) & ((ji & (CHUNK - 1)) <= (li & (CHUNK - 1)))
    cum_row = _dot_exact_rhs(g_row, _ones_where(dup))

    _gdn_chunks(q_ref, k_ref, v_ref, z_ref, nw_ref, o_ref, s_ref, cum_col, cum_row, beta_col,
                chunks_per_step)


def _gdn(p_big, p_sm, p_smt, a_row, a_col, norm_w, chunks_per_step):
    L = p_big.shape[0]
    R = CHUNK * chunks_per_step
    base = (GLA_QK * 2 + GLA_V * 2) // GDN_QK
    kern = functools.partial(_gdn_kernel, chunks_per_step=chunks_per_step)
    return pl.pallas_call(
        kern,
        grid=(L // R,),
        in_specs=[
            pl.BlockSpec((R, GDN_QK), lambda n: (n, base)),
            pl.BlockSpec((R, GDN_QK), lambda n: (n, base + 1)),
            pl.BlockSpec((R, GDN_V), lambda n: (n, base + 2)),
            pl.BlockSpec((R, GDN_V), lambda n: (n, base + 3)),
            pl.BlockSpec((R, LANES), lambda n: (n, 0)),
            pl.BlockSpec((LANES, R), lambda n: (0, n)),
            pl.BlockSpec((2, LANES), lambda n: (0, 0)),
            pl.BlockSpec((LANES, 2), lambda n: (0, 0)),
            pl.BlockSpec((1, GDN_DV), lambda n: (0, 0)),
        ],
        out_specs=pl.BlockSpec((R, GDN_V), lambda n: (n, 0)),
        out_shape=jax.ShapeDtypeStruct((L, GDN_V), BF16),
        scratch_shapes=[pltpu.VMEM((GDN_HEADS, GDN_DK, GDN_DV), F32)],
        compiler_params=pltpu.CompilerParams(
            dimension_semantics=("arbitrary",),
            vmem_limit_bytes=VMEM_LIMIT),
        name="gdn",
    )(p_big, p_big, p_big, p_big, p_sm, p_smt, a_row, a_col, norm_w)


def _outproj_kernel(x_ref, oa_ref, ob_ref, wa_ref, wb_ref, y_ref):
    y_ref[...] = (x_ref[...] + _dot(oa_ref[...], wa_ref[...])
                  + _dot(ob_ref[...], wb_ref[...]))


def _outproj(x2, o_a, o_b, w_out_b, tm):
    L, D = x2.shape
    return pl.pallas_call(
        _outproj_kernel,
        grid=(L // tm,),
        in_specs=[
            pl.BlockSpec((tm, D), lambda i: (i, 0)),
            pl.BlockSpec((tm, GLA_V), lambda i: (i, 0)),
            pl.BlockSpec((tm, GDN_V), lambda i: (i, 0)),
            pl.BlockSpec((GLA_V, D), lambda i: (0, 0)),
            pl.BlockSpec((GDN_V, D), lambda i: (GLA_V // GDN_V, 0)),
        ],
        out_specs=pl.BlockSpec((tm, D), lambda i: (i, 0)),
        out_shape=jax.ShapeDtypeStruct((L, D), F32),
        compiler_params=pltpu.CompilerParams(
            dimension_semantics=("arbitrary",),
            vmem_limit_bytes=VMEM_LIMIT),
        name="outproj",
    )(x2, o_a, o_b, w_out_b, w_out_b)


def _ffn_kernel(x_ref, fnw_ref, wg_ref, wu_ref, wd_ref, onw_ref, y_ref, hf_ref, acc_ref):
    f = pl.program_id(1)

    @pl.when(f == 0)
    def _():
        x = x_ref[...]
        ms = jnp.mean(x * x, axis=-1, keepdims=True)
        hf_ref[...] = (x * lax.rsqrt(ms + EPS) * fnw_ref[...]).astype(BF16)
        acc_ref[...] = jnp.zeros_like(acc_ref)

    hf = hf_ref[...]
    g = _dot(hf, wg_ref[...])
    u = _dot(hf, wu_ref[...])
    acc_ref[...] += _dot((_silu(g) * u).astype(BF16), wd_ref[...])

    @pl.when(f == pl.num_programs(1) - 1)
    def _():
        r = x_ref[...] + acc_ref[...]
        ms = jnp.mean(r * r, axis=-1, keepdims=True)
        y_ref[...] = r * lax.rsqrt(ms + EPS) * onw_ref[...]


def _ffn(x1, ffn_norm_w, wg, wu, wd, final_norm_w, tm, tf):
    L, D = x1.shape
    F = wg.shape[1]
    return pl.pallas_call(
        _ffn_kernel,
        grid=(L // tm, F // tf),
        in_specs=[
            pl.BlockSpec((tm, D), lambda i, f: (i, 0)),
            pl.BlockSpec((1, D), lambda i, f: (0, 0)),
            pl.BlockSpec((D, tf), lambda i, f: (0, f)),
            pl.BlockSpec((D, tf), lambda i, f: (0, f)),
            pl.BlockSpec((tf, D), lambda i, f: (f, 0)),
            pl.BlockSpec((1, D), lambda i, f: (0, 0)),
        ],
        out_specs=pl.BlockSpec((tm, D), lambda i, f: (i, 0)),
        out_shape=jax.ShapeDtypeStruct((L, D), F32),
        scratch_shapes=[pltpu.VMEM((tm, D), BF16), pltpu.VMEM((tm, D), F32)],
        compiler_params=pltpu.CompilerParams(
            dimension_semantics=("arbitrary", "arbitrary"),
            vmem_limit_bytes=VMEM_LIMIT),
        name="ffn",
    )(x1, ffn_norm_w, wg, wu, wd, final_norm_w)


def _hi_lo(w):
    hi = w.astype(BF16)
    lo = (w - hi.astype(F32)).astype(BF16)
    return jnp.stack([hi, lo])


def _diag_selector():
    s_of_row = np.arange(SUB_BLOCK * GLA_DK) // GLA_DK
    n_mod = np.arange(CHUNK) % SUB_BLOCK
    return jnp.asarray(s_of_row[:, None] == n_mod[None, :], dtype=BF16)


def _pick(n, candidates):
    for c in candidates:
        if n % c == 0:
            return c
    raise ValueError(f"no tile in {candidates} divides {n}")


def kernel(x, attn_norm_w, w_in, gla_gate_w2, gla_gate_b, gla_norm_w, gdn_conv_w,
           gdn_a_log, gdn_dt_bias, gdn_norm_w, w_out, ffn_norm_w, w_gate, w_up,
           w_down, final_norm_w):
    bsz, seq, d_model = x.shape
    assert bsz == 1 and seq % CHUNK == 0
    x2 = x.reshape(seq, d_model).astype(F32)

    sizes = (GLA_QK, GLA_QK, GLA_V, GLA_GATE_RANK, GLA_V,
             GDN_QK, GDN_QK, GDN_V, GDN_HEADS, GDN_HEADS, GDN_V)
    offs = np.concatenate([[0], np.cumsum(sizes)])
    assert w_in.shape == (d_model, offs[-1])
    w_in_t = w_in.astype(F32).T
    row = lambda i: w_in_t[offs[i]:offs[i + 1]]
    n_small = GLA_GATE_RANK + 2 * GDN_HEADS
    w_small_t = jnp.concatenate([row(3), row(8), row(9)], axis=0)
    w_small_t = jnp.pad(w_small_t, ((0, LANES - n_small), (0, 0)))
    w_smt = _hi_lo(w_small_t).reshape(2 * LANES, d_model)

    tm1 = _pick(seq, (1024, 512, 256, 128, 64))
    p_big, p_sm, p_smt, (w_out_b, w_gate_b, w_up_b, w_down_b) = _inproj(
        x2, attn_norm_w.reshape(1, -1).astype(F32), _wprep(w_in_t, 512), w_smt,
        gdn_conv_w.astype(F32), [w.astype(F32) for w in (w_out, w_gate, w_up, w_down)], tm1, 512)

    cps = 2 if seq % (2 * CHUNK) == 0 else 1

    w2_pad = jnp.pad(gla_gate_w2.astype(F32), ((0, LANES - GLA_GATE_RANK), (0, 0)))
    o_a = _gla(p_big, p_sm, _hi_lo(w2_pad), gla_gate_b.reshape(1, -1).astype(F32),
               gla_norm_w.reshape(1, -1).astype(F32), _diag_selector(), cps)

    a_log_pad = jnp.zeros((LANES,), F32).at[COL_A:COL_A + GDN_HEADS].set(gdn_a_log.astype(F32))
    dt_pad = jnp.zeros((LANES,), F32).at[COL_A:COL_A + GDN_HEADS].set(gdn_dt_bias.astype(F32))
    a_row = jnp.stack([a_log_pad, dt_pad])
    o_b = _gdn(p_big, p_sm, p_smt, a_row, a_row.T,
               gdn_norm_w.reshape(1, -1).astype(F32), 4 if seq % (4 * CHUNK) == 0 else cps)

    tm4 = _pick(seq, (512, 256, 128, 64))
    x1 = _outproj(x2, o_a, o_b, w_out_b, tm4)

    y = _ffn(x1, ffn_norm_w.reshape(1, -1).astype(F32), w_gate_b, w_up_b, w_down_b,
             final_norm_w.reshape(1, -1).astype(F32), tm4, 512)
    return y.reshape(bsz, seq, d_model).astype(x.dtype)
```

```python
import functools

import numpy as np
import jax
import jax.numpy as jnp
from jax import lax
from jax.experimental import pallas as pl
from jax.experimental.pallas import tpu as pltpu

F32 = jnp.float32
BF16 = jnp.bfloat16

EPS = 1e-6
CHUNK = 64

GLA_HEADS = 4
GLA_DK = 128
GLA_DV = 256
GLA_GATE_RANK = 16
GLA_GATE_TAU = 16.0
GLA_QK = GLA_HEADS * GLA_DK
GLA_V = GLA_HEADS * GLA_DV

GDN_HEADS = 8
GDN_DK = 128
GDN_DV = 128
GDN_CONV = 4
GDN_QK = GDN_HEADS * GDN_DK
GDN_V = GDN_HEADS * GDN_DV

LANES = 128
SUB_BLOCK = 16
VMEM_LIMIT = 48 * 1024 * 1024
N_BIG = 2 * GLA_QK + 2 * GLA_V + 2 * GDN_QK + 2 * GDN_V

COL_LR = 0
COL_A = GLA_GATE_RANK
COL_B = GLA_GATE_RANK + GDN_HEADS


def _dot(a, b):
    return jnp.dot(a, b, preferred_element_type=F32)


def _dot_nt(a, b):
    return lax.dot_general(a, b, (((1,), (1,)), ((), ())), preferred_element_type=F32)


def _dot_tn(a, b):
    return lax.dot_general(a, b, (((0,), (0,)), ((), ())), preferred_element_type=F32)


def _split3(a):
    hi = a.astype(BF16)
    r1 = a - hi.astype(F32)
    mid = r1.astype(BF16)
    lo = (r1 - mid.astype(F32)).astype(BF16)
    return hi, mid, lo


def _dot_exact_rhs(a, b_exact):
    hi, mid, lo = _split3(a)
    return _dot(hi, b_exact) + _dot(mid, b_exact) + _dot(lo, b_exact)


def _dot_exact_lhs(a_exact, b):
    hi, mid, lo = _split3(b)
    return _dot(a_exact, hi) + _dot(a_exact, mid) + _dot(a_exact, lo)


def _sigmoid(x):
    return 1.0 / (1.0 + jnp.exp(-x))


def _silu(x):
    return x * _sigmoid(x)


def _softplus(x):
    return jnp.maximum(x, 0.0) + jnp.log1p(jnp.exp(-jnp.abs(x)))


def _log_sigmoid(x):
    return -_softplus(-x)


def _wprep_kernel(wa_ref, wb_ref, o_ref, *, shift_steps):
    j = pl.program_id(0)
    j1, j2 = shift_steps

    def realigned(delta):
        if delta == 0:
            return wa_ref[...].astype(BF16)
        return jnp.concatenate([wa_ref[pl.ds(delta, wa_ref.shape[0] - delta), :],
                                wb_ref[pl.ds(0, delta), :]], axis=0).astype(BF16)

    @pl.when(j < j1)
    def _():
        o_ref[...] = realigned(0)

    @pl.when((j >= j1) & (j < j2))
    def _():
        o_ref[...] = realigned(GLA_GATE_RANK)

    @pl.when(j >= j2)
    def _():
        o_ref[...] = realigned(GLA_GATE_RANK + 2 * GDN_HEADS)


def _wprep(w_in_t, tn):
    D = w_in_t.shape[1]
    n_small = GLA_GATE_RANK + 2 * GDN_HEADS
    g1 = GLA_QK * 2 + GLA_V
    g2 = g1 + GLA_V + GDN_QK * 2 + GDN_V
    assert g1 % tn == 0 and g2 % tn == 0 and N_BIG % tn == 0 and tn % n_small == 0
    return pl.pallas_call(
        functools.partial(_wprep_kernel, shift_steps=(g1 // tn, g2 // tn)),
        grid=(N_BIG // tn,),
        in_specs=[
            pl.BlockSpec((tn, D), lambda j: (j, 0)),
            pl.BlockSpec((n_small, D), lambda j: ((tn // n_small) * (j + 1), 0)),
        ],
        out_specs=pl.BlockSpec((tn, D), lambda j: (j, 0)),
        out_shape=jax.ShapeDtypeStruct((N_BIG, D), BF16),
        compiler_params=pltpu.CompilerParams(
            dimension_semantics=("arbitrary",),
            vmem_limit_bytes=VMEM_LIMIT),
        name="wprep",
    )(w_in_t, w_in_t)


def _inproj_kernel(x_ref, nw_ref, wbig_ref, wsmt_ref, cw_ref, *rest, n_side, tile_kinds):
    side_in = rest[:n_side]
    pbig_ref, psm_ref, psmt_ref = rest[n_side:n_side + 3]
    side_out = rest[n_side + 3:2 * n_side + 3]
    h_ref, ext_scr, tail_scr = rest[2 * n_side + 3:]
    i = pl.program_id(0)
    j = pl.program_id(1)
    tm = h_ref.shape[0]
    tn = wbig_ref.shape[0]
    piece = min(tm, 256)

    for src, dst in zip(side_in, side_out):
        dst[...] = src[...].astype(BF16)

    @pl.when(j == 0)
    def _():
        x = x_ref[...]
        ms = jnp.mean(x * x, axis=-1, keepdims=True)
        h = x * lax.rsqrt(ms + EPS) * nw_ref[...]
        hb = h.astype(BF16)
        h_ref[...] = hb
        hl = (h - hb.astype(F32)).astype(BF16)
        r = _dot_nt(wsmt_ref[...], hb)
        pt = r[:LANES] + r[LANES:] + _dot_nt(wsmt_ref[pl.ds(0, LANES), :], hl)
        psmt_ref[...] = pt
        psm_ref[...] = pt.T

    def piece_dot(r0):
        return _dot_nt(h_ref[pl.ds(r0, piece), :], wbig_ref[...])

    def steps_of(*kinds):
        pred = None
        for jj, kind in enumerate(tile_kinds):
            if kind in kinds:
                pred = (j == jj) if pred is None else (pred | (j == jj))
        return pred

    @pl.when(steps_of('plain'))
    def _():
        pbig_ref[...] = _dot_nt(h_ref[...], wbig_ref[...]).astype(BF16)

    @pl.when(steps_of('silu'))
    def _():
        for r0 in range(0, tm, piece):
            pbig_ref[pl.ds(r0, piece), :] = _silu(piece_dot(r0)).astype(BF16)

    conv_kinds = ('conv_norm_q', 'conv_norm_k', 'conv')
    first_conv = min(jj for jj, kind in enumerate(tile_kinds) if kind in conv_kinds)

    def conv_steps(normalise):
        jc = j - first_conv
        ext_scr[pl.ds(0, 8), :] = jnp.where(i == 0, 0.0, tail_scr[jc])
        scale = jnp.where(steps_of('conv_norm_q'), GDN_DK ** -0.5, 1.0) if normalise else None
        for r0 in range(0, tm, piece):
            ext_scr[pl.ds(8 + r0, piece), :] = piece_dot(r0)
            acc = cw_ref[GDN_CONV - 1:GDN_CONV, :] * ext_scr[pl.ds(8 + r0, piece), :]
            for t in range(GDN_CONV - 1):
                acc = acc + cw_ref[t:t + 1, :] * ext_scr[pl.ds(8 - (GDN_CONV - 1) + t + r0, piece), :]
            y = _silu(acc)
            if normalise:
                heads = []
                for hh in range(tn // GDN_DK):
                    yh = y[:, hh * GDN_DK:(hh + 1) * GDN_DK]
                    ssq = jnp.sum(yh * yh, axis=-1, keepdims=True)
                    heads.append(yh * (lax.rsqrt(ssq + EPS) * scale))
                y = jnp.concatenate(heads, axis=1)
            pbig_ref[pl.ds(r0, piece), :] = y.astype(BF16)
        tail_scr[jc] = ext_scr[pl.ds(tm, 8), :]

    @pl.when(steps_of('conv_norm_q', 'conv_norm_k'))
    def _():
        conv_steps(True)

    @pl.when(steps_of('conv'))
    def _():
        conv_steps(False)


def _max_split(n_steps, size, quantum):
    for nb in range(n_steps, 0, -1):
        if size % (nb * quantum) == 0:
            return nb
    raise ValueError(f"{size} is not a multiple of {quantum}")


def _inproj(x2, norm_w, w_big, w_smt, conv_w, side_weights, tm, tn):
    L, D = x2.shape
    ni, nj = L // tm, N_BIG // tn
    widths = (('plain', 2 * GLA_QK + GLA_V), ('silu', GLA_V), ('conv_norm_q', GDN_QK),
              ('conv_norm_k', GDN_QK), ('conv', GDN_V), ('silu', GDN_V))
    assert all(width % tn == 0 for _, width in widths) and tn % GDN_DK == 0
    tile_kinds = tuple(kind for kind, width in widths for _ in range(width // tn))
    conv_tiles = [jj for jj, kind in enumerate(tile_kinds) if kind.startswith('conv')]
    assert conv_tiles == list(range(conv_tiles[0], conv_tiles[-1] + 1))
    side_specs, side_shapes = [], []
    for w in side_weights:
        rb = _max_split(ni, w.shape[0], 16)
        cb = _max_split(nj, w.shape[1], LANES)
        imap = functools.partial(
            lambda i, j, rb, cb: (jnp.minimum(i, rb - 1), jnp.minimum(j, cb - 1)), rb=rb, cb=cb)
        side_specs.append(pl.BlockSpec((w.shape[0] // rb, w.shape[1] // cb), imap))
        side_shapes.append(jax.ShapeDtypeStruct(w.shape, BF16))
    outs = pl.pallas_call(
        functools.partial(_inproj_kernel, n_side=len(side_weights), tile_kinds=tile_kinds),
        grid=(ni, nj),
        in_specs=[
            pl.BlockSpec((tm, D), lambda i, j: (i, 0)),
            pl.BlockSpec((1, D), lambda i, j: (0, 0)),
            pl.BlockSpec((tn, D), lambda i, j: (j, 0)),
            pl.BlockSpec((2 * LANES, D), lambda i, j: (0, 0)),
            pl.BlockSpec((GDN_CONV, tn),
                         lambda i, j: (0, jnp.clip(j - conv_tiles[0], 0, len(conv_tiles) - 1))),
        ] + side_specs,
        out_specs=[
            pl.BlockSpec((tm, tn), lambda i, j: (i, j)),
            pl.BlockSpec((tm, LANES), lambda i, j: (i, 0)),
            pl.BlockSpec((LANES, tm), lambda i, j: (0, i)),
        ] + side_specs,
        out_shape=[
            jax.ShapeDtypeStruct((L, N_BIG), BF16),
            jax.ShapeDtypeStruct((L, LANES), F32),
            jax.ShapeDtypeStruct((LANES, L), F32),
        ] + side_shapes,
        scratch_shapes=[pltpu.VMEM((tm, D), BF16),
                        pltpu.VMEM((tm + 8, tn), F32),
                        pltpu.VMEM((len(conv_tiles), 8, tn), F32)],
        compiler_params=pltpu.CompilerParams(
            dimension_semantics=("arbitrary", "arbitrary"),
            vmem_limit_bytes=VMEM_LIMIT),
        name="inproj",
    )(x2, norm_w, w_big, w_smt, conv_w, *side_weights)
    return outs[0], outs[1], outs[2], outs[3:]


def _ones_where(mask):
    return jnp.where(mask, 1.0, 0.0).astype(BF16)


def _chunk_tril(rows, upper=False):
    ti = lax.broadcasted_iota(jnp.int32, (rows, rows), 0)
    si = lax.broadcasted_iota(jnp.int32, (rows, rows), 1)
    same = (ti >> 6) == (si >> 6)
    return same & ((ti <= si) if upper else (ti >= si))


def _gla_chunk(q_ref, k_ref, v_ref, gate_ref, nw_ref, sel_ref, o_ref, st_ref,
               c_scr, k_scr, r0):
    C = CHUNK
    rows = pl.ds(r0, C)
    row = lax.broadcasted_iota(jnp.int32, (C, GLA_DK), 0)
    ti = lax.broadcasted_iota(jnp.int32, (C, C), 0)
    si = lax.broadcasted_iota(jnp.int32, (C, C), 1)
    in_hi32 = row >= 32
    right16 = (row & 31) >= 16
    row_lt32 = row < 32
    same_pair = (ti >> 5) == (si >> 5)
    same_blk = (ti >> 4) == (si >> 4)
    trow = lax.broadcasted_iota(jnp.int32, (SUB_BLOCK, GLA_DK), 0)

    for h in range(GLA_HEADS):
        ks = slice(h * GLA_DK, (h + 1) * GLA_DK)
        vs = slice(h * GLA_DV, (h + 1) * GLA_DV)
        q = q_ref[rows, ks].astype(F32) * (GLA_DK ** -0.5)
        k = k_scr[rows, ks]
        v = v_ref[rows, vs]
        c = c_scr[rows, ks]

        ref = c_scr[pl.ds(r0 + 31, 1), ks]
        e = jnp.exp(jnp.where(in_hi32, c - ref, ref - c))
        qt = jnp.where(in_hi32, q * e, 0.0).astype(BF16)
        kt = jnp.where(in_hi32, 0.0, k * e).astype(BF16)
        a = _dot_nt(qt, kt)
        ref = jnp.where(row_lt32, c_scr[pl.ds(r0 + 15, 1), ks], c_scr[pl.ds(r0 + 47, 1), ks])
        e = jnp.exp(jnp.where(right16, c - ref, ref - c))
        qt = jnp.where(right16, q * e, 0.0).astype(BF16)
        kt = jnp.where(right16, 0.0, k * e).astype(BF16)
        a = a + jnp.where(same_pair, _dot_nt(qt, kt), 0.0)
        blocks = []
        for b in range(C // SUB_BLOCK):
            b0 = b * SUB_BLOCK
            cb = c[b0:b0 + SUB_BLOCK]
            qb = q[b0:b0 + SUB_BLOCK]
            parts = []
            for s in range(SUB_BLOCK):
                c_s = c_scr[pl.ds(r0 + b0 + s, 1), ks]
                k_s = k_scr[pl.ds(r0 + b0 + s, 1), ks]
                e = jnp.exp(jnp.where(trow >= s, cb - c_s, -jnp.inf))
                parts.append(((qb * e) * k_s).astype(BF16))
            blocks.append(jnp.concatenate(parts, axis=1))
        p2 = jnp.concatenate(blocks, axis=0)
        a = a + jnp.where(same_blk, _dot(p2, sel_ref[...]), 0.0)

        st = st_ref[h]
        c_last = c_scr[pl.ds(r0 + C - 1, 1), ks]
        qd = (q * jnp.exp(c)).astype(BF16)
        o = _dot(a.astype(BF16), v) + _dot_nt(qd, st.astype(BF16))
        kd = (k * jnp.exp(c_last - c)).astype(BF16)
        st_ref[h] = jnp.exp(c_last) * st + _dot_tn(v, kd)

        ms = jnp.mean(o * o, axis=-1, keepdims=True)
        on = o * lax.rsqrt(ms + EPS) * nw_ref[...]
        o_ref[rows, vs] = (on * gate_ref[rows, vs].astype(F32)).astype(BF16)


def _gla_kernel(q_ref, k_ref, v_ref, gate_ref, psm_ref, w2_ref, gb_ref, nw_ref,
                sel_ref, wsrc_ref, o_ref, wdst_ref, st_ref, c_scr, k_scr, *, chunks_per_step):
    @pl.when(pl.program_id(0) == 0)
    def _():
        st_ref[...] = jnp.zeros_like(st_ref)

    wdst_ref[...] = wsrc_ref[...].astype(BF16)

    a_hi, a_mid, _ = _split3(psm_ref[...])
    z = (_dot(a_hi, w2_ref[0]) + _dot(a_mid, w2_ref[0]) + _dot(a_hi, w2_ref[1])
         + gb_ref[...])
    log_a = _log_sigmoid(z) * (1.0 / GLA_GATE_TAU)
    tril = _ones_where(_chunk_tril(CHUNK * chunks_per_step))
    c_scr[...] = _dot_exact_lhs(tril, log_a)
    k_scr[...] = k_ref[...].astype(F32)

    for ci in range(chunks_per_step):
        _gla_chunk(q_ref, k_ref, v_ref, gate_ref, nw_ref, sel_ref, o_ref, st_ref,
                   c_scr, k_scr, ci * CHUNK)


def _row_cast_spec(n_steps, w):
    nb = _max_split(n_steps, w.shape[0], 16)
    return pl.BlockSpec((w.shape[0] // nb, w.shape[1]), lambda n: (jnp.minimum(n, nb - 1), 0))


def _gla(p_big, p_sm, w2, gate_b, norm_w, sel, side_w, chunks_per_step):
    L = p_big.shape[0]
    R = CHUNK * chunks_per_step
    side_spec = _row_cast_spec(L // R, side_w)
    kern = functools.partial(_gla_kernel, chunks_per_step=chunks_per_step)
    return pl.pallas_call(
        kern,
        grid=(L // R,),
        in_specs=[
            pl.BlockSpec((R, GLA_QK), lambda n: (n, 0)),
            pl.BlockSpec((R, GLA_QK), lambda n: (n, 1)),
            pl.BlockSpec((R, GLA_V), lambda n: (n, 1)),
            pl.BlockSpec((R, GLA_V), lambda n: (n, 2)),
            pl.BlockSpec((R, LANES), lambda n: (n, 0)),
            pl.BlockSpec((2, LANES, GLA_QK), lambda n: (0, 0, 0)),
            pl.BlockSpec((1, GLA_QK), lambda n: (0, 0)),
            pl.BlockSpec((1, GLA_DV), lambda n: (0, 0)),
            pl.BlockSpec((SUB_BLOCK * GLA_DK, CHUNK), lambda n: (0, 0)),
            side_spec,
        ],
        out_specs=[pl.BlockSpec((R, GLA_V), lambda n: (n, 0)), side_spec],
        out_shape=[jax.ShapeDtypeStruct((L, GLA_V), BF16),
                   jax.ShapeDtypeStruct(side_w.shape, BF16)],
        scratch_shapes=[
            pltpu.VMEM((GLA_HEADS, GLA_DV, GLA_DK), F32),
            pltpu.VMEM((R, GLA_QK), F32),
            pltpu.VMEM((R, GLA_QK), F32),
        ],
        compiler_params=pltpu.CompilerParams(
            dimension_semantics=("arbitrary",),
            vmem_limit_bytes=VMEM_LIMIT),
        name="gla",
    )(p_big, p_big, p_big, p_big, p_sm, w2, gate_b, norm_w, sel, side_w)


def _gdn_chunks(q_ref, k_ref, v_ref, z_ref, nw_ref, o_ref, s_ref, cum_col, cum_row, beta_col,
                chunks_per_step):
    C = CHUNK
    ti = lax.broadcasted_iota(jnp.int32, (C, 2 * C), 0)
    li = lax.broadcasted_iota(jnp.int32, (C, 2 * C), 1)
    si = li & (C - 1)
    right = li >= C
    incl = ti >= si
    strict = ti > si
    eye_right = jnp.where((ti == si) & right, 1.0, 0.0)
    tasks = [(ci, h) for ci in range(chunks_per_step) for h in range(GDN_HEADS)]

    wmat, attn, rhs, wq, kd, g_last = {}, {}, {}, {}, {}, {}
    for t in tasks:
        ci, h = t
        r0 = ci * C
        hs = slice(h * GDN_DK, (h + 1) * GDN_DK)
        qnb = q_ref[pl.ds(r0, C), hs]
        knb = k_ref[pl.ds(r0, C), hs]
        kn = knb.astype(F32)
        cv = v_ref[pl.ds(r0, C), hs].astype(F32)
        cc = cum_col[r0:r0 + C, COL_A + h:COL_A + h + 1]
        cr = cum_row[COL_A + h:COL_A + h + 1, 2 * r0:2 * r0 + 2 * C]
        beta = beta_col[r0:r0 + C, COL_B + h:COL_B + h + 1]
        gamma = jnp.exp(jnp.where(incl, cc - cr, -jnp.inf))
        kb = kn * beta
        x = _dot_nt(jnp.concatenate([qnb, kb.astype(BF16)], axis=0),
                    jnp.concatenate([knb, knb], axis=0))
        attn[t] = (x[:C, :C] * gamma[:, :C]).astype(BF16)
        wmat[t] = jnp.where(strict, x[C:] * gamma, 0.0)
        e_c = jnp.exp(cc)
        c_last = cc[C - 1:C, :]
        rhs1 = jnp.concatenate([cv * beta, kb * e_c], axis=1).astype(BF16)
        rhs[t] = jnp.concatenate([rhs1, rhs1], axis=0)
        wq[t] = (qnb.astype(F32) * e_c).astype(BF16)
        kd[t] = (kn * jnp.exp(c_last - cc)).astype(BF16)
        g_last[t] = jnp.exp(c_last)

    for t in tasks:
        n2b = wmat[t].astype(BF16)
        wmat[t] = jnp.where(right, eye_right - wmat[t], _dot(n2b[:, :C], n2b))
    for _ in range(5):
        for t in tasks:
            wb = wmat[t].astype(BF16)
            wmat[t] = _dot(wb[:, :C], wb) + jnp.where(right, wmat[t], 0.0)
    sol = {t: _dot(wmat[t].astype(BF16), rhs[t]) for t in tasks}

    for ci in range(chunks_per_step):
        rows = pl.ds(ci * C, C)
        heads = [(ci, h) for h in range(GDN_HEADS)]
        s_old = {t: s_ref[t[1]] for t in heads}
        ws = {t: _dot(jnp.concatenate([sol[t][:, GDN_DV:].astype(BF16), wq[t]], axis=0),
                      s_old[t].astype(BF16)) for t in heads}
        vnb = {t: (sol[t][:, :GDN_DV] - ws[t][:C]).astype(BF16) for t in heads}
        for t in heads:
            h = t[1]
            hs = slice(h * GDN_DK, (h + 1) * GDN_DK)
            o = ws[t][C:] + _dot(attn[t], vnb[t])
            s_ref[h] = g_last[t] * s_old[t] + _dot_tn(kd[t], vnb[t])
            ms = jnp.mean(o * o, axis=-1, keepdims=True)
            on = o * lax.rsqrt(ms + EPS) * nw_ref[...]
            o_ref[rows, hs] = (on * z_ref[rows, hs].astype(F32)).astype(BF16)


def _gdn_kernel(q_ref, k_ref, v_ref, z_ref, psm_ref, psmt_ref, arow_ref,
                acol_ref, nw_ref, *rest, chunks_per_step, n_side):
    side_in = rest[:n_side]
    o_ref = rest[n_side]
    side_out = rest[n_side + 1:2 * n_side + 1]
    s_ref = rest[2 * n_side + 1]
    R = CHUNK * chunks_per_step

    @pl.when(pl.program_id(0) == 0)
    def _():
        s_ref[...] = jnp.zeros_like(s_ref)

    for src, dst in zip(side_in, side_out):
        dst[...] = src[...].astype(BF16)

    psm = psm_ref[...]
    g_col = -jnp.exp(arow_ref[0:1, :]) * _softplus(psm + arow_ref[1:2, :])
    beta_col = _sigmoid(psm)
    cum_col = _dot_exact_lhs(_ones_where(_chunk_tril(R)), g_col)
    g_row = -jnp.exp(acol_ref[:, 0:1]) * _softplus(psmt_ref[...] + acol_ref[:, 1:2])
    ji = lax.broadcasted_iota(jnp.int32, (R, 2 * R), 0)
    li = lax.broadcasted_iota(jnp.int32, (R, 2 * R), 1)
    dup = ((ji >> 6) == (li >> 7)) & ((ji & (CHUNK - 1)) <= (li & (CHUNK - 1)))
    cum_row = _dot_exact_rhs(g_row, _ones_where(dup))

    _gdn_chunks(q_ref, k_ref, v_ref, z_ref, nw_ref, o_ref, s_ref, cum_col, cum_row, beta_col,
                chunks_per_step)


def _gdn(p_big, p_sm, p_smt, a_row, a_col, norm_w, side_weights, chunks_per_step):
    L = p_big.shape[0]
    R = CHUNK * chunks_per_step
    base = (GLA_QK * 2 + GLA_V * 2) // GDN_QK
    kern = functools.partial(_gdn_kernel, chunks_per_step=chunks_per_step,
                             n_side=len(side_weights))
    side_specs = [_row_cast_spec(L // R, w) for w in side_weights]
    outs = pl.pallas_call(
        kern,
        grid=(L // R,),
        in_specs=[
            pl.BlockSpec((R, GDN_QK), lambda n: (n, base)),
            pl.BlockSpec((R, GDN_QK), lambda n: (n, base + 1)),
            pl.BlockSpec((R, GDN_V), lambda n: (n, base + 2)),
            pl.BlockSpec((R, GDN_V), lambda n: (n, base + 3)),
            pl.BlockSpec((R, LANES), lambda n: (n, 0)),
            pl.BlockSpec((LANES, R), lambda n: (0, n)),
            pl.BlockSpec((2, LANES), lambda n: (0, 0)),
            pl.BlockSpec((LANES, 2), lambda n: (0, 0)),
            pl.BlockSpec((1, GDN_DV), lambda n: (0, 0)),
        ] + side_specs,
        out_specs=[pl.BlockSpec((R, GDN_V), lambda n: (n, 0))] + side_specs,
        out_shape=[jax.ShapeDtypeStruct((L, GDN_V), BF16)]
        + [jax.ShapeDtypeStruct(w.shape, BF16) for w in side_weights],
        scratch_shapes=[pltpu.VMEM((GDN_HEADS, GDN_DK, GDN_DV), F32)],
        compiler_params=pltpu.CompilerParams(
            dimension_semantics=("arbitrary",),
            vmem_limit_bytes=VMEM_LIMIT),
        name="gdn",
    )(p_big, p_big, p_big, p_big, p_sm, p_smt, a_row, a_col, norm_w, *side_weights)
    return outs[0], outs[1:]


def _outproj_kernel(x_ref, oa_ref, ob_ref, wa_ref, wb_ref, y_ref):
    y_ref[...] = (x_ref[...] + _dot(oa_ref[...], wa_ref[...])
                  + _dot(ob_ref[...], wb_ref[...]))


def _outproj(x2, o_a, o_b, w_out_b, tm):
    L, D = x2.shape
    return pl.pallas_call(
        _outproj_kernel,
        grid=(L // tm,),
        in_specs=[
            pl.BlockSpec((tm, D), lambda i: (i, 0)),
            pl.BlockSpec((tm, GLA_V), lambda i: (i, 0)),
            pl.BlockSpec((tm, GDN_V), lambda i: (i, 0)),
            pl.BlockSpec((GLA_V, D), lambda i: (0, 0)),
            pl.BlockSpec((GDN_V, D), lambda i: (GLA_V // GDN_V, 0)),
        ],
        out_specs=pl.BlockSpec((tm, D), lambda i: (i, 0)),
        out_shape=jax.ShapeDtypeStruct((L, D), F32),
        compiler_params=pltpu.CompilerParams(
            dimension_semantics=("arbitrary",),
            vmem_limit_bytes=VMEM_LIMIT),
        name="outproj",
    )(x2, o_a, o_b, w_out_b, w_out_b)


def _ffn_kernel(x_ref, fnw_ref, wg_ref, wu_ref, wd_ref, onw_ref, y_ref, hf_ref, acc_ref):
    f = pl.program_id(1)

    @pl.when(f == 0)
    def _():
        x = x_ref[...]
        ms = jnp.mean(x * x, axis=-1, keepdims=True)
        hf_ref[...] = (x * lax.rsqrt(ms + EPS) * fnw_ref[...]).astype(BF16)
        acc_ref[...] = jnp.zeros_like(acc_ref)

    hf = hf_ref[...]
    g = _dot(hf, wg_ref[...])
    u = _dot(hf, wu_ref[...])
    acc_ref[...] += _dot((_silu(g) * u).astype(BF16), wd_ref[...])

    @pl.when(f == pl.num_programs(1) - 1)
    def _():
        r = x_ref[...] + acc_ref[...]
        ms = jnp.mean(r * r, axis=-1, keepdims=True)
        y_ref[...] = r * lax.rsqrt(ms + EPS) * onw_ref[...]


def _ffn(x1, ffn_norm_w, wg, wu, wd, final_norm_w, tm, tf):
    L, D = x1.shape
    F = wg.shape[1]
    return pl.pallas_call(
        _ffn_kernel,
        grid=(L // tm, F // tf),
        in_specs=[
            pl.BlockSpec((tm, D), lambda i, f: (i, 0)),
            pl.BlockSpec((1, D), lambda i, f: (0, 0)),
            pl.BlockSpec((D, tf), lambda i, f: (0, f)),
            pl.BlockSpec((D, tf), lambda i, f: (0, f)),
            pl.BlockSpec((tf, D), lambda i, f: (f, 0)),
            pl.BlockSpec((1, D), lambda i, f: (0, 0)),
        ],
        out_specs=pl.BlockSpec((tm, D), lambda i, f: (i, 0)),
        out_shape=jax.ShapeDtypeStruct((L, D), F32),
        scratch_shapes=[pltpu.VMEM((tm, D), BF16), pltpu.VMEM((tm, D), F32)],
        compiler_params=pltpu.CompilerParams(
            dimension_semantics=("arbitrary", "arbitrary"),
            vmem_limit_bytes=VMEM_LIMIT),
        name="ffn",
    )(x1, ffn_norm_w, wg, wu, wd, final_norm_w)


def _hi_lo(w):
    hi = w.astype(BF16)
    lo = (w - hi.astype(F32)).astype(BF16)
    return jnp.stack([hi, lo])


def _diag_selector():
    s_of_row = np.arange(SUB_BLOCK * GLA_DK) // GLA_DK
    n_mod = np.arange(CHUNK) % SUB_BLOCK
    return jnp.asarray(s_of_row[:, None] == n_mod[None, :], dtype=BF16)


def _pick(n, candidates):
    for c in candidates:
        if n % c == 0:
            return c
    raise ValueError(f"no tile in {candidates} divides {n}")


def kernel(x, attn_norm_w, w_in, gla_gate_w2, gla_gate_b, gla_norm_w, gdn_conv_w,
           gdn_a_log, gdn_dt_bias, gdn_norm_w, w_out, ffn_norm_w, w_gate, w_up,
           w_down, final_norm_w):
    bsz, seq, d_model = x.shape
    assert bsz == 1 and seq % CHUNK == 0
    x2 = x.reshape(seq, d_model).astype(F32)

    sizes = (GLA_QK, GLA_QK, GLA_V, GLA_GATE_RANK, GLA_V,
             GDN_QK, GDN_QK, GDN_V, GDN_HEADS, GDN_HEADS, GDN_V)
    offs = np.concatenate([[0], np.cumsum(sizes)])
    assert w_in.shape == (d_model, offs[-1])
    w_in_t = w_in.astype(F32).T
    row = lambda i: w_in_t[offs[i]:offs[i + 1]]
    n_small = GLA_GATE_RANK + 2 * GDN_HEADS
    w_small_t = jnp.concatenate([row(3), row(8), row(9)], axis=0)
    w_small_t = jnp.pad(w_small_t, ((0, LANES - n_small), (0, 0)))
    w_smt = _hi_lo(w_small_t).reshape(2 * LANES, d_model)

    tm1 = _pick(seq, (1024, 512, 256, 128, 64))
    p_big, p_sm, p_smt, _ = _inproj(
        x2, attn_norm_w.reshape(1, -1).astype(F32), _wprep(w_in_t, 512), w_smt,
        gdn_conv_w.astype(F32), [], tm1, 512)

    cps = 2 if seq % (2 * CHUNK) == 0 else 1

    w2_pad = jnp.pad(gla_gate_w2.astype(F32), ((0, LANES - GLA_GATE_RANK), (0, 0)))
    o_a, w_out_b = _gla(p_big, p_sm, _hi_lo(w2_pad), gla_gate_b.reshape(1, -1).astype(F32),
                        gla_norm_w.reshape(1, -1).astype(F32), _diag_selector(),
                        w_out.astype(F32), cps)

    a_log_pad = jnp.zeros((LANES,), F32).at[COL_A:COL_A + GDN_HEADS].set(gdn_a_log.astype(F32))
    dt_pad = jnp.zeros((LANES,), F32).at[COL_A:COL_A + GDN_HEADS].set(gdn_dt_bias.astype(F32))
    a_row = jnp.stack([a_log_pad, dt_pad])
    o_b, (w_gate_b, w_up_b, w_down_b) = _gdn(
        p_big, p_sm, p_smt, a_row, a_row.T, gdn_norm_w.reshape(1, -1).astype(F32),
        [w.astype(F32) for w in (w_gate, w_up, w_down)], 4 if seq % (4 * CHUNK) == 0 else cps)

    tm4 = _pick(seq, (512, 256, 128, 64))
    x1 = _outproj(x2, o_a, o_b, w_out_b, tm4)

    y = _ffn(x1, ffn_norm_w.reshape(1, -1).astype(F32), w_gate_b, w_up_b, w_down_b,
             final_norm_w.reshape(1, -1).astype(F32), tm4, 512)
    return y.reshape(bsz, seq, d_model).astype(x.dtype)
```

```python
import functools

import numpy as np
import jax
import jax.numpy as jnp
from jax import lax
from jax.experimental import pallas as pl
from jax.experimental.pallas import tpu as pltpu

F32 = jnp.float32
BF16 = jnp.bfloat16

EPS = 1e-6
CHUNK = 64

GLA_HEADS = 4
GLA_DK = 128
GLA_DV = 256
GLA_GATE_RANK = 16
GLA_GATE_TAU = 16.0
GLA_QK = GLA_HEADS * GLA_DK
GLA_V = GLA_HEADS * GLA_DV

GDN_HEADS = 8
GDN_DK = 128
GDN_DV = 128
GDN_CONV = 4
GDN_QK = GDN_HEADS * GDN_DK
GDN_V = GDN_HEADS * GDN_DV

LANES = 128
SUB_BLOCK = 16
VMEM_LIMIT = 48 * 1024 * 1024
N_BIG = 2 * GLA_QK + 2 * GLA_V + 2 * GDN_QK + 2 * GDN_V

COL_LR = 0
COL_A = GLA_GATE_RANK
COL_B = GLA_GATE_RANK + GDN_HEADS


def _dot(a, b):
    return jnp.dot(a, b, preferred_element_type=F32)


def _dot_nt(a, b):
    return lax.dot_general(a, b, (((1,), (1,)), ((), ())), preferred_element_type=F32)


def _dot_tn(a, b):
    return lax.dot_general(a, b, (((0,), (0,)), ((), ())), preferred_element_type=F32)


def _split3(a):
    hi = a.astype(BF16)
    r1 = a - hi.astype(F32)
    mid = r1.astype(BF16)
    lo = (r1 - mid.astype(F32)).astype(BF16)
    return hi, mid, lo


def _dot_exact_rhs(a, b_exact):
    hi, mid, lo = _split3(a)
    return _dot(hi, b_exact) + _dot(mid, b_exact) + _dot(lo, b_exact)


def _dot_exact_lhs(a_exact, b):
    hi, mid, lo = _split3(b)
    return _dot(a_exact, hi) + _dot(a_exact, mid) + _dot(a_exact, lo)


def _sigmoid(x):
    return 1.0 / (1.0 + jnp.exp(-x))


def _silu(x):
    return x * _sigmoid(x)


def _softplus(x):
    return jnp.maximum(x, 0.0) + jnp.log1p(jnp.exp(-jnp.abs(x)))


def _log_sigmoid(x):
    return -_softplus(-x)


def _wprep_kernel(wa_ref, wb_ref, o_ref, *, shift_steps):
    j = pl.program_id(0)
    j1, j2 = shift_steps

    def realigned(delta):
        if delta == 0:
            return wa_ref[...].astype(BF16)
        return jnp.concatenate([wa_ref[pl.ds(delta, wa_ref.shape[0] - delta), :],
                                wb_ref[pl.ds(0, delta), :]], axis=0).astype(BF16)

    @pl.when(j < j1)
    def _():
        o_ref[...] = realigned(0)

    @pl.when((j >= j1) & (j < j2))
    def _():
        o_ref[...] = realigned(GLA_GATE_RANK)

    @pl.when(j >= j2)
    def _():
        o_ref[...] = realigned(GLA_GATE_RANK + 2 * GDN_HEADS)


def _wprep(w_in_t, tn):
    D = w_in_t.shape[1]
    n_small = GLA_GATE_RANK + 2 * GDN_HEADS
    g1 = GLA_QK * 2 + GLA_V
    g2 = g1 + GLA_V + GDN_QK * 2 + GDN_V
    assert g1 % tn == 0 and g2 % tn == 0 and N_BIG % tn == 0 and tn % n_small == 0
    return pl.pallas_call(
        functools.partial(_wprep_kernel, shift_steps=(g1 // tn, g2 // tn)),
        grid=(N_BIG // tn,),
        in_specs=[
            pl.BlockSpec((tn, D), lambda j: (j, 0)),
            pl.BlockSpec((n_small, D), lambda j: ((tn // n_small) * (j + 1), 0)),
        ],
        out_specs=pl.BlockSpec((tn, D), lambda j: (j, 0)),
        out_shape=jax.ShapeDtypeStruct((N_BIG, D), BF16),
        compiler_params=pltpu.CompilerParams(
            dimension_semantics=("arbitrary",),
            vmem_limit_bytes=VMEM_LIMIT),
        name="wprep",
    )(w_in_t, w_in_t)


def _inproj_kernel(x_ref, nw_ref, wbig_ref, wsmt_ref, cw_ref, *rest, n_side, tile_kinds):
    side_in = rest[:n_side]
    pbig_ref, psm_ref, psmt_ref = rest[n_side:n_side + 3]
    side_out = rest[n_side + 3:2 * n_side + 3]
    h_ref, ext_scr, tail_scr = rest[2 * n_side + 3:]
    i = pl.program_id(0)
    j = pl.program_id(1)
    tm = h_ref.shape[0]
    tn = wbig_ref.shape[0]
    piece = min(tm, 256)

    for src, dst in zip(side_in, side_out):
        dst[...] = src[...].astype(BF16)

    @pl.when(j == 0)
    def _():
        x = x_ref[...]
        ms = jnp.mean(x * x, axis=-1, keepdims=True)
        h = x * lax.rsqrt(ms + EPS) * nw_ref[...]
        hb = h.astype(BF16)
        h_ref[...] = hb
        hl = (h - hb.astype(F32)).astype(BF16)
        r = _dot_nt(wsmt_ref[...], hb)
        pt = r[:LANES] + r[LANES:] + _dot_nt(wsmt_ref[pl.ds(0, LANES), :], hl)
        psmt_ref[...] = pt
        psm_ref[...] = pt.T

    def piece_dot(r0):
        return _dot_nt(h_ref[pl.ds(r0, piece), :], wbig_ref[...])

    def steps_of(*kinds):
        pred = None
        for jj, kind in enumerate(tile_kinds):
            if kind in kinds:
                pred = (j == jj) if pred is None else (pred | (j == jj))
        return pred

    @pl.when(steps_of('plain'))
    def _():
        pbig_ref[...] = _dot_nt(h_ref[...], wbig_ref[...]).astype(BF16)

    @pl.when(steps_of('silu'))
    def _():
        for r0 in range(0, tm, piece):
            pbig_ref[pl.ds(r0, piece), :] = _silu(piece_dot(r0)).astype(BF16)

    conv_kinds = ('conv_norm_q', 'conv_norm_k', 'conv')
    first_conv = min(jj for jj, kind in enumerate(tile_kinds) if kind in conv_kinds)

    def conv_steps(normalise):
        jc = j - first_conv
        ext_scr[pl.ds(0, 8), :] = jnp.where(i == 0, 0.0, tail_scr[jc])
        scale = jnp.where(steps_of('conv_norm_q'), GDN_DK ** -0.5, 1.0) if normalise else None
        for r0 in range(0, tm, piece):
            ext_scr[pl.ds(8 + r0, piece), :] = piece_dot(r0)
            acc = cw_ref[GDN_CONV - 1:GDN_CONV, :] * ext_scr[pl.ds(8 + r0, piece), :]
            for t in range(GDN_CONV - 1):
                acc = acc + cw_ref[t:t + 1, :] * ext_scr[pl.ds(8 - (GDN_CONV - 1) + t + r0, piece), :]
            y = _silu(acc)
            if normalise:
                heads = []
                for hh in range(tn // GDN_DK):
                    yh = y[:, hh * GDN_DK:(hh + 1) * GDN_DK]
                    ssq = jnp.sum(yh * yh, axis=-1, keepdims=True)
                    heads.append(yh * (lax.rsqrt(ssq + EPS) * scale))
                y = jnp.concatenate(heads, axis=1)
            pbig_ref[pl.ds(r0, piece), :] = y.astype(BF16)
        tail_scr[jc] = ext_scr[pl.ds(tm, 8), :]

    @pl.when(steps_of('conv_norm_q', 'conv_norm_k'))
    def _():
        conv_steps(True)

    @pl.when(steps_of('conv'))
    def _():
        conv_steps(False)


def _max_split(n_steps, size, quantum):
    for nb in range(n_steps, 0, -1):
        if size % (nb * quantum) == 0:
            return nb
    raise ValueError(f"{size} is not a multiple of {quantum}")


def _inproj(x2, norm_w, w_big, w_smt, conv_w, side_weights, tm, tn):
    L, D = x2.shape
    ni, nj = L // tm, N_BIG // tn
    widths = (('plain', 2 * GLA_QK + GLA_V), ('silu', GLA_V), ('conv_norm_q', GDN_QK),
              ('conv_norm_k', GDN_QK), ('conv', GDN_V), ('silu', GDN_V))
    assert all(width % tn == 0 for _, width in widths) and tn % GDN_DK == 0
    tile_kinds = tuple(kind for kind, width in widths for _ in range(width // tn))
    conv_tiles = [jj for jj, kind in enumerate(tile_kinds) if kind.startswith('conv')]
    assert conv_tiles == list(range(conv_tiles[0], conv_tiles[-1] + 1))
    side_specs, side_shapes = [], []
    for w in side_weights:
        rb = _max_split(ni, w.shape[0], 16)
        cb = _max_split(nj, w.shape[1], LANES)
        imap = functools.partial(
            lambda i, j, rb, cb: (jnp.minimum(i, rb - 1), jnp.minimum(j, cb - 1)), rb=rb, cb=cb)
        side_specs.append(pl.BlockSpec((w.shape[0] // rb, w.shape[1] // cb), imap))
        side_shapes.append(jax.ShapeDtypeStruct(w.shape, BF16))
    outs = pl.pallas_call(
        functools.partial(_inproj_kernel, n_side=len(side_weights), tile_kinds=tile_kinds),
        grid=(ni, nj),
        in_specs=[
            pl.BlockSpec((tm, D), lambda i, j: (i, 0)),
            pl.BlockSpec((1, D), lambda i, j: (0, 0)),
            pl.BlockSpec((tn, D), lambda i, j: (j, 0)),
            pl.BlockSpec((2 * LANES, D), lambda i, j: (0, 0)),
            pl.BlockSpec((GDN_CONV, tn),
                         lambda i, j: (0, jnp.clip(j - conv_tiles[0], 0, len(conv_tiles) - 1))),
        ] + side_specs,
        out_specs=[
            pl.BlockSpec((tm, tn), lambda i, j: (i, j)),
            pl.BlockSpec((tm, LANES), lambda i, j: (i, 0)),
            pl.BlockSpec((LANES, tm), lambda i, j: (0, i)),
        ] + side_specs,
        out_shape=[
            jax.ShapeDtypeStruct((L, N_BIG), BF16),
            jax.ShapeDtypeStruct((L, LANES), F32),
            jax.ShapeDtypeStruct((LANES, L), F32),
        ] + side_shapes,
        scratch_shapes=[pltpu.VMEM((tm, D), BF16),
                        pltpu.VMEM((tm + 8, tn), F32),
                        pltpu.VMEM((len(conv_tiles), 8, tn), F32)],
        compiler_params=pltpu.CompilerParams(
            dimension_semantics=("arbitrary", "arbitrary"),
            vmem_limit_bytes=VMEM_LIMIT),
        name="inproj",
    )(x2, norm_w, w_big, w_smt, conv_w, *side_weights)
    return outs[0], outs[1], outs[2], outs[3:]


def _ones_where(mask):
    return jnp.where(mask, 1.0, 0.0).astype(BF16)


def _chunk_tril(rows, upper=False):
    ti = lax.broadcasted_iota(jnp.int32, (rows, rows), 0)
    si = lax.broadcasted_iota(jnp.int32, (rows, rows), 1)
    same = (ti >> 6) == (si >> 6)
    return same & ((ti <= si) if upper else (ti >= si))


def _gla_ref_rows(c, c_scr, r0, ks, blk):
    C = CHUNK
    if 2 * blk >= 8:
        return jnp.concatenate(
            [jnp.broadcast_to(c_scr[pl.ds(r0 + g + blk - 1, 1), ks], (2 * blk, GLA_DK))
             for g in range(0, C, 2 * blk)], axis=0)
    pos = lax.broadcasted_iota(jnp.int32, (C, GLA_DK), 0) & (2 * blk - 1)
    out = c
    for off in range(2 * blk):
        if off != blk - 1:
            shift = (off - (blk - 1)) % C
            out = jnp.where(pos == off, pltpu.roll(c, shift, axis=0), out)
    return out


def _gla_scores(q, k, c, c_scr, r0, ks):
    C = CHUNK
    row = lax.broadcasted_iota(jnp.int32, (C, GLA_DK), 0)
    ti = lax.broadcasted_iota(jnp.int32, (C, C), 0)
    si = lax.broadcasted_iota(jnp.int32, (C, C), 1)
    a = jnp.where(ti == si, jnp.sum(q * k, axis=-1, keepdims=True), 0.0)
    blk = C // 2
    while blk >= 1:
        right = (row & (2 * blk - 1)) >= blk
        d = c - _gla_ref_rows(c, c_scr, r0, ks, blk)
        e = jnp.exp(jnp.where(right, d, -d))
        qt = jnp.where(right, q * e, 0.0).astype(BF16)
        kt = jnp.where(right, 0.0, k * e).astype(BF16)
        part = _dot_nt(qt, kt)
        if 2 * blk < C:
            shift = (2 * blk).bit_length() - 1
            part = jnp.where((ti >> shift) == (si >> shift), part, 0.0)
        a = a + part
        blk //= 2
    return a


def _gla_chunks(q_ref, v_ref, gate_ref, nw_ref, o_ref, st_ref, c_scr, k_scr, chunks_per_step):
    C = CHUNK
    tasks = [(ci, h) for ci in range(chunks_per_step) for h in range(GLA_HEADS)]

    def front(t):
        ci, h = t
        r0 = ci * C
        ks = slice(h * GLA_DK, (h + 1) * GLA_DK)
        q = q_ref[pl.ds(r0, C), ks].astype(F32) * (GLA_DK ** -0.5)
        k = k_scr[pl.ds(r0, C), ks]
        c = c_scr[pl.ds(r0, C), ks]
        a = _gla_scores(q, k, c, c_scr, r0, ks)
        c_last = c_scr[pl.ds(r0 + C - 1, 1), ks]
        qd = (q * jnp.exp(c)).astype(BF16)
        kd = (k * jnp.exp(c_last - c)).astype(BF16)
        return a.astype(BF16), qd, kd, jnp.exp(c_last)

    def back(t, a, qd, kd, g_last):
        ci, h = t
        rows = pl.ds(ci * C, C)
        vs = slice(h * GLA_DV, (h + 1) * GLA_DV)
        v = v_ref[rows, vs]
        st = st_ref[h]
        o = _dot(a, v) + _dot_nt(qd, st.astype(BF16))
        st_ref[h] = g_last * st + _dot_tn(v, kd)
        ms = jnp.mean(o * o, axis=-1, keepdims=True)
        on = o * lax.rsqrt(ms + EPS) * nw_ref[...]
        o_ref[rows, vs] = (on * gate_ref[rows, vs].astype(F32)).astype(BF16)

    lag = 2
    fronts = {}
    for idx, t in enumerate(tasks):
        fronts[t] = front(t)
        if idx >= lag:
            back(tasks[idx - lag], *fronts.pop(tasks[idx - lag]))
    for t in tasks[-lag:]:
        back(t, *fronts.pop(t))


def _gla_kernel(q_ref, k_ref, v_ref, gate_ref, psm_ref, w2_ref, gb_ref, nw_ref,
                wsrc_ref, o_ref, wdst_ref, st_ref, c_scr, k_scr, *, chunks_per_step):
    @pl.when(pl.program_id(0) == 0)
    def _():
        st_ref[...] = jnp.zeros_like(st_ref)

    wdst_ref[...] = wsrc_ref[...].astype(BF16)

    a_hi, a_mid, _ = _split3(psm_ref[...])
    z = (_dot(a_hi, w2_ref[0]) + _dot(a_mid, w2_ref[0]) + _dot(a_hi, w2_ref[1])
         + gb_ref[...])
    log_a = _log_sigmoid(z) * (1.0 / GLA_GATE_TAU)
    tril = _ones_where(_chunk_tril(CHUNK * chunks_per_step))
    c_scr[...] = _dot_exact_lhs(tril, log_a)
    k_scr[...] = k_ref[...].astype(F32)

    _gla_chunks(q_ref, v_ref, gate_ref, nw_ref, o_ref, st_ref, c_scr, k_scr, chunks_per_step)


def _row_cast_spec(n_steps, w):
    nb = _max_split(n_steps, w.shape[0], 16)
    return pl.BlockSpec((w.shape[0] // nb, w.shape[1]), lambda n: (jnp.minimum(n, nb - 1), 0))


def _gla(p_big, p_sm, w2, gate_b, norm_w, side_w, chunks_per_step):
    L = p_big.shape[0]
    R = CHUNK * chunks_per_step
    side_spec = _row_cast_spec(L // R, side_w)
    kern = functools.partial(_gla_kernel, chunks_per_step=chunks_per_step)
    return pl.pallas_call(
        kern,
        grid=(L // R,),
        in_specs=[
            pl.BlockSpec((R, GLA_QK), lambda n: (n, 0)),
            pl.BlockSpec((R, GLA_QK), lambda n: (n, 1)),
            pl.BlockSpec((R, GLA_V), lambda n: (n, 1)),
            pl.BlockSpec((R, GLA_V), lambda n: (n, 2)),
            pl.BlockSpec((R, LANES), lambda n: (n, 0)),
            pl.BlockSpec((2, LANES, GLA_QK), lambda n: (0, 0, 0)),
            pl.BlockSpec((1, GLA_QK), lambda n: (0, 0)),
            pl.BlockSpec((1, GLA_DV), lambda n: (0, 0)),
            side_spec,
        ],
        out_specs=[pl.BlockSpec((R, GLA_V), lambda n: (n, 0)), side_spec],
        out_shape=[jax.ShapeDtypeStruct((L, GLA_V), BF16),
                   jax.ShapeDtypeStruct(side_w.shape, BF16)],
        scratch_shapes=[
            pltpu.VMEM((GLA_HEADS, GLA_DV, GLA_DK), F32),
            pltpu.VMEM((R, GLA_QK), F32),
            pltpu.VMEM((R, GLA_QK), F32),
        ],
        compiler_params=pltpu.CompilerParams(
            dimension_semantics=("arbitrary",),
            vmem_limit_bytes=VMEM_LIMIT),
        name="gla",
    )(p_big, p_big, p_big, p_big, p_sm, w2, gate_b, norm_w, side_w)


def _gdn_chunks(q_ref, k_ref, v_ref, z_ref, nw_ref, o_ref, s_ref, cum_col, cum_row, beta_col,
                chunks_per_step):
    C = CHUNK
    ti = lax.broadcasted_iota(jnp.int32, (C, 2 * C), 0)
    li = lax.broadcasted_iota(jnp.int32, (C, 2 * C), 1)
    si = li & (C - 1)
    right = li >= C
    incl = ti >= si
    strict = ti > si
    eye_right = jnp.where((ti == si) & right, 1.0, 0.0)
    tasks = [(ci, h) for ci in range(chunks_per_step) for h in range(GDN_HEADS)]

    wmat, attn, rhs, wq, kd, g_last = {}, {}, {}, {}, {}, {}
    for t in tasks:
        ci, h = t
        r0 = ci * C
        hs = slice(h * GDN_DK, (h + 1) * GDN_DK)
        qnb = q_ref[pl.ds(r0, C), hs]
        knb = k_ref[pl.ds(r0, C), hs]
        kn = knb.astype(F32)
        cv = v_ref[pl.ds(r0, C), hs].astype(F32)
        cc = cum_col[r0:r0 + C, COL_A + h:COL_A + h + 1]
        cr = cum_row[COL_A + h:COL_A + h + 1, 2 * r0:2 * r0 + 2 * C]
        beta = beta_col[r0:r0 + C, COL_B + h:COL_B + h + 1]
        gamma = jnp.exp(jnp.where(incl, cc - cr, -jnp.inf))
        kb = kn * beta
        x = _dot_nt(jnp.concatenate([qnb, kb.astype(BF16)], axis=0),
                    jnp.concatenate([knb, knb], axis=0))
        attn[t] = (x[:C, :C] * gamma[:, :C]).astype(BF16)
        wmat[t] = jnp.where(strict, x[C:] * gamma, 0.0)
        e_c = jnp.exp(cc)
        c_last = cc[C - 1:C, :]
        rhs1 = jnp.concatenate([cv * beta, kb * e_c], axis=1).astype(BF16)
        rhs[t] = jnp.concatenate([rhs1, rhs1], axis=0)
        wq[t] = (qnb.astype(F32) * e_c).astype(BF16)
        kd[t] = (kn * jnp.exp(c_last - cc)).astype(BF16)
        g_last[t] = jnp.exp(c_last)

    for t in tasks:
        n2b = wmat[t].astype(BF16)
        wmat[t] = jnp.where(right, eye_right - wmat[t], _dot(n2b[:, :C], n2b))
    for _ in range(5):
        for t in tasks:
            wb = wmat[t].astype(BF16)
            wmat[t] = _dot(wb[:, :C], wb) + jnp.where(right, wmat[t], 0.0)
    sol = {t: _dot(wmat[t].astype(BF16), rhs[t]) for t in tasks}

    for ci in range(chunks_per_step):
        rows = pl.ds(ci * C, C)
        heads = [(ci, h) for h in range(GDN_HEADS)]
        s_old = {t: s_ref[t[1]] for t in heads}
        ws = {t: _dot(jnp.concatenate([sol[t][:, GDN_DV:].astype(BF16), wq[t]], axis=0),
                      s_old[t].astype(BF16)) for t in heads}
        vnb = {t: (sol[t][:, :GDN_DV] - ws[t][:C]).astype(BF16) for t in heads}
        for t in heads:
            h = t[1]
            hs = slice(h * GDN_DK, (h + 1) * GDN_DK)
            o = ws[t][C:] + _dot(attn[t], vnb[t])
            s_ref[h] = g_last[t] * s_old[t] + _dot_tn(kd[t], vnb[t])
            ms = jnp.mean(o * o, axis=-1, keepdims=True)
            on = o * lax.rsqrt(ms + EPS) * nw_ref[...]
            o_ref[rows, hs] = (on * z_ref[rows, hs].astype(F32)).astype(BF16)


def _gdn_kernel(q_ref, k_ref, v_ref, z_ref, psm_ref, psmt_ref, arow_ref,
                acol_ref, nw_ref, *rest, chunks_per_step, n_side):
    side_in = rest[:n_side]
    o_ref = rest[n_side]
    side_out = rest[n_side + 1:2 * n_side + 1]
    s_ref = rest[2 * n_side + 1]
    R = CHUNK * chunks_per_step

    @pl.when(pl.program_id(0) == 0)
    def _():
        s_ref[...] = jnp.zeros_like(s_ref)

    for src, dst in zip(side_in, side_out):
        dst[...] = src[...].astype(BF16)

    psm = psm_ref[...]
    g_col = -jnp.exp(arow_ref[0:1, :]) * _softplus(psm + arow_ref[1:2, :])
    beta_col = _sigmoid(psm)
    cum_col = _dot_exact_lhs(_ones_where(_chunk_tril(R)), g_col)
    g_row = -jnp.exp(acol_ref[:, 0:1]) * _softplus(psmt_ref[...] + acol_ref[:, 1:2])
    ji = lax.broadcasted_iota(jnp.int32, (R, 2 * R), 0)
    li = lax.broadcasted_iota(jnp.int32, (R, 2 * R), 1)
    dup = ((ji >> 6) == (li >> 7)) & ((ji & (CHUNK - 1)) <= (li & (CHUNK - 1)))
    cum_row = _dot_exact_rhs(g_row, _ones_where(dup))

    _gdn_chunks(q_ref, k_ref, v_ref, z_ref, nw_ref, o_ref, s_ref, cum_col, cum_row, beta_col,
                chunks_per_step)


def _gdn(p_big, p_sm, p_smt, a_row, a_col, norm_w, side_weights, chunks_per_step):
    L = p_big.shape[0]
    R = CHUNK * chunks_per_step
    base = (GLA_QK * 2 + GLA_V * 2) // GDN_QK
    kern = functools.partial(_gdn_kernel, chunks_per_step=chunks_per_step,
                             n_side=len(side_weights))
    side_specs = [_row_cast_spec(L // R, w) for w in side_weights]
    outs = pl.pallas_call(
        kern,
        grid=(L // R,),
        in_specs=[
            pl.BlockSpec((R, GDN_QK), lambda n: (n, base)),
            pl.BlockSpec((R, GDN_QK), lambda n: (n, base + 1)),
            pl.BlockSpec((R, GDN_V), lambda n: (n, base + 2)),
            pl.BlockSpec((R, GDN_V), lambda n: (n, base + 3)),
            pl.BlockSpec((R, LANES), lambda n: (n, 0)),
            pl.BlockSpec((LANES, R), lambda n: (0, n)),
            pl.BlockSpec((2, LANES), lambda n: (0, 0)),
            pl.BlockSpec((LANES, 2), lambda n: (0, 0)),
            pl.BlockSpec((1, GDN_DV), lambda n: (0, 0)),
        ] + side_specs,
        out_specs=[pl.BlockSpec((R, GDN_V), lambda n: (n, 0))] + side_specs,
        out_shape=[jax.ShapeDtypeStruct((L, GDN_V), BF16)]
        + [jax.ShapeDtypeStruct(w.shape, BF16) for w in side_weights],
        scratch_shapes=[pltpu.VMEM((GDN_HEADS, GDN_DK, GDN_DV), F32)],
        compiler_params=pltpu.CompilerParams(
            dimension_semantics=("arbitrary",),
            vmem_limit_bytes=VMEM_LIMIT),
        name="gdn",
    )(p_big, p_big, p_big, p_big, p_sm, p_smt, a_row, a_col, norm_w, *side_weights)
    return outs[0], outs[1:]


def _outproj_kernel(x_ref, oa_ref, ob_ref, wa_ref, wb_ref, y_ref):
    y_ref[...] = (x_ref[...] + _dot(oa_ref[...], wa_ref[...])
                  + _dot(ob_ref[...], wb_ref[...]))


def _outproj(x2, o_a, o_b, w_out_b, tm):
    L, D = x2.shape
    return pl.pallas_call(
        _outproj_kernel,
        grid=(L // tm,),
        in_specs=[
            pl.BlockSpec((tm, D), lambda i: (i, 0)),
            pl.BlockSpec((tm, GLA_V), lambda i: (i, 0)),
            pl.BlockSpec((tm, GDN_V), lambda i: (i, 0)),
            pl.BlockSpec((GLA_V, D), lambda i: (0, 0)),
            pl.BlockSpec((GDN_V, D), lambda i: (GLA_V // GDN_V, 0)),
        ],
        out_specs=pl.BlockSpec((tm, D), lambda i: (i, 0)),
        out_shape=jax.ShapeDtypeStruct((L, D), F32),
        compiler_params=pltpu.CompilerParams(
            dimension_semantics=("arbitrary",),
            vmem_limit_bytes=VMEM_LIMIT),
        name="outproj",
    )(x2, o_a, o_b, w_out_b, w_out_b)


def _ffn_kernel(x_ref, fnw_ref, wg_ref, wu_ref, wd_ref, onw_ref, y_ref, hf_ref, acc_ref):
    f = pl.program_id(1)

    @pl.when(f == 0)
    def _():
        x = x_ref[...]
        ms = jnp.mean(x * x, axis=-1, keepdims=True)
        hf_ref[...] = (x * lax.rsqrt(ms + EPS) * fnw_ref[...]).astype(BF16)
        acc_ref[...] = jnp.zeros_like(acc_ref)

    hf = hf_ref[...]
    g = _dot(hf, wg_ref[...])
    u = _dot(hf, wu_ref[...])
    acc_ref[...] += _dot((_silu(g) * u).astype(BF16), wd_ref[...])

    @pl.when(f == pl.num_programs(1) - 1)
    def _():
        r = x_ref[...] + acc_ref[...]
        ms = jnp.mean(r * r, axis=-1, keepdims=True)
        y_ref[...] = r * lax.rsqrt(ms + EPS) * onw_ref[...]


def _ffn(x1, ffn_norm_w, wg, wu, wd, final_norm_w, tm, tf):
    L, D = x1.shape
    F = wg.shape[1]
    return pl.pallas_call(
        _ffn_kernel,
        grid=(L // tm, F // tf),
        in_specs=[
            pl.BlockSpec((tm, D), lambda i, f: (i, 0)),
            pl.BlockSpec((1, D), lambda i, f: (0, 0)),
            pl.BlockSpec((D, tf), lambda i, f: (0, f)),
            pl.BlockSpec((D, tf), lambda i, f: (0, f)),
            pl.BlockSpec((tf, D), lambda i, f: (f, 0)),
            pl.BlockSpec((1, D), lambda i, f: (0, 0)),
        ],
        out_specs=pl.BlockSpec((tm, D), lambda i, f: (i, 0)),
        out_shape=jax.ShapeDtypeStruct((L, D), F32),
        scratch_shapes=[pltpu.VMEM((tm, D), BF16), pltpu.VMEM((tm, D), F32)],
        compiler_params=pltpu.CompilerParams(
            dimension_semantics=("arbitrary", "arbitrary"),
            vmem_limit_bytes=VMEM_LIMIT),
        name="ffn",
    )(x1, ffn_norm_w, wg, wu, wd, final_norm_w)


def _hi_lo(w):
    hi = w.astype(BF16)
    lo = (w - hi.astype(F32)).astype(BF16)
    return jnp.stack([hi, lo])


def _diag_selector():
    s_of_row = np.arange(SUB_BLOCK * GLA_DK) // GLA_DK
    n_mod = np.arange(CHUNK) % SUB_BLOCK
    return jnp.asarray(s_of_row[:, None] == n_mod[None, :], dtype=BF16)


def _pick(n, candidates):
    for c in candidates:
        if n % c == 0:
            return c
    raise ValueError(f"no tile in {candidates} divides {n}")


def kernel(x, attn_norm_w, w_in, gla_gate_w2, gla_gate_b, gla_norm_w, gdn_conv_w,
           gdn_a_log, gdn_dt_bias, gdn_norm_w, w_out, ffn_norm_w, w_gate, w_up,
           w_down, final_norm_w):
    bsz, seq, d_model = x.shape
    assert bsz == 1 and seq % CHUNK == 0
    x2 = x.reshape(seq, d_model).astype(F32)

    sizes = (GLA_QK, GLA_QK, GLA_V, GLA_GATE_RANK, GLA_V,
             GDN_QK, GDN_QK, GDN_V, GDN_HEADS, GDN_HEADS, GDN_V)
    offs = np.concatenate([[0], np.cumsum(sizes)])
    assert w_in.shape == (d_model, offs[-1])
    w_in_t = w_in.astype(F32).T
    row = lambda i: w_in_t[offs[i]:offs[i + 1]]
    n_small = GLA_GATE_RANK + 2 * GDN_HEADS
    w_small_t = jnp.concatenate([row(3), row(8), row(9)], axis=0)
    w_small_t = jnp.pad(w_small_t, ((0, LANES - n_small), (0, 0)))
    w_smt = _hi_lo(w_small_t).reshape(2 * LANES, d_model)

    tm1 = _pick(seq, (1024, 512, 256, 128, 64))
    p_big, p_sm, p_smt, _ = _inproj(
        x2, attn_norm_w.reshape(1, -1).astype(F32), _wprep(w_in_t, 512), w_smt,
        gdn_conv_w.astype(F32), [], tm1, 512)

    cps = _pick(seq // CHUNK, (4, 2, 1))

    w2_pad = jnp.pad(gla_gate_w2.astype(F32), ((0, LANES - GLA_GATE_RANK), (0, 0)))
    o_a, w_out_b = _gla(p_big, p_sm, _hi_lo(w2_pad), gla_gate_b.reshape(1, -1).astype(F32),
                        gla_norm_w.reshape(1, -1).astype(F32), w_out.astype(F32), cps)

    a_log_pad = jnp.zeros((LANES,), F32).at[COL_A:COL_A + GDN_HEADS].set(gdn_a_log.astype(F32))
    dt_pad = jnp.zeros((LANES,), F32).at[COL_A:COL_A + GDN_HEADS].set(gdn_dt_bias.astype(F32))
    a_row = jnp.stack([a_log_pad, dt_pad])
    o_b, (w_gate_b, w_up_b, w_down_b) = _gdn(
        p_big, p_sm, p_smt, a_row, a_row.T, gdn_norm_w.reshape(1, -1).astype(F32),
        [w.astype(F32) for w in (w_gate, w_up, w_down)], cps)

    tm4 = _pick(seq, (512, 256, 128, 64))
    x1 = _outproj(x2, o_a, o_b, w_out_b, tm4)

    y = _ffn(x1, ffn_norm_w.reshape(1, -1).astype(F32), w_gate_b, w_up_b, w_down_b,
             final_norm_w.reshape(1, -1).astype(F32), tm4, 512)
    return y.reshape(bsz, seq, d_model).astype(x.dtype)
```

```python
import functools

import numpy as np
import jax
import jax.numpy as jnp
from jax import lax
from jax.experimental import pallas as pl
from jax.experimental.pallas import tpu as pltpu

F32 = jnp.float32
BF16 = jnp.bfloat16

EPS = 1e-6
CHUNK = 64

GLA_HEADS = 4
GLA_DK = 128
GLA_DV = 256
GLA_GATE_RANK = 16
GLA_GATE_TAU = 16.0
GLA_QK = GLA_HEADS * GLA_DK
GLA_V = GLA_HEADS * GLA_DV

GDN_HEADS = 8
GDN_DK = 128
GDN_DV = 128
GDN_CONV = 4
GDN_QK = GDN_HEADS * GDN_DK
GDN_V = GDN_HEADS * GDN_DV

LANES = 128
VMEM_LIMIT = 48 * 1024 * 1024
VMEM_LIMIT_INPROJ = 56 * 1024 * 1024
N_BIG = 2 * GLA_QK + 2 * GLA_V + 2 * GDN_QK + 2 * GDN_V

COL_LR = 0
COL_A = GLA_GATE_RANK
COL_B = GLA_GATE_RANK + GDN_HEADS


def _dot(a, b):
    return jnp.dot(a, b, preferred_element_type=F32)


def _dot_nt(a, b):
    return lax.dot_general(a, b, (((1,), (1,)), ((), ())), preferred_element_type=F32)


def _dot_tn(a, b):
    return lax.dot_general(a, b, (((0,), (0,)), ((), ())), preferred_element_type=F32)


def _split3(a):
    hi = a.astype(BF16)
    r1 = a - hi.astype(F32)
    mid = r1.astype(BF16)
    lo = (r1 - mid.astype(F32)).astype(BF16)
    return hi, mid, lo


def _dot_exact_rhs(a, b_exact):
    hi, mid, lo = _split3(a)
    return _dot(hi, b_exact) + _dot(mid, b_exact) + _dot(lo, b_exact)


def _dot_exact_lhs(a_exact, b):
    hi, mid, lo = _split3(b)
    return _dot(a_exact, hi) + _dot(a_exact, mid) + _dot(a_exact, lo)


def _sigmoid(x):
    return 1.0 / (1.0 + jnp.exp(-x))


def _silu(x):
    return x * _sigmoid(x)


def _softplus(x):
    return jnp.maximum(x, 0.0) + jnp.log1p(jnp.exp(-jnp.abs(x)))


def _log_sigmoid(x):
    return -_softplus(-x)


def _wprep_kernel(wa_ref, wb_ref, o_ref, *, shift_steps):
    j = pl.program_id(0)
    j1, j2 = shift_steps

    def realigned(delta):
        if delta == 0:
            return wa_ref[...].astype(BF16)
        return jnp.concatenate([wa_ref[pl.ds(delta, wa_ref.shape[0] - delta), :],
                                wb_ref[pl.ds(0, delta), :]], axis=0).astype(BF16)

    @pl.when(j < j1)
    def _():
        o_ref[...] = realigned(0)

    @pl.when((j >= j1) & (j < j2))
    def _():
        o_ref[...] = realigned(GLA_GATE_RANK)

    @pl.when(j >= j2)
    def _():
        o_ref[...] = realigned(GLA_GATE_RANK + 2 * GDN_HEADS)


def _wprep(w_in_t, tn):
    D = w_in_t.shape[1]
    n_small = GLA_GATE_RANK + 2 * GDN_HEADS
    g1 = GLA_QK * 2 + GLA_V
    g2 = g1 + GLA_V + GDN_QK * 2 + GDN_V
    assert g1 % tn == 0 and g2 % tn == 0 and N_BIG % tn == 0 and tn % n_small == 0
    return pl.pallas_call(
        functools.partial(_wprep_kernel, shift_steps=(g1 // tn, g2 // tn)),
        grid=(N_BIG // tn,),
        in_specs=[
            pl.BlockSpec((tn, D), lambda j: (j, 0)),
            pl.BlockSpec((n_small, D), lambda j: ((tn // n_small) * (j + 1), 0)),
        ],
        out_specs=pl.BlockSpec((tn, D), lambda j: (j, 0)),
        out_shape=jax.ShapeDtypeStruct((N_BIG, D), BF16),
        compiler_params=pltpu.CompilerParams(
            dimension_semantics=("arbitrary",),
            vmem_limit_bytes=VMEM_LIMIT),
        name="wprep",
    )(w_in_t, w_in_t)


def _inproj_kernel(x_ref, nw_ref, wbig_ref, wsmt_ref, cw_ref, pbig_ref, psm_ref, psmt_ref,
                   h_ref, ext_scr, tail_scr, *, tile_kinds):
    i = pl.program_id(0)
    j = pl.program_id(1)
    tm = h_ref.shape[0]
    tn = wbig_ref.shape[0]
    piece = min(tm, 256)

    @pl.when(j == 0)
    def _():
        x = x_ref[...]
        ms = jnp.mean(x * x, axis=-1, keepdims=True)
        h = x * lax.rsqrt(ms + EPS) * nw_ref[...]
        hb = h.astype(BF16)
        h_ref[...] = hb
        hl = (h - hb.astype(F32)).astype(BF16)
        r = _dot_nt(wsmt_ref[...], hb)
        pt = r[:LANES] + r[LANES:] + _dot_nt(wsmt_ref[pl.ds(0, LANES), :], hl)
        psmt_ref[...] = pt
        psm_ref[...] = pt.T

    def piece_dot(r0):
        return _dot_nt(h_ref[pl.ds(r0, piece), :], wbig_ref[...])

    def steps_of(*kinds):
        pred = None
        for jj, kind in enumerate(tile_kinds):
            if kind in kinds:
                pred = (j == jj) if pred is None else (pred | (j == jj))
        return pred

    @pl.when(steps_of('plain'))
    def _():
        pbig_ref[...] = _dot_nt(h_ref[...], wbig_ref[...]).astype(BF16)

    @pl.when(steps_of('silu'))
    def _():
        for r0 in range(0, tm, piece):
            pbig_ref[pl.ds(r0, piece), :] = _silu(piece_dot(r0)).astype(BF16)

    conv_kinds = ('conv_norm_q', 'conv_norm_k', 'conv')
    first_conv = min(jj for jj, kind in enumerate(tile_kinds) if kind in conv_kinds)

    def conv_steps(normalise):
        jc = j - first_conv
        ext_scr[pl.ds(0, 8), :] = jnp.where(i == 0, 0.0, tail_scr[jc])
        scale = jnp.where(steps_of('conv_norm_q'), GDN_DK ** -0.5, 1.0) if normalise else None
        for r0 in range(0, tm, piece):
            ext_scr[pl.ds(8 + r0, piece), :] = piece_dot(r0)
            acc = cw_ref[GDN_CONV - 1:GDN_CONV, :] * ext_scr[pl.ds(8 + r0, piece), :]
            for t in range(GDN_CONV - 1):
                acc = acc + cw_ref[t:t + 1, :] * ext_scr[pl.ds(8 - (GDN_CONV - 1) + t + r0, piece), :]
            y = _silu(acc)
            if normalise:
                heads = []
                for hh in range(tn // GDN_DK):
                    yh = y[:, hh * GDN_DK:(hh + 1) * GDN_DK]
                    ssq = jnp.sum(yh * yh, axis=-1, keepdims=True)
                    heads.append(yh * (lax.rsqrt(ssq + EPS) * scale))
                y = jnp.concatenate(heads, axis=1)
            pbig_ref[pl.ds(r0, piece), :] = y.astype(BF16)
        tail_scr[jc] = ext_scr[pl.ds(tm, 8), :]

    @pl.when(steps_of('conv_norm_q', 'conv_norm_k'))
    def _():
        conv_steps(True)

    @pl.when(steps_of('conv'))
    def _():
        conv_steps(False)


def _max_split(n_steps, size, quantum):
    for nb in range(n_steps, 0, -1):
        if size % (nb * quantum) == 0:
            return nb
    raise ValueError(f"{size} is not a multiple of {quantum}")


def _inproj(x2, norm_w, w_big, w_smt, conv_w, tm, tn):
    L, D = x2.shape
    widths = (('plain', 2 * GLA_QK + GLA_V), ('silu', GLA_V), ('conv_norm_q', GDN_QK),
              ('conv_norm_k', GDN_QK), ('conv', GDN_V), ('silu', GDN_V))
    assert all(width % tn == 0 for _, width in widths) and tn % GDN_DK == 0
    tile_kinds = tuple(kind for kind, width in widths for _ in range(width // tn))
    conv_tiles = [jj for jj, kind in enumerate(tile_kinds) if kind.startswith('conv')]
    assert conv_tiles == list(range(conv_tiles[0], conv_tiles[-1] + 1))
    return pl.pallas_call(
        functools.partial(_inproj_kernel, tile_kinds=tile_kinds),
        grid=(L // tm, N_BIG // tn),
        in_specs=[
            pl.BlockSpec((tm, D), lambda i, j: (i, 0)),
            pl.BlockSpec((1, D), lambda i, j: (0, 0)),
            pl.BlockSpec((tn, D), lambda i, j: (j, 0)),
            pl.BlockSpec((2 * LANES, D), lambda i, j: (0, 0)),
            pl.BlockSpec((GDN_CONV, tn),
                         lambda i, j: (0, jnp.clip(j - conv_tiles[0], 0, len(conv_tiles) - 1))),
        ],
        out_specs=[
            pl.BlockSpec((tm, tn), lambda i, j: (i, j)),
            pl.BlockSpec((tm, LANES), lambda i, j: (i, 0)),
            pl.BlockSpec((LANES, tm), lambda i, j: (0, i)),
        ],
        out_shape=[
            jax.ShapeDtypeStruct((L, N_BIG), BF16),
            jax.ShapeDtypeStruct((L, LANES), F32),
            jax.ShapeDtypeStruct((LANES, L), F32),
        ],
        scratch_shapes=[pltpu.VMEM((tm, D), BF16),
                        pltpu.VMEM((tm + 8, tn), F32),
                        pltpu.VMEM((len(conv_tiles), 8, tn), F32)],
        compiler_params=pltpu.CompilerParams(
            dimension_semantics=("arbitrary", "arbitrary"),
            vmem_limit_bytes=VMEM_LIMIT_INPROJ),
        name="inproj",
    )(x2, norm_w, w_big, w_smt, conv_w)


def _ones_where(mask):
    return jnp.where(mask, 1.0, 0.0).astype(BF16)


def _chunk_tril(rows, upper=False):
    ti = lax.broadcasted_iota(jnp.int32, (rows, rows), 0)
    si = lax.broadcasted_iota(jnp.int32, (rows, rows), 1)
    same = (ti >> 6) == (si >> 6)
    return same & ((ti <= si) if upper else (ti >= si))


def _gla_ref_rows(c, c_scr, r0, ks, blk):
    C = CHUNK
    if 2 * blk >= 8:
        return jnp.concatenate(
            [jnp.broadcast_to(c_scr[pl.ds(r0 + g + blk - 1, 1), ks], (2 * blk, GLA_DK))
             for g in range(0, C, 2 * blk)], axis=0)
    pos = lax.broadcasted_iota(jnp.int32, (C, GLA_DK), 0) & (2 * blk - 1)
    out = c
    for off in range(2 * blk):
        if off != blk - 1:
            shift = (off - (blk - 1)) % C
            out = jnp.where(pos == off, pltpu.roll(c, shift, axis=0), out)
    return out


def _gla_scores(q, k, c, c_scr, r0, ks):
    C = CHUNK
    row = lax.broadcasted_iota(jnp.int32, (C, GLA_DK), 0)
    ti = lax.broadcasted_iota(jnp.int32, (C, C), 0)
    si = lax.broadcasted_iota(jnp.int32, (C, C), 1)
    a = jnp.where(ti == si, jnp.sum(q * k, axis=-1, keepdims=True), 0.0)
    blk = C // 2
    while blk >= 1:
        right = (row & (2 * blk - 1)) >= blk
        d = c - _gla_ref_rows(c, c_scr, r0, ks, blk)
        e = jnp.exp(jnp.where(right, d, -d))
        qt = jnp.where(right, q * e, 0.0).astype(BF16)
        kt = jnp.where(right, 0.0, k * e).astype(BF16)
        part = _dot_nt(qt, kt)
        if 2 * blk < C:
            shift = (2 * blk).bit_length() - 1
            part = jnp.where((ti >> shift) == (si >> shift), part, 0.0)
        a = a + part
        blk //= 2
    return a


def _gla_chunks(q_ref, v_ref, gate_ref, nw_ref, o_ref, st_ref, c_scr, k_scr, chunks_per_step):
    C = CHUNK
    tasks = [(ci, h) for ci in range(chunks_per_step) for h in range(GLA_HEADS)]

    def front(t):
        ci, h = t
        r0 = ci * C
        ks = slice(h * GLA_DK, (h + 1) * GLA_DK)
        q = q_ref[pl.ds(r0, C), ks].astype(F32) * (GLA_DK ** -0.5)
        k = k_scr[pl.ds(r0, C), ks]
        c = c_scr[pl.ds(r0, C), ks]
        a = _gla_scores(q, k, c, c_scr, r0, ks)
        c_last = c_scr[pl.ds(r0 + C - 1, 1), ks]
        qd = (q * jnp.exp(c)).astype(BF16)
        kd = (k * jnp.exp(c_last - c)).astype(BF16)
        return a.astype(BF16), qd, kd, jnp.exp(c_last)

    def back(t, a, qd, kd, g_last):
        ci, h = t
        rows = pl.ds(ci * C, C)
        vs = slice(h * GLA_DV, (h + 1) * GLA_DV)
        v = v_ref[rows, vs]
        st = st_ref[h]
        o = _dot(a, v) + _dot_nt(qd, st.astype(BF16))
        st_ref[h] = g_last * st + _dot_tn(v, kd)
        ms = jnp.mean(o * o, axis=-1, keepdims=True)
        on = o * lax.rsqrt(ms + EPS) * nw_ref[...]
        o_ref[rows, vs] = (on * gate_ref[rows, vs].astype(F32)).astype(BF16)

    lag = 2
    fronts = {}
    for idx, t in enumerate(tasks):
        fronts[t] = front(t)
        if idx >= lag:
            back(tasks[idx - lag], *fronts.pop(tasks[idx - lag]))
    for t in tasks[-lag:]:
        back(t, *fronts.pop(t))


def _gla_kernel(q_ref, k_ref, v_ref, gate_ref, psm_ref, w2_ref, gb_ref, nw_ref,
                wsrc_ref, o_ref, wdst_ref, st_ref, c_scr, k_scr, *, chunks_per_step):
    @pl.when(pl.program_id(0) == 0)
    def _():
        st_ref[...] = jnp.zeros_like(st_ref)

    wdst_ref[...] = wsrc_ref[...].astype(BF16)

    a_hi, a_mid, _ = _split3(psm_ref[...])
    z = (_dot(a_hi, w2_ref[0]) + _dot(a_mid, w2_ref[0]) + _dot(a_hi, w2_ref[1])
         + gb_ref[...])
    log_a = _log_sigmoid(z) * (1.0 / GLA_GATE_TAU)
    tril = _ones_where(_chunk_tril(CHUNK * chunks_per_step))
    c_scr[...] = _dot_exact_lhs(tril, log_a)
    k_scr[...] = k_ref[...].astype(F32)

    _gla_chunks(q_ref, v_ref, gate_ref, nw_ref, o_ref, st_ref, c_scr, k_scr, chunks_per_step)


def _row_cast_spec(n_steps, w):
    nb = _max_split(n_steps, w.shape[0], 16)
    return pl.BlockSpec((w.shape[0] // nb, w.shape[1]), lambda n: (jnp.minimum(n, nb - 1), 0))


def _gla(p_big, p_sm, w2, gate_b, norm_w, side_w, chunks_per_step):
    L = p_big.shape[0]
    R = CHUNK * chunks_per_step
    side_spec = _row_cast_spec(L // R, side_w)
    kern = functools.partial(_gla_kernel, chunks_per_step=chunks_per_step)
    return pl.pallas_call(
        kern,
        grid=(L // R,),
        in_specs=[
            pl.BlockSpec((R, GLA_QK), lambda n: (n, 0)),
            pl.BlockSpec((R, GLA_QK), lambda n: (n, 1)),
            pl.BlockSpec((R, GLA_V), lambda n: (n, 1)),
            pl.BlockSpec((R, GLA_V), lambda n: (n, 2)),
            pl.BlockSpec((R, LANES), lambda n: (n, 0)),
            pl.BlockSpec((2, LANES, GLA_QK), lambda n: (0, 0, 0)),
            pl.BlockSpec((1, GLA_QK), lambda n: (0, 0)),
            pl.BlockSpec((1, GLA_DV), lambda n: (0, 0)),
            side_spec,
        ],
        out_specs=[pl.BlockSpec((R, GLA_V), lambda n: (n, 0)), side_spec],
        out_shape=[jax.ShapeDtypeStruct((L, GLA_V), BF16),
                   jax.ShapeDtypeStruct(side_w.shape, BF16)],
        scratch_shapes=[
            pltpu.VMEM((GLA_HEADS, GLA_DV, GLA_DK), F32),
            pltpu.VMEM((R, GLA_QK), F32),
            pltpu.VMEM((R, GLA_QK), F32),
        ],
        compiler_params=pltpu.CompilerParams(
            dimension_semantics=("arbitrary",),
            vmem_limit_bytes=VMEM_LIMIT),
        name="gla",
    )(p_big, p_big, p_big, p_big, p_sm, w2, gate_b, norm_w, side_w)


def _gdn_chunks(q_ref, k_ref, v_ref, z_ref, nw_ref, o_ref, s_ref, cum_col, cum_row, beta_col,
                chunks_per_step):
    C = CHUNK
    ti = lax.broadcasted_iota(jnp.int32, (C, 2 * C), 0)
    li = lax.broadcasted_iota(jnp.int32, (C, 2 * C), 1)
    si = li & (C - 1)
    right = li >= C
    incl = ti >= si
    strict = ti > si
    eye_right = jnp.where((ti == si) & right, 1.0, 0.0)
    tasks = [(ci, h) for ci in range(chunks_per_step) for h in range(GDN_HEADS)]

    wmat, attn, rhs, wq, kd, g_last = {}, {}, {}, {}, {}, {}
    for t in tasks:
        ci, h = t
        r0 = ci * C
        hs = slice(h * GDN_DK, (h + 1) * GDN_DK)
        qnb = q_ref[pl.ds(r0, C), hs]
        knb = k_ref[pl.ds(r0, C), hs]
        kn = knb.astype(F32)
        cv = v_ref[pl.ds(r0, C), hs].astype(F32)
        cc = cum_col[r0:r0 + C, COL_A + h:COL_A + h + 1]
        cr = cum_row[COL_A + h:COL_A + h + 1, 2 * r0:2 * r0 + 2 * C]
        beta = beta_col[r0:r0 + C, COL_B + h:COL_B + h + 1]
        gamma = jnp.exp(jnp.where(incl, cc - cr, -jnp.inf))
        kb = kn * beta
        x = _dot_nt(jnp.concatenate([qnb, kb.astype(BF16)], axis=0),
                    jnp.concatenate([knb, knb], axis=0))
        attn[t] = (x[:C, :C] * gamma[:, :C]).astype(BF16)
        wmat[t] = jnp.where(strict, x[C:] * gamma, 0.0)
        e_c = jnp.exp(cc)
        c_last = cc[C - 1:C, :]
        rhs1 = jnp.concatenate([cv * beta, kb * e_c], axis=1).astype(BF16)
        rhs[t] = jnp.concatenate([rhs1, rhs1], axis=0)
        wq[t] = (qnb.astype(F32) * e_c).astype(BF16)
        kd[t] = (kn * jnp.exp(c_last - cc)).astype(BF16)
        g_last[t] = jnp.exp(c_last)

    for t in tasks:
        n2b = wmat[t].astype(BF16)
        wmat[t] = jnp.where(right, eye_right - wmat[t], _dot(n2b[:, :C], n2b))
    for _ in range(5):
        for t in tasks:
            wb = wmat[t].astype(BF16)
            wmat[t] = _dot(wb[:, :C], wb) + jnp.where(right, wmat[t], 0.0)
    sol = {t: _dot(wmat[t].astype(BF16), rhs[t]) for t in tasks}

    for ci in range(chunks_per_step):
        rows = pl.ds(ci * C, C)
        heads = [(ci, h) for h in range(GDN_HEADS)]
        s_old = {t: s_ref[t[1]] for t in heads}
        ws = {t: _dot(jnp.concatenate([sol[t][:, GDN_DV:].astype(BF16), wq[t]], axis=0),
                      s_old[t].astype(BF16)) for t in heads}
        vnb = {t: (sol[t][:, :GDN_DV] - ws[t][:C]).astype(BF16) for t in heads}
        for t in heads:
            h = t[1]
            hs = slice(h * GDN_DK, (h + 1) * GDN_DK)
            o = ws[t][C:] + _dot(attn[t], vnb[t])
            s_ref[h] = g_last[t] * s_old[t] + _dot_tn(kd[t], vnb[t])
            ms = jnp.mean(o * o, axis=-1, keepdims=True)
            on = o * lax.rsqrt(ms + EPS) * nw_ref[...]
            o_ref[rows, hs] = (on * z_ref[rows, hs].astype(F32)).astype(BF16)


def _gdn_kernel(q_ref, k_ref, v_ref, z_ref, psm_ref, psmt_ref, arow_ref,
                acol_ref, nw_ref, *rest, chunks_per_step, n_side):
    side_in = rest[:n_side]
    o_ref = rest[n_side]
    side_out = rest[n_side + 1:2 * n_side + 1]
    s_ref = rest[2 * n_side + 1]
    R = CHUNK * chunks_per_step

    @pl.when(pl.program_id(0) == 0)
    def _():
        s_ref[...] = jnp.zeros_like(s_ref)

    for src, dst in zip(side_in, side_out):
        dst[...] = src[...].astype(BF16)

    psm = psm_ref[...]
    g_col = -jnp.exp(arow_ref[0:1, :]) * _softplus(psm + arow_ref[1:2, :])
    beta_col = _sigmoid(psm)
    cum_col = _dot_exact_lhs(_ones_where(_chunk_tril(R)), g_col)
    g_row = -jnp.exp(acol_ref[:, 0:1]) * _softplus(psmt_ref[...] + acol_ref[:, 1:2])
    ji = lax.broadcasted_iota(jnp.int32, (R, 2 * R), 0)
    li = lax.broadcasted_iota(jnp.int32, (R, 2 * R), 1)
    dup = ((ji >> 6) == (li >> 7)) & ((ji & (CHUNK - 1)) <= (li & (CHUNK - 1)))
    cum_row = _dot_exact_rhs(g_row, _ones_where(dup))

    _gdn_chunks(q_ref, k_ref, v_ref, z_ref, nw_ref, o_ref, s_ref, cum_col, cum_row, beta_col,
                chunks_per_step)


def _gdn(p_big, p_sm, p_smt, a_row, a_col, norm_w, side_weights, chunks_per_step):
    L = p_big.shape[0]
    R = CHUNK * chunks_per_step
    base = (GLA_QK * 2 + GLA_V * 2) // GDN_QK
    kern = functools.partial(_gdn_kernel, chunks_per_step=chunks_per_step,
                             n_side=len(side_weights))
    side_specs = [_row_cast_spec(L // R, w) for w in side_weights]
    outs = pl.pallas_call(
        kern,
        grid=(L // R,),
        in_specs=[
            pl.BlockSpec((R, GDN_QK), lambda n: (n, base)),
            pl.BlockSpec((R, GDN_QK), lambda n: (n, base + 1)),
            pl.BlockSpec((R, GDN_V), lambda n: (n, base + 2)),
            pl.BlockSpec((R, GDN_V), lambda n: (n, base + 3)),
            pl.BlockSpec((R, LANES), lambda n: (n, 0)),
            pl.BlockSpec((LANES, R), lambda n: (0, n)),
            pl.BlockSpec((2, LANES), lambda n: (0, 0)),
            pl.BlockSpec((LANES, 2), lambda n: (0, 0)),
            pl.BlockSpec((1, GDN_DV), lambda n: (0, 0)),
        ] + side_specs,
        out_specs=[pl.BlockSpec((R, GDN_V), lambda n: (n, 0))] + side_specs,
        out_shape=[jax.ShapeDtypeStruct((L, GDN_V), BF16)]
        + [jax.ShapeDtypeStruct(w.shape, BF16) for w in side_weights],
        scratch_shapes=[pltpu.VMEM((GDN_HEADS, GDN_DK, GDN_DV), F32)],
        compiler_params=pltpu.CompilerParams(
            dimension_semantics=("arbitrary",),
            vmem_limit_bytes=VMEM_LIMIT),
        name="gdn",
    )(p_big, p_big, p_big, p_big, p_sm, p_smt, a_row, a_col, norm_w, *side_weights)
    return outs[0], outs[1:]


def _outproj_kernel(x_ref, oa_ref, ob_ref, wa_ref, wb_ref, y_ref):
    y_ref[...] = (x_ref[...] + _dot(oa_ref[...], wa_ref[...])
                  + _dot(ob_ref[...], wb_ref[...]))


def _outproj(x2, o_a, o_b, w_out_b, tm):
    L, D = x2.shape
    return pl.pallas_call(
        _outproj_kernel,
        grid=(L // tm,),
        in_specs=[
            pl.BlockSpec((tm, D), lambda i: (i, 0)),
            pl.BlockSpec((tm, GLA_V), lambda i: (i, 0)),
            pl.BlockSpec((tm, GDN_V), lambda i: (i, 0)),
            pl.BlockSpec((GLA_V, D), lambda i: (0, 0)),
            pl.BlockSpec((GDN_V, D), lambda i: (GLA_V // GDN_V, 0)),
        ],
        out_specs=pl.BlockSpec((tm, D), lambda i: (i, 0)),
        out_shape=jax.ShapeDtypeStruct((L, D), F32),
        compiler_params=pltpu.CompilerParams(
            dimension_semantics=("arbitrary",),
            vmem_limit_bytes=VMEM_LIMIT),
        name="outproj",
    )(x2, o_a, o_b, w_out_b, w_out_b)


def _ffn_kernel(x_ref, fnw_ref, wg_ref, wu_ref, wd_ref, onw_ref, y_ref, hf_ref, acc_ref):
    f = pl.program_id(1)

    @pl.when(f == 0)
    def _():
        x = x_ref[...]
        ms = jnp.mean(x * x, axis=-1, keepdims=True)
        hf_ref[...] = (x * lax.rsqrt(ms + EPS) * fnw_ref[...]).astype(BF16)
        acc_ref[...] = jnp.zeros_like(acc_ref)

    hf = hf_ref[...]
    g = _dot(hf, wg_ref[...])
    u = _dot(hf, wu_ref[...])
    acc_ref[...] += _dot((_silu(g) * u).astype(BF16), wd_ref[...])

    @pl.when(f == pl.num_programs(1) - 1)
    def _():
        r = x_ref[...] + acc_ref[...]
        ms = jnp.mean(r * r, axis=-1, keepdims=True)
        y_ref[...] = r * lax.rsqrt(ms + EPS) * onw_ref[...]


def _ffn(x1, ffn_norm_w, wg, wu, wd, final_norm_w, tm, tf):
    L, D = x1.shape
    F = wg.shape[1]
    return pl.pallas_call(
        _ffn_kernel,
        grid=(L // tm, F // tf),
        in_specs=[
            pl.BlockSpec((tm, D), lambda i, f: (i, 0)),
            pl.BlockSpec((1, D), lambda i, f: (0, 0)),
            pl.BlockSpec((D, tf), lambda i, f: (0, f)),
            pl.BlockSpec((D, tf), lambda i, f: (0, f)),
            pl.BlockSpec((tf, D), lambda i, f: (f, 0)),
            pl.BlockSpec((1, D), lambda i, f: (0, 0)),
        ],
        out_specs=pl.BlockSpec((tm, D), lambda i, f: (i, 0)),
        out_shape=jax.ShapeDtypeStruct((L, D), F32),
        scratch_shapes=[pltpu.VMEM((tm, D), BF16), pltpu.VMEM((tm, D), F32)],
        compiler_params=pltpu.CompilerParams(
            dimension_semantics=("arbitrary", "arbitrary"),
            vmem_limit_bytes=VMEM_LIMIT),
        name="ffn",
    )(x1, ffn_norm_w, wg, wu, wd, final_norm_w)


def _hi_lo(w):
    hi = w.astype(BF16)
    lo = (w - hi.astype(F32)).astype(BF16)
    return jnp.stack([hi, lo])


def _pick(n, candidates):
    for c in candidates:
        if n % c == 0:
            return c
    raise ValueError(f"no tile in {candidates} divides {n}")


def kernel(x, attn_norm_w, w_in, gla_gate_w2, gla_gate_b, gla_norm_w, gdn_conv_w,
           gdn_a_log, gdn_dt_bias, gdn_norm_w, w_out, ffn_norm_w, w_gate, w_up,
           w_down, final_norm_w):
    bsz, seq, d_model = x.shape
    assert bsz == 1 and seq % CHUNK == 0
    x2 = x.reshape(seq, d_model).astype(F32)

    sizes = (GLA_QK, GLA_QK, GLA_V, GLA_GATE_RANK, GLA_V,
             GDN_QK, GDN_QK, GDN_V, GDN_HEADS, GDN_HEADS, GDN_V)
    offs = np.concatenate([[0], np.cumsum(sizes)])
    assert w_in.shape == (d_model, offs[-1])
    w_in_t = w_in.astype(F32).T
    row = lambda i: w_in_t[offs[i]:offs[i + 1]]
    n_small = GLA_GATE_RANK + 2 * GDN_HEADS
    w_small_t = jnp.concatenate([row(3), row(8), row(9)], axis=0)
    w_small_t = jnp.pad(w_small_t, ((0, LANES - n_small), (0, 0)))
    w_smt = _hi_lo(w_small_t).reshape(2 * LANES, d_model)

    tm1 = _pick(seq, (1024, 512, 256, 128, 64))
    p_big, p_sm, p_smt = _inproj(
        x2, attn_norm_w.reshape(1, -1).astype(F32), _wprep(w_in_t, 512), w_smt,
        gdn_conv_w.astype(F32), tm1, 1024)

    cps = _pick(seq // CHUNK, (4, 2, 1))

    w2_pad = jnp.pad(gla_gate_w2.astype(F32), ((0, LANES - GLA_GATE_RANK), (0, 0)))
    o_a, w_out_b = _gla(p_big, p_sm, _hi_lo(w2_pad), gla_gate_b.reshape(1, -1).astype(F32),
                        gla_norm_w.reshape(1, -1).astype(F32), w_out.astype(F32), cps)

    a_log_pad = jnp.zeros((LANES,), F32).at[COL_A:COL_A + GDN_HEADS].set(gdn_a_log.astype(F32))
    dt_pad = jnp.zeros((LANES,), F32).at[COL_A:COL_A + GDN_HEADS].set(gdn_dt_bias.astype(F32))
    a_row = jnp.stack([a_log_pad, dt_pad])
    o_b, (w_gate_b, w_up_b, w_down_b) = _gdn(
        p_big, p_sm, p_smt, a_row, a_row.T, gdn_norm_w.reshape(1, -1).astype(F32),
        [w.astype(F32) for w in (w_gate, w_up, w_down)], cps)

    tm4 = _pick(seq, (512, 256, 128, 64))
    x1 = _outproj(x2, o_a, o_b, w_out_b, tm4)

    y = _ffn(x1, ffn_norm_w.reshape(1, -1).astype(F32), w_gate_b, w_up_b, w_down_b,
             final_norm_w.reshape(1, -1).astype(F32), tm4, 512)
    return y.reshape(bsz, seq, d_model).astype(x.dtype)
```

```python
import functools

import numpy as np
import jax
import jax.numpy as jnp
from jax import lax
from jax.experimental import pallas as pl
from jax.experimental.pallas import tpu as pltpu

F32 = jnp.float32
BF16 = jnp.bfloat16

EPS = 1e-6
CHUNK = 64

GLA_HEADS = 4
GLA_DK = 128
GLA_DV = 256
GLA_GATE_RANK = 16
GLA_GATE_TAU = 16.0
GLA_QK = GLA_HEADS * GLA_DK
GLA_V = GLA_HEADS * GLA_DV

GDN_HEADS = 8
GDN_DK = 128
GDN_DV = 128
GDN_CONV = 4
GDN_QK = GDN_HEADS * GDN_DK
GDN_V = GDN_HEADS * GDN_DV

LANES = 128
VMEM_LIMIT = 48 * 1024 * 1024
VMEM_LIMIT_INPROJ = 56 * 1024 * 1024
N_BIG = 2 * GLA_QK + 2 * GLA_V + 2 * GDN_QK + 2 * GDN_V

COL_LR = 0
COL_A = GLA_GATE_RANK
COL_B = GLA_GATE_RANK + GDN_HEADS


def _dot(a, b):
    return jnp.dot(a, b, preferred_element_type=F32)


def _dot_nt(a, b):
    return lax.dot_general(a, b, (((1,), (1,)), ((), ())), preferred_element_type=F32)


def _dot_tn(a, b):
    return lax.dot_general(a, b, (((0,), (0,)), ((), ())), preferred_element_type=F32)


def _split3(a):
    hi = a.astype(BF16)
    r1 = a - hi.astype(F32)
    mid = r1.astype(BF16)
    lo = (r1 - mid.astype(F32)).astype(BF16)
    return hi, mid, lo


def _dot_exact_rhs(a, b_exact):
    hi, mid, lo = _split3(a)
    return _dot(hi, b_exact) + _dot(mid, b_exact) + _dot(lo, b_exact)


def _dot_exact_lhs(a_exact, b):
    hi, mid, lo = _split3(b)
    return _dot(a_exact, hi) + _dot(a_exact, mid) + _dot(a_exact, lo)


def _sigmoid(x):
    return 1.0 / (1.0 + jnp.exp(-x))


def _silu(x):
    return x * _sigmoid(x)


def _softplus(x):
    return jnp.maximum(x, 0.0) + jnp.log1p(jnp.exp(-jnp.abs(x)))


def _log_sigmoid(x):
    return -_softplus(-x)


def _wprep_kernel(wa_ref, wb_ref, wout_ref, o_ref, wout_b_ref, *, shift_steps):
    j = pl.program_id(0)
    j1, j2 = shift_steps
    wout_b_ref[...] = wout_ref[...].astype(BF16)

    def realigned(delta):
        if delta == 0:
            return wa_ref[...].astype(BF16)
        return jnp.concatenate([wa_ref[pl.ds(delta, wa_ref.shape[0] - delta), :],
                                wb_ref[pl.ds(0, delta), :]], axis=0).astype(BF16)

    @pl.when(j < j1)
    def _():
        o_ref[...] = realigned(0)

    @pl.when((j >= j1) & (j < j2))
    def _():
        o_ref[...] = realigned(GLA_GATE_RANK)

    @pl.when(j >= j2)
    def _():
        o_ref[...] = realigned(GLA_GATE_RANK + 2 * GDN_HEADS)


def _wprep(w_in_t, w_out, tn):
    D = w_in_t.shape[1]
    wout_spec = _row_cast_spec(N_BIG // tn, w_out)
    n_small = GLA_GATE_RANK + 2 * GDN_HEADS
    g1 = GLA_QK * 2 + GLA_V
    g2 = g1 + GLA_V + GDN_QK * 2 + GDN_V
    assert g1 % tn == 0 and g2 % tn == 0 and N_BIG % tn == 0 and tn % n_small == 0
    return pl.pallas_call(
        functools.partial(_wprep_kernel, shift_steps=(g1 // tn, g2 // tn)),
        grid=(N_BIG // tn,),
        in_specs=[
            pl.BlockSpec((tn, D), lambda j: (j, 0)),
            pl.BlockSpec((n_small, D), lambda j: ((tn // n_small) * (j + 1), 0)),
            wout_spec,
        ],
        out_specs=[pl.BlockSpec((tn, D), lambda j: (j, 0)), wout_spec],
        out_shape=[jax.ShapeDtypeStruct((N_BIG, D), BF16),
                   jax.ShapeDtypeStruct(w_out.shape, BF16)],
        compiler_params=pltpu.CompilerParams(
            dimension_semantics=("arbitrary",),
            vmem_limit_bytes=VMEM_LIMIT),
        name="wprep",
    )(w_in_t, w_in_t, w_out)


def _inproj_kernel(x_ref, nw_ref, wbig_ref, wsmt_ref, cw_ref, pbig_ref, psm_ref, psmt_ref,
                   h_ref, ext_scr, tail_scr, *, tile_kinds):
    i = pl.program_id(0)
    j = pl.program_id(1)
    tm = h_ref.shape[0]
    tn = wbig_ref.shape[0]
    piece = min(tm, 256)

    @pl.when(j == 0)
    def _():
        x = x_ref[...]
        ms = jnp.mean(x * x, axis=-1, keepdims=True)
        h = x * lax.rsqrt(ms + EPS) * nw_ref[...]
        hb = h.astype(BF16)
        h_ref[...] = hb
        hl = (h - hb.astype(F32)).astype(BF16)
        r = _dot_nt(wsmt_ref[...], hb)
        pt = r[:LANES] + r[LANES:] + _dot_nt(wsmt_ref[pl.ds(0, LANES), :], hl)
        psmt_ref[...] = pt
        psm_ref[...] = pt.T

    def piece_dot(r0):
        return _dot_nt(h_ref[pl.ds(r0, piece), :], wbig_ref[...])

    def steps_of(*kinds):
        pred = None
        for jj, kind in enumerate(tile_kinds):
            if kind in kinds:
                pred = (j == jj) if pred is None else (pred | (j == jj))
        return pred

    @pl.when(steps_of('plain'))
    def _():
        pbig_ref[...] = _dot_nt(h_ref[...], wbig_ref[...]).astype(BF16)

    @pl.when(steps_of('silu'))
    def _():
        for r0 in range(0, tm, piece):
            pbig_ref[pl.ds(r0, piece), :] = _silu(piece_dot(r0)).astype(BF16)

    conv_kinds = ('conv_norm_q', 'conv_norm_k', 'conv')
    first_conv = min(jj for jj, kind in enumerate(tile_kinds) if kind in conv_kinds)

    def conv_steps(normalise):
        jc = j - first_conv
        ext_scr[pl.ds(0, 8), :] = jnp.where(i == 0, 0.0, tail_scr[jc])
        scale = jnp.where(steps_of('conv_norm_q'), GDN_DK ** -0.5, 1.0) if normalise else None
        for r0 in range(0, tm, piece):
            ext_scr[pl.ds(8 + r0, piece), :] = piece_dot(r0)
            acc = cw_ref[GDN_CONV - 1:GDN_CONV, :] * ext_scr[pl.ds(8 + r0, piece), :]
            for t in range(GDN_CONV - 1):
                acc = acc + cw_ref[t:t + 1, :] * ext_scr[pl.ds(8 - (GDN_CONV - 1) + t + r0, piece), :]
            y = _silu(acc)
            if normalise:
                heads = []
                for hh in range(tn // GDN_DK):
                    yh = y[:, hh * GDN_DK:(hh + 1) * GDN_DK]
                    ssq = jnp.sum(yh * yh, axis=-1, keepdims=True)
                    heads.append(yh * (lax.rsqrt(ssq + EPS) * scale))
                y = jnp.concatenate(heads, axis=1)
            pbig_ref[pl.ds(r0, piece), :] = y.astype(BF16)
        tail_scr[jc] = ext_scr[pl.ds(tm, 8), :]

    @pl.when(steps_of('conv_norm_q', 'conv_norm_k'))
    def _():
        conv_steps(True)

    @pl.when(steps_of('conv'))
    def _():
        conv_steps(False)


def _max_split(n_steps, size, quantum):
    for nb in range(n_steps, 0, -1):
        if size % (nb * quantum) == 0:
            return nb
    raise ValueError(f"{size} is not a multiple of {quantum}")


def _inproj(x2, norm_w, w_big, w_smt, conv_w, tm, tn):
    L, D = x2.shape
    widths = (('plain', 2 * GLA_QK + GLA_V), ('silu', GLA_V), ('conv_norm_q', GDN_QK),
              ('conv_norm_k', GDN_QK), ('conv', GDN_V), ('silu', GDN_V))
    assert all(width % tn == 0 for _, width in widths) and tn % GDN_DK == 0
    tile_kinds = tuple(kind for kind, width in widths for _ in range(width // tn))
    conv_tiles = [jj for jj, kind in enumerate(tile_kinds) if kind.startswith('conv')]
    assert conv_tiles == list(range(conv_tiles[0], conv_tiles[-1] + 1))
    return pl.pallas_call(
        functools.partial(_inproj_kernel, tile_kinds=tile_kinds),
        grid=(L // tm, N_BIG // tn),
        in_specs=[
            pl.BlockSpec((tm, D), lambda i, j: (i, 0)),
            pl.BlockSpec((1, D), lambda i, j: (0, 0)),
            pl.BlockSpec((tn, D), lambda i, j: (j, 0)),
            pl.BlockSpec((2 * LANES, D), lambda i, j: (0, 0)),
            pl.BlockSpec((GDN_CONV, tn),
                         lambda i, j: (0, jnp.clip(j - conv_tiles[0], 0, len(conv_tiles) - 1))),
        ],
        out_specs=[
            pl.BlockSpec((tm, tn), lambda i, j: (i, j)),
            pl.BlockSpec((tm, LANES), lambda i, j: (i, 0)),
            pl.BlockSpec((LANES, tm), lambda i, j: (0, i)),
        ],
        out_shape=[
            jax.ShapeDtypeStruct((L, N_BIG), BF16),
            jax.ShapeDtypeStruct((L, LANES), F32),
            jax.ShapeDtypeStruct((LANES, L), F32),
        ],
        scratch_shapes=[pltpu.VMEM((tm, D), BF16),
                        pltpu.VMEM((tm + 8, tn), F32),
                        pltpu.VMEM((len(conv_tiles), 8, tn), F32)],
        compiler_params=pltpu.CompilerParams(
            dimension_semantics=("arbitrary", "arbitrary"),
            vmem_limit_bytes=VMEM_LIMIT_INPROJ),
        name="inproj",
    )(x2, norm_w, w_big, w_smt, conv_w)


def _ones_where(mask):
    return jnp.where(mask, 1.0, 0.0).astype(BF16)


def _chunk_tril(rows, upper=False):
    ti = lax.broadcasted_iota(jnp.int32, (rows, rows), 0)
    si = lax.broadcasted_iota(jnp.int32, (rows, rows), 1)
    same = (ti >> 6) == (si >> 6)
    return same & ((ti <= si) if upper else (ti >= si))


def _gla_ref_rows(c, c_scr, r0, ks, blk):
    C = CHUNK
    if 2 * blk >= 8:
        return jnp.concatenate(
            [jnp.broadcast_to(c_scr[pl.ds(r0 + g + blk - 1, 1), ks], (2 * blk, GLA_DK))
             for g in range(0, C, 2 * blk)], axis=0)
    pos = lax.broadcasted_iota(jnp.int32, (C, GLA_DK), 0) & (2 * blk - 1)
    out = c
    for off in range(2 * blk):
        if off != blk - 1:
            shift = (off - (blk - 1)) % C
            out = jnp.where(pos == off, pltpu.roll(c, shift, axis=0), out)
    return out


def _gla_scores(q, k, c, c_scr, r0, ks):
    C = CHUNK
    row = lax.broadcasted_iota(jnp.int32, (C, GLA_DK), 0)
    ti = lax.broadcasted_iota(jnp.int32, (C, C), 0)
    si = lax.broadcasted_iota(jnp.int32, (C, C), 1)
    a = jnp.where(ti == si, jnp.sum(q * k, axis=-1, keepdims=True), 0.0)
    blk = C // 2
    while blk >= 1:
        right = (row & (2 * blk - 1)) >= blk
        d = c - _gla_ref_rows(c, c_scr, r0, ks, blk)
        e = jnp.exp(jnp.where(right, d, -d))
        qt = jnp.where(right, q * e, 0.0).astype(BF16)
        kt = jnp.where(right, 0.0, k * e).astype(BF16)
        part = _dot_nt(qt, kt)
        if 2 * blk < C:
            shift = (2 * blk).bit_length() - 1
            part = jnp.where((ti >> shift) == (si >> shift), part, 0.0)
        a = a + part
        blk //= 2
    return a


def _lagged_proj_pieces(o_scr, base_ref, w_ref, out_ref, n_pieces):
    prev = 1 - (pl.program_id(0) & 1)
    width = w_ref.shape[1] // n_pieces

    def make(p):
        cols = slice(p * width, (p + 1) * width)

        def run():
            out_ref[:, cols] = base_ref[:, cols] + _dot(o_scr[prev], w_ref[:, cols])
        return run
    return [make(p) for p in range(n_pieces)]


def _gla_chunks(q_ref, v_ref, gate_ref, nw_ref, o_ref, st_ref, c_scr, k_scr, chunks_per_step,
                fillers):
    C = CHUNK
    tasks = [(ci, h) for ci in range(chunks_per_step) for h in range(GLA_HEADS)]

    def front(t):
        ci, h = t
        r0 = ci * C
        ks = slice(h * GLA_DK, (h + 1) * GLA_DK)
        q = q_ref[pl.ds(r0, C), ks].astype(F32) * (GLA_DK ** -0.5)
        k = k_scr[pl.ds(r0, C), ks]
        c = c_scr[pl.ds(r0, C), ks]
        a = _gla_scores(q, k, c, c_scr, r0, ks)
        c_last = c_scr[pl.ds(r0 + C - 1, 1), ks]
        qd = (q * jnp.exp(c)).astype(BF16)
        kd = (k * jnp.exp(c_last - c)).astype(BF16)
        return a.astype(BF16), qd, kd, jnp.exp(c_last)

    def back(t, a, qd, kd, g_last):
        ci, h = t
        rows = pl.ds(ci * C, C)
        vs = slice(h * GLA_DV, (h + 1) * GLA_DV)
        v = v_ref[rows, vs]
        st = st_ref[h]
        o = _dot(a, v) + _dot_nt(qd, st.astype(BF16))
        st_ref[h] = g_last * st + _dot_tn(v, kd)
        ms = jnp.mean(o * o, axis=-1, keepdims=True)
        on = o * lax.rsqrt(ms + EPS) * nw_ref[...]
        o_ref[rows, vs] = (on * gate_ref[rows, vs].astype(F32)).astype(BF16)

    lag = 2
    fronts = {}
    fillers = list(fillers)
    every = max(1, len(tasks) // max(1, len(fillers)))
    for idx, t in enumerate(tasks):
        fronts[t] = front(t)
        if fillers and idx % every == 0:
            fillers.pop(0)()
        if idx >= lag:
            back(tasks[idx - lag], *fronts.pop(tasks[idx - lag]))
    for t in tasks[-lag:]:
        back(t, *fronts.pop(t))
    for f in fillers:
        f()


def _gla_kernel(q_ref, k_ref, v_ref, gate_ref, psm_ref, w2_ref, gb_ref, nw_ref,
                base_ref, wproj_ref, xo_ref, st_ref, c_scr, k_scr, o_scr, *, chunks_per_step):
    slot = pl.program_id(0) & 1

    @pl.when(pl.program_id(0) == 0)
    def _():
        st_ref[...] = jnp.zeros_like(st_ref)
        o_scr[1] = jnp.zeros(o_scr.shape[1:], o_scr.dtype)

    a_hi, a_mid, _ = _split3(psm_ref[...])
    z = (_dot(a_hi, w2_ref[0]) + _dot(a_mid, w2_ref[0]) + _dot(a_hi, w2_ref[1])
         + gb_ref[...])
    log_a = _log_sigmoid(z) * (1.0 / GLA_GATE_TAU)
    tril = _ones_where(_chunk_tril(CHUNK * chunks_per_step))
    c_scr[...] = _dot_exact_lhs(tril, log_a)
    k_scr[...] = k_ref[...].astype(F32)

    _gla_chunks(q_ref, v_ref, gate_ref, nw_ref, o_scr.at[slot], st_ref, c_scr, k_scr,
                chunks_per_step, _lagged_proj_pieces(o_scr, base_ref, wproj_ref, xo_ref, 8))


def _row_cast_spec(n_steps, w):
    nb = _max_split(n_steps, w.shape[0], 16)
    return pl.BlockSpec((w.shape[0] // nb, w.shape[1]), lambda n: (jnp.minimum(n, nb - 1), 0))


def _gla(p_big, p_sm, w2, gate_b, norm_w, x2, w_out_b, chunks_per_step):
    L, D = x2.shape
    R = CHUNK * chunks_per_step
    last = L // R - 1
    cur = lambda n: jnp.minimum(n, last)
    prv = lambda n: jnp.maximum(n - 1, 0)
    kern = functools.partial(_gla_kernel, chunks_per_step=chunks_per_step)
    return pl.pallas_call(
        kern,
        grid=(L // R + 1,),
        in_specs=[
            pl.BlockSpec((R, GLA_QK), lambda n: (cur(n), 0)),
            pl.BlockSpec((R, GLA_QK), lambda n: (cur(n), 1)),
            pl.BlockSpec((R, GLA_V), lambda n: (cur(n), 1)),
            pl.BlockSpec((R, GLA_V), lambda n: (cur(n), 2)),
            pl.BlockSpec((R, LANES), lambda n: (cur(n), 0)),
            pl.BlockSpec((2, LANES, GLA_QK), lambda n: (0, 0, 0)),
            pl.BlockSpec((1, GLA_QK), lambda n: (0, 0)),
            pl.BlockSpec((1, GLA_DV), lambda n: (0, 0)),
            pl.BlockSpec((R, D), lambda n: (prv(n), 0)),
            pl.BlockSpec((GLA_V, D), lambda n: (0, 0)),
        ],
        out_specs=pl.BlockSpec((R, D), lambda n: (prv(n), 0)),
        out_shape=jax.ShapeDtypeStruct((L, D), F32),
        scratch_shapes=[
            pltpu.VMEM((GLA_HEADS, GLA_DV, GLA_DK), F32),
            pltpu.VMEM((R, GLA_QK), F32),
            pltpu.VMEM((R, GLA_QK), F32),
            pltpu.VMEM((2, R, GLA_V), BF16),
        ],
        compiler_params=pltpu.CompilerParams(
            dimension_semantics=("arbitrary",),
            vmem_limit_bytes=VMEM_LIMIT),
        name="gla",
    )(p_big, p_big, p_big, p_big, p_sm, w2, gate_b, norm_w, x2, w_out_b)


def _gdn_chunks(q_ref, k_ref, v_ref, z_ref, nw_ref, o_ref, s_ref, cum_col, cum_row, beta_col,
                chunks_per_step, fillers):
    C = CHUNK
    fillers = list(fillers)

    def fill():
        if fillers:
            fillers.pop(0)()
    ti = lax.broadcasted_iota(jnp.int32, (C, 2 * C), 0)
    li = lax.broadcasted_iota(jnp.int32, (C, 2 * C), 1)
    si = li & (C - 1)
    right = li >= C
    incl = ti >= si
    strict = ti > si
    eye_right = jnp.where((ti == si) & right, 1.0, 0.0)
    tasks = [(ci, h) for ci in range(chunks_per_step) for h in range(GDN_HEADS)]

    wmat, attn, rhs, wq, kd, g_last = {}, {}, {}, {}, {}, {}
    for t in tasks:
        ci, h = t
        r0 = ci * C
        hs = slice(h * GDN_DK, (h + 1) * GDN_DK)
        qnb = q_ref[pl.ds(r0, C), hs]
        knb = k_ref[pl.ds(r0, C), hs]
        kn = knb.astype(F32)
        cv = v_ref[pl.ds(r0, C), hs].astype(F32)
        cc = cum_col[r0:r0 + C, COL_A + h:COL_A + h + 1]
        cr = cum_row[COL_A + h:COL_A + h + 1, 2 * r0:2 * r0 + 2 * C]
        beta = beta_col[r0:r0 + C, COL_B + h:COL_B + h + 1]
        gamma = jnp.exp(jnp.where(incl, cc - cr, -jnp.inf))
        kb = kn * beta
        x = _dot_nt(jnp.concatenate([qnb, kb.astype(BF16)], axis=0),
                    jnp.concatenate([knb, knb], axis=0))
        attn[t] = (x[:C, :C] * gamma[:, :C]).astype(BF16)
        wmat[t] = jnp.where(strict, x[C:] * gamma, 0.0)
        e_c = jnp.exp(cc)
        c_last = cc[C - 1:C, :]
        rhs1 = jnp.concatenate([cv * beta, kb * e_c], axis=1).astype(BF16)
        rhs[t] = jnp.concatenate([rhs1, rhs1], axis=0)
        wq[t] = (qnb.astype(F32) * e_c).astype(BF16)
        kd[t] = (kn * jnp.exp(c_last - cc)).astype(BF16)
        g_last[t] = jnp.exp(c_last)

    fill()
    for t in tasks:
        n2b = wmat[t].astype(BF16)
        wmat[t] = jnp.where(right, eye_right - wmat[t], _dot(n2b[:, :C], n2b))
    fill()
    for _ in range(5):
        for t in tasks:
            wb = wmat[t].astype(BF16)
            wmat[t] = _dot(wb[:, :C], wb) + jnp.where(right, wmat[t], 0.0)
        fill()
    sol = {t: _dot(wmat[t].astype(BF16), rhs[t]) for t in tasks}
    fill()

    for ci in range(chunks_per_step):
        rows = pl.ds(ci * C, C)
        heads = [(ci, h) for h in range(GDN_HEADS)]
        s_old = {t: s_ref[t[1]] for t in heads}
        ws = {t: _dot(jnp.concatenate([sol[t][:, GDN_DV:].astype(BF16), wq[t]], axis=0),
                      s_old[t].astype(BF16)) for t in heads}
        vnb = {t: (sol[t][:, :GDN_DV] - ws[t][:C]).astype(BF16) for t in heads}
        for t in heads:
            h = t[1]
            hs = slice(h * GDN_DK, (h + 1) * GDN_DK)
            o = ws[t][C:] + _dot(attn[t], vnb[t])
            s_ref[h] = g_last[t] * s_old[t] + _dot_tn(kd[t], vnb[t])
            ms = jnp.mean(o * o, axis=-1, keepdims=True)
            on = o * lax.rsqrt(ms + EPS) * nw_ref[...]
            o_ref[rows, hs] = (on * z_ref[rows, hs].astype(F32)).astype(BF16)
    while fillers:
        fill()


def _gdn_kernel(q_ref, k_ref, v_ref, z_ref, psm_ref, psmt_ref, arow_ref,
                acol_ref, nw_ref, base_ref, wproj_ref, *rest, chunks_per_step, n_side):
    side_in = rest[:n_side]
    xo_ref = rest[n_side]
    side_out = rest[n_side + 1:2 * n_side + 1]
    s_ref, o_scr = rest[2 * n_side + 1:]
    R = CHUNK * chunks_per_step
    slot = pl.program_id(0) & 1

    @pl.when(pl.program_id(0) == 0)
    def _():
        s_ref[...] = jnp.zeros_like(s_ref)
        o_scr[1] = jnp.zeros(o_scr.shape[1:], o_scr.dtype)

    for src, dst in zip(side_in, side_out):
        dst[...] = src[...].astype(BF16)

    psm = psm_ref[...]
    g_col = -jnp.exp(arow_ref[0:1, :]) * _softplus(psm + arow_ref[1:2, :])
    beta_col = _sigmoid(psm)
    cum_col = _dot_exact_lhs(_ones_where(_chunk_tril(R)), g_col)
    g_row = -jnp.exp(acol_ref[:, 0:1]) * _softplus(psmt_ref[...] + acol_ref[:, 1:2])
    ji = lax.broadcasted_iota(jnp.int32, (R, 2 * R), 0)
    li = lax.broadcasted_iota(jnp.int32, (R, 2 * R), 1)
    dup = ((ji >> 6) == (li >> 7)) & ((ji & (CHUNK - 1)) <= (li & (CHUNK - 1)))
    cum_row = _dot_exact_rhs(g_row, _ones_where(dup))

    _gdn_chunks(q_ref, k_ref, v_ref, z_ref, nw_ref, o_scr.at[slot], s_ref, cum_col, cum_row,
                beta_col, chunks_per_step,
                _lagged_proj_pieces(o_scr, base_ref, wproj_ref, xo_ref, 8))


def _gdn(p_big, p_sm, p_smt, a_row, a_col, norm_w, x1a, w_out_b, side_weights, chunks_per_step):
    L, D = x1a.shape
    R = CHUNK * chunks_per_step
    base = (GLA_QK * 2 + GLA_V * 2) // GDN_QK
    last = L // R - 1
    cur = lambda n: jnp.minimum(n, last)
    prv = lambda n: jnp.maximum(n - 1, 0)
    kern = functools.partial(_gdn_kernel, chunks_per_step=chunks_per_step,
                             n_side=len(side_weights))
    side_specs = [_row_cast_spec(L // R, w) for w in side_weights]
    outs = pl.pallas_call(
        kern,
        grid=(L // R + 1,),
        in_specs=[
            pl.BlockSpec((R, GDN_QK), lambda n: (cur(n), base)),
            pl.BlockSpec((R, GDN_QK), lambda n: (cur(n), base + 1)),
            pl.BlockSpec((R, GDN_V), lambda n: (cur(n), base + 2)),
            pl.BlockSpec((R, GDN_V), lambda n: (cur(n), base + 3)),
            pl.BlockSpec((R, LANES), lambda n: (cur(n), 0)),
            pl.BlockSpec((LANES, R), lambda n: (0, cur(n))),
            pl.BlockSpec((2, LANES), lambda n: (0, 0)),
            pl.BlockSpec((LANES, 2), lambda n: (0, 0)),
            pl.BlockSpec((1, GDN_DV), lambda n: (0, 0)),
            pl.BlockSpec((R, D), lambda n: (prv(n), 0)),
            pl.BlockSpec((GDN_V, D), lambda n: (GLA_V // GDN_V, 0)),
        ] + side_specs,
        out_specs=[pl.BlockSpec((R, D), lambda n: (prv(n), 0))] + side_specs,
        out_shape=[jax.ShapeDtypeStruct((L, D), F32)]
        + [jax.ShapeDtypeStruct(w.shape, BF16) for w in side_weights],
        scratch_shapes=[pltpu.VMEM((GDN_HEADS, GDN_DK, GDN_DV), F32),
                        pltpu.VMEM((2, R, GDN_V), BF16)],
        compiler_params=pltpu.CompilerParams(
            dimension_semantics=("arbitrary",),
            vmem_limit_bytes=VMEM_LIMIT),
        name="gdn",
    )(p_big, p_big, p_big, p_big, p_sm, p_smt, a_row, a_col, norm_w, x1a, w_out_b,
      *side_weights)
    return outs[0], outs[1:]


def _ffn_kernel(x_ref, fnw_ref, wg_ref, wu_ref, wd_ref, onw_ref, y_ref, hf_ref, acc_ref):
    f = pl.program_id(1)

    @pl.when(f == 0)
    def _():
        x = x_ref[...]
        ms = jnp.mean(x * x, axis=-1, keepdims=True)
        hf_ref[...] = (x * lax.rsqrt(ms + EPS) * fnw_ref[...]).astype(BF16)
        acc_ref[...] = jnp.zeros_like(acc_ref)

    hf = hf_ref[...]
    g = _dot(hf, wg_ref[...])
    u = _dot(hf, wu_ref[...])
    acc_ref[...] += _dot((_silu(g) * u).astype(BF16), wd_ref[...])

    @pl.when(f == pl.num_programs(1) - 1)
    def _():
        r = x_ref[...] + acc_ref[...]
        ms = jnp.mean(r * r, axis=-1, keepdims=True)
        y_ref[...] = r * lax.rsqrt(ms + EPS) * onw_ref[...]


def _ffn(x1, ffn_norm_w, wg, wu, wd, final_norm_w, tm, tf):
    L, D = x1.shape
    F = wg.shape[1]
    return pl.pallas_call(
        _ffn_kernel,
        grid=(L // tm, F // tf),
        in_specs=[
            pl.BlockSpec((tm, D), lambda i, f: (i, 0)),
            pl.BlockSpec((1, D), lambda i, f: (0, 0)),
            pl.BlockSpec((D, tf), lambda i, f: (0, f)),
            pl.BlockSpec((D, tf), lambda i, f: (0, f)),
            pl.BlockSpec((tf, D), lambda i, f: (f, 0)),
            pl.BlockSpec((1, D), lambda i, f: (0, 0)),
        ],
        out_specs=pl.BlockSpec((tm, D), lambda i, f: (i, 0)),
        out_shape=jax.ShapeDtypeStruct((L, D), F32),
        scratch_shapes=[pltpu.VMEM((tm, D), BF16), pltpu.VMEM((tm, D), F32)],
        compiler_params=pltpu.CompilerParams(
            dimension_semantics=("arbitrary", "arbitrary"),
            vmem_limit_bytes=VMEM_LIMIT),
        name="ffn",
    )(x1, ffn_norm_w, wg, wu, wd, final_norm_w)


def _hi_lo(w):
    hi = w.astype(BF16)
    lo = (w - hi.astype(F32)).astype(BF16)
    return jnp.stack([hi, lo])


def _pick(n, candidates):
    for c in candidates:
        if n % c == 0:
            return c
    raise ValueError(f"no tile in {candidates} divides {n}")


def kernel(x, attn_norm_w, w_in, gla_gate_w2, gla_gate_b, gla_norm_w, gdn_conv_w,
           gdn_a_log, gdn_dt_bias, gdn_norm_w, w_out, ffn_norm_w, w_gate, w_up,
           w_down, final_norm_w):
    bsz, seq, d_model = x.shape
    assert bsz == 1 and seq % CHUNK == 0
    x2 = x.reshape(seq, d_model).astype(F32)

    sizes = (GLA_QK, GLA_QK, GLA_V, GLA_GATE_RANK, GLA_V,
             GDN_QK, GDN_QK, GDN_V, GDN_HEADS, GDN_HEADS, GDN_V)
    offs = np.concatenate([[0], np.cumsum(sizes)])
    assert w_in.shape == (d_model, offs[-1])
    w_in_t = w_in.astype(F32).T
    row = lambda i: w_in_t[offs[i]:offs[i + 1]]
    n_small = GLA_GATE_RANK + 2 * GDN_HEADS
    w_small_t = jnp.concatenate([row(3), row(8), row(9)], axis=0)
    w_small_t = jnp.pad(w_small_t, ((0, LANES - n_small), (0, 0)))
    w_smt = _hi_lo(w_small_t).reshape(2 * LANES, d_model)

    tm1 = _pick(seq, (1024, 512, 256, 128, 64))
    w_big_t, w_out_b = _wprep(w_in_t, w_out.astype(F32), 512)
    p_big, p_sm, p_smt = _inproj(
        x2, attn_norm_w.reshape(1, -1).astype(F32), w_big_t, w_smt,
        gdn_conv_w.astype(F32), tm1, 1024)

    cps = _pick(seq // CHUNK, (4, 2, 1))

    w2_pad = jnp.pad(gla_gate_w2.astype(F32), ((0, LANES - GLA_GATE_RANK), (0, 0)))
    x1a = _gla(p_big, p_sm, _hi_lo(w2_pad), gla_gate_b.reshape(1, -1).astype(F32),
               gla_norm_w.reshape(1, -1).astype(F32), x2, w_out_b, cps)

    a_log_pad = jnp.zeros((LANES,), F32).at[COL_A:COL_A + GDN_HEADS].set(gdn_a_log.astype(F32))
    dt_pad = jnp.zeros((LANES,), F32).at[COL_A:COL_A + GDN_HEADS].set(gdn_dt_bias.astype(F32))
    a_row = jnp.stack([a_log_pad, dt_pad])
    x1, (w_gate_b, w_up_b, w_down_b) = _gdn(
        p_big, p_sm, p_smt, a_row, a_row.T, gdn_norm_w.reshape(1, -1).astype(F32), x1a, w_out_b,
        [w.astype(F32) for w in (w_gate, w_up, w_down)], cps)

    tm4 = _pick(seq, (512, 256, 128, 64))

    y = _ffn(x1, ffn_norm_w.reshape(1, -1).astype(F32), w_gate_b, w_up_b, w_down_b,
             final_norm_w.reshape(1, -1).astype(F32), tm4, 512)
    return y.reshape(bsz, seq, d_model).astype(x.dtype)
```

```python
import functools

import numpy as np
import jax
import jax.numpy as jnp
from jax import lax
from jax.experimental import pallas as pl
from jax.experimental.pallas import tpu as pltpu

F32 = jnp.float32
BF16 = jnp.bfloat16

EPS = 1e-6
CHUNK = 64

GLA_HEADS = 4
GLA_DK = 128
GLA_DV = 256
GLA_GATE_RANK = 16
GLA_GATE_TAU = 16.0
GLA_QK = GLA_HEADS * GLA_DK
GLA_V = GLA_HEADS * GLA_DV

GDN_HEADS = 8
GDN_DK = 128
GDN_DV = 128
GDN_CONV = 4
GDN_QK = GDN_HEADS * GDN_DK
GDN_V = GDN_HEADS * GDN_DV

LANES = 128
VMEM_LIMIT = 48 * 1024 * 1024
VMEM_LIMIT_INPROJ = 56 * 1024 * 1024
N_BIG = 2 * GLA_QK + 2 * GLA_V + 2 * GDN_QK + 2 * GDN_V

COL_LR = 0
COL_A = GLA_GATE_RANK
COL_B = GLA_GATE_RANK + GDN_HEADS


def _dot(a, b):
    return jnp.dot(a, b, preferred_element_type=F32)


def _dot_nt(a, b):
    return lax.dot_general(a, b, (((1,), (1,)), ((), ())), preferred_element_type=F32)


def _dot_tn(a, b):
    return lax.dot_general(a, b, (((0,), (0,)), ((), ())), preferred_element_type=F32)


def _split3(a):
    hi = a.astype(BF16)
    r1 = a - hi.astype(F32)
    mid = r1.astype(BF16)
    lo = (r1 - mid.astype(F32)).astype(BF16)
    return hi, mid, lo


def _dot_exact_rhs(a, b_exact):
    hi, mid, lo = _split3(a)
    return _dot(hi, b_exact) + _dot(mid, b_exact) + _dot(lo, b_exact)


def _dot_exact_lhs(a_exact, b):
    hi, mid, lo = _split3(b)
    return _dot(a_exact, hi) + _dot(a_exact, mid) + _dot(a_exact, lo)


def _sigmoid(x):
    return 1.0 / (1.0 + jnp.exp(-x))


def _silu(x):
    return x * _sigmoid(x)


def _softplus(x):
    return jnp.maximum(x, 0.0) + jnp.log1p(jnp.exp(-jnp.abs(x)))


def _log_sigmoid(x):
    return -_softplus(-x)


def _wprep_kernel(wa_ref, wb_ref, wout_ref, o_ref, wout_b_ref, *, shift_steps):
    j = pl.program_id(0)
    j1, j2 = shift_steps
    wout_b_ref[...] = wout_ref[...].astype(BF16)

    def realigned(delta):
        if delta == 0:
            return wa_ref[...].astype(BF16)
        return jnp.concatenate([wa_ref[pl.ds(delta, wa_ref.shape[0] - delta), :],
                                wb_ref[pl.ds(0, delta), :]], axis=0).astype(BF16)

    @pl.when(j < j1)
    def _():
        o_ref[...] = realigned(0)

    @pl.when((j >= j1) & (j < j2))
    def _():
        o_ref[...] = realigned(GLA_GATE_RANK)

    @pl.when(j >= j2)
    def _():
        o_ref[...] = realigned(GLA_GATE_RANK + 2 * GDN_HEADS)


def _wprep(w_in_t, w_out, tn):
    D = w_in_t.shape[1]
    wout_spec = _row_cast_spec(N_BIG // tn, w_out)
    n_small = GLA_GATE_RANK + 2 * GDN_HEADS
    g1 = GLA_QK * 2 + GLA_V
    g2 = g1 + GLA_V + GDN_QK * 2 + GDN_V
    assert g1 % tn == 0 and g2 % tn == 0 and N_BIG % tn == 0 and tn % n_small == 0
    return pl.pallas_call(
        functools.partial(_wprep_kernel, shift_steps=(g1 // tn, g2 // tn)),
        grid=(N_BIG // tn,),
        in_specs=[
            pl.BlockSpec((tn, D), lambda j: (j, 0)),
            pl.BlockSpec((n_small, D), lambda j: ((tn // n_small) * (j + 1), 0)),
            wout_spec,
        ],
        out_specs=[pl.BlockSpec((tn, D), lambda j: (j, 0)), wout_spec],
        out_shape=[jax.ShapeDtypeStruct((N_BIG, D), BF16),
                   jax.ShapeDtypeStruct(w_out.shape, BF16)],
        compiler_params=pltpu.CompilerParams(
            dimension_semantics=("arbitrary",),
            vmem_limit_bytes=VMEM_LIMIT),
        name="wprep",
    )(w_in_t, w_in_t, w_out)


def _inproj_kernel(x_ref, nw_ref, wbig_ref, wsmt_ref, cw_ref, pbig_ref, psm_ref, psmt_ref,
                   h_ref, ext_scr, tail_scr, *, tile_kinds):
    i = pl.program_id(0)
    j = pl.program_id(1)
    tm = h_ref.shape[0]
    tn = wbig_ref.shape[0]
    piece = min(tm, 256)

    @pl.when(j == 0)
    def _():
        x = x_ref[...]
        ms = jnp.mean(x * x, axis=-1, keepdims=True)
        h = x * lax.rsqrt(ms + EPS) * nw_ref[...]
        hb = h.astype(BF16)
        h_ref[...] = hb
        hl = (h - hb.astype(F32)).astype(BF16)
        r = _dot_nt(wsmt_ref[...], hb)
        pt = r[:LANES] + r[LANES:] + _dot_nt(wsmt_ref[pl.ds(0, LANES), :], hl)
        psmt_ref[...] = pt
        psm_ref[...] = pt.T

    def piece_dot(r0):
        return _dot_nt(h_ref[pl.ds(r0, piece), :], wbig_ref[...])

    def steps_of(*kinds):
        pred = None
        for jj, kind in enumerate(tile_kinds):
            if kind in kinds:
                pred = (j == jj) if pred is None else (pred | (j == jj))
        return pred

    @pl.when(steps_of('plain'))
    def _():
        pbig_ref[...] = _dot_nt(h_ref[...], wbig_ref[...]).astype(BF16)

    @pl.when(steps_of('silu'))
    def _():
        for r0 in range(0, tm, piece):
            pbig_ref[pl.ds(r0, piece), :] = _silu(piece_dot(r0)).astype(BF16)

    conv_kinds = ('conv_norm_q', 'conv_norm_k', 'conv')
    first_conv = min(jj for jj, kind in enumerate(tile_kinds) if kind in conv_kinds)

    def conv_steps(normalise):
        jc = j - first_conv
        ext_scr[pl.ds(0, 8), :] = jnp.where(i == 0, 0.0, tail_scr[jc])
        scale = jnp.where(steps_of('conv_norm_q'), GDN_DK ** -0.5, 1.0) if normalise else None
        for r0 in range(0, tm, piece):
            ext_scr[pl.ds(8 + r0, piece), :] = piece_dot(r0)
            acc = cw_ref[GDN_CONV - 1:GDN_CONV, :] * ext_scr[pl.ds(8 + r0, piece), :]
            for t in range(GDN_CONV - 1):
                acc = acc + cw_ref[t:t + 1, :] * ext_scr[pl.ds(8 - (GDN_CONV - 1) + t + r0, piece), :]
            y = _silu(acc)
            if normalise:
                heads = []
                for hh in range(tn // GDN_DK):
                    yh = y[:, hh * GDN_DK:(hh + 1) * GDN_DK]
                    ssq = jnp.sum(yh * yh, axis=-1, keepdims=True)
                    heads.append(yh * (lax.rsqrt(ssq + EPS) * scale))
                y = jnp.concatenate(heads, axis=1)
            pbig_ref[pl.ds(r0, piece), :] = y.astype(BF16)
        tail_scr[jc] = ext_scr[pl.ds(tm, 8), :]

    @pl.when(steps_of('conv_norm_q', 'conv_norm_k'))
    def _():
        conv_steps(True)

    @pl.when(steps_of('conv'))
    def _():
        conv_steps(False)


def _max_split(n_steps, size, quantum):
    for nb in range(n_steps, 0, -1):
        if size % (nb * quantum) == 0:
            return nb
    raise ValueError(f"{size} is not a multiple of {quantum}")


def _inproj(x2, norm_w, w_big, w_smt, conv_w, tm, tn):
    L, D = x2.shape
    widths = (('plain', 2 * GLA_QK + GLA_V), ('silu', GLA_V), ('conv_norm_q', GDN_QK),
              ('conv_norm_k', GDN_QK), ('conv', GDN_V), ('silu', GDN_V))
    assert all(width % tn == 0 for _, width in widths) and tn % GDN_DK == 0
    tile_kinds = tuple(kind for kind, width in widths for _ in range(width // tn))
    conv_tiles = [jj for jj, kind in enumerate(tile_kinds) if kind.startswith('conv')]
    assert conv_tiles == list(range(conv_tiles[0], conv_tiles[-1] + 1))
    return pl.pallas_call(
        functools.partial(_inproj_kernel, tile_kinds=tile_kinds),
        grid=(L // tm, N_BIG // tn),
        in_specs=[
            pl.BlockSpec((tm, D), lambda i, j: (i, 0)),
            pl.BlockSpec((1, D), lambda i, j: (0, 0)),
            pl.BlockSpec((tn, D), lambda i, j: (j, 0)),
            pl.BlockSpec((2 * LANES, D), lambda i, j: (0, 0)),
            pl.BlockSpec((GDN_CONV, tn),
                         lambda i, j: (0, jnp.clip(j - conv_tiles[0], 0, len(conv_tiles) - 1))),
        ],
        out_specs=[
            pl.BlockSpec((tm, tn), lambda i, j: (i, j)),
            pl.BlockSpec((tm, LANES), lambda i, j: (i, 0)),
            pl.BlockSpec((LANES, tm), lambda i, j: (0, i)),
        ],
        out_shape=[
            jax.ShapeDtypeStruct((L, N_BIG), BF16),
            jax.ShapeDtypeStruct((L, LANES), F32),
            jax.ShapeDtypeStruct((LANES, L), F32),
        ],
        scratch_shapes=[pltpu.VMEM((tm, D), BF16),
                        pltpu.VMEM((tm + 8, tn), F32),
                        pltpu.VMEM((len(conv_tiles), 8, tn), F32)],
        compiler_params=pltpu.CompilerParams(
            dimension_semantics=("arbitrary", "arbitrary"),
            vmem_limit_bytes=VMEM_LIMIT_INPROJ),
        name="inproj",
    )(x2, norm_w, w_big, w_smt, conv_w)


def _ones_where(mask):
    return jnp.where(mask, 1.0, 0.0).astype(BF16)


def _chunk_tril(rows, upper=False):
    ti = lax.broadcasted_iota(jnp.int32, (rows, rows), 0)
    si = lax.broadcasted_iota(jnp.int32, (rows, rows), 1)
    same = (ti >> 6) == (si >> 6)
    return same & ((ti <= si) if upper else (ti >= si))


def _gla_ref_rows(c, c_scr, r0, ks, blk):
    C = CHUNK
    if 2 * blk >= 8:
        return jnp.concatenate(
            [jnp.broadcast_to(c_scr[pl.ds(r0 + g + blk - 1, 1), ks], (2 * blk, GLA_DK))
             for g in range(0, C, 2 * blk)], axis=0)
    pos = lax.broadcasted_iota(jnp.int32, (C, GLA_DK), 0) & (2 * blk - 1)
    out = c
    for off in range(2 * blk):
        if off != blk - 1:
            shift = (off - (blk - 1)) % C
            out = jnp.where(pos == off, pltpu.roll(c, shift, axis=0), out)
    return out


def _gla_scores(q, k, c, c_scr, r0, ks):
    C = CHUNK
    row = lax.broadcasted_iota(jnp.int32, (C, GLA_DK), 0)
    ti = lax.broadcasted_iota(jnp.int32, (C, C), 0)
    si = lax.broadcasted_iota(jnp.int32, (C, C), 1)
    a = jnp.where(ti == si, jnp.sum(q * k, axis=-1, keepdims=True), 0.0)
    blk = C // 2
    while blk >= 1:
        right = (row & (2 * blk - 1)) >= blk
        d = c - _gla_ref_rows(c, c_scr, r0, ks, blk)
        e = jnp.exp(jnp.where(right, d, -d))
        qt = jnp.where(right, q * e, 0.0).astype(BF16)
        kt = jnp.where(right, 0.0, k * e).astype(BF16)
        part = _dot_nt(qt, kt)
        if 2 * blk < C:
            shift = (2 * blk).bit_length() - 1
            part = jnp.where((ti >> shift) == (si >> shift), part, 0.0)
        a = a + part
        blk //= 2
    return a


def _lagged_proj_pieces(o_scr, base_ref, w_ref, out_ref, n_pieces):
    prev = 1 - (pl.program_id(0) & 1)
    width = w_ref.shape[1] // n_pieces

    def make(p):
        cols = slice(p * width, (p + 1) * width)

        def run():
            out_ref[:, cols] = base_ref[:, cols] + _dot(o_scr[prev], w_ref[:, cols])
        return run
    return [make(p) for p in range(n_pieces)]


def _gla_chunks(q_ref, v_ref, gate_ref, nw_ref, o_ref, st_ref, c_scr, k_scr, chunks_per_step,
                fillers):
    C = CHUNK
    tasks = [(ci, h) for ci in range(chunks_per_step) for h in range(GLA_HEADS)]

    def front(t):
        ci, h = t
        r0 = ci * C
        ks = slice(h * GLA_DK, (h + 1) * GLA_DK)
        q = q_ref[pl.ds(r0, C), ks].astype(F32) * (GLA_DK ** -0.5)
        k = k_scr[pl.ds(r0, C), ks]
        c = c_scr[pl.ds(r0, C), ks]
        a = _gla_scores(q, k, c, c_scr, r0, ks)
        c_last = c_scr[pl.ds(r0 + C - 1, 1), ks]
        qd = (q * jnp.exp(c)).astype(BF16)
        kd = (k * jnp.exp(c_last - c)).astype(BF16)
        return a.astype(BF16), qd, kd, jnp.exp(c_last)

    def back(t, a, qd, kd, g_last):
        ci, h = t
        rows = pl.ds(ci * C, C)
        vs = slice(h * GLA_DV, (h + 1) * GLA_DV)
        v = v_ref[rows, vs]
        st = st_ref[h]
        o = _dot(a, v) + _dot_nt(qd, st.astype(BF16))
        st_ref[h] = g_last * st + _dot_tn(v, kd)
        ms = jnp.mean(o * o, axis=-1, keepdims=True)
        on = o * lax.rsqrt(ms + EPS) * nw_ref[...]
        o_ref[rows, vs] = (on * gate_ref[rows, vs].astype(F32)).astype(BF16)

    lag = 2
    fronts = {}
    for f in fillers:
        f()
    for idx, t in enumerate(tasks):
        fronts[t] = front(t)
        if idx >= lag:
            back(tasks[idx - lag], *fronts.pop(tasks[idx - lag]))
    for t in tasks[-lag:]:
        back(t, *fronts.pop(t))


def _gla_kernel(q_ref, k_ref, v_ref, gate_ref, psm_ref, w2_ref, gb_ref, nw_ref,
                base_ref, wproj_ref, xo_ref, st_ref, c_scr, k_scr, o_scr, *, chunks_per_step):
    slot = pl.program_id(0) & 1

    @pl.when(pl.program_id(0) == 0)
    def _():
        st_ref[...] = jnp.zeros_like(st_ref)
        o_scr[1] = jnp.zeros(o_scr.shape[1:], o_scr.dtype)

    proj = _lagged_proj_pieces(o_scr, base_ref, wproj_ref, xo_ref, 8)

    a_hi, a_mid, _ = _split3(psm_ref[...])
    z = (_dot(a_hi, w2_ref[0]) + _dot(a_mid, w2_ref[0]) + _dot(a_hi, w2_ref[1])
         + gb_ref[...])
    for f in proj[:3]:
        f()
    log_a = _log_sigmoid(z) * (1.0 / GLA_GATE_TAU)
    tril = _ones_where(_chunk_tril(CHUNK * chunks_per_step))
    c_scr[...] = _dot_exact_lhs(tril, log_a)
    k_scr[...] = k_ref[...].astype(F32)

    _gla_chunks(q_ref, v_ref, gate_ref, nw_ref, o_scr.at[slot], st_ref, c_scr, k_scr,
                chunks_per_step, proj[3:])


def _row_cast_spec(n_steps, w):
    nb = _max_split(n_steps, w.shape[0], 16)
    return pl.BlockSpec((w.shape[0] // nb, w.shape[1]), lambda n: (jnp.minimum(n, nb - 1), 0))


def _gla(p_big, p_sm, w2, gate_b, norm_w, x2, w_out_b, chunks_per_step):
    L, D = x2.shape
    R = CHUNK * chunks_per_step
    last = L // R - 1
    cur = lambda n: jnp.minimum(n, last)
    prv = lambda n: jnp.maximum(n - 1, 0)
    kern = functools.partial(_gla_kernel, chunks_per_step=chunks_per_step)
    return pl.pallas_call(
        kern,
        grid=(L // R + 1,),
        in_specs=[
            pl.BlockSpec((R, GLA_QK), lambda n: (cur(n), 0)),
            pl.BlockSpec((R, GLA_QK), lambda n: (cur(n), 1)),
            pl.BlockSpec((R, GLA_V), lambda n: (cur(n), 1)),
            pl.BlockSpec((R, GLA_V), lambda n: (cur(n), 2)),
            pl.BlockSpec((R, LANES), lambda n: (cur(n), 0)),
            pl.BlockSpec((2, LANES, GLA_QK), lambda n: (0, 0, 0)),
            pl.BlockSpec((1, GLA_QK), lambda n: (0, 0)),
            pl.BlockSpec((1, GLA_DV), lambda n: (0, 0)),
            pl.BlockSpec((R, D), lambda n: (prv(n), 0)),
            pl.BlockSpec((GLA_V, D), lambda n: (0, 0)),
        ],
        out_specs=pl.BlockSpec((R, D), lambda n: (prv(n), 0)),
        out_shape=jax.ShapeDtypeStruct((L, D), F32),
        scratch_shapes=[
            pltpu.VMEM((GLA_HEADS, GLA_DV, GLA_DK), F32),
            pltpu.VMEM((R, GLA_QK), F32),
            pltpu.VMEM((R, GLA_QK), F32),
            pltpu.VMEM((2, R, GLA_V), BF16),
        ],
        compiler_params=pltpu.CompilerParams(
            dimension_semantics=("arbitrary",),
            vmem_limit_bytes=VMEM_LIMIT),
        name="gla",
    )(p_big, p_big, p_big, p_big, p_sm, w2, gate_b, norm_w, x2, w_out_b)


def _gdn_chunks(q_ref, k_ref, v_ref, z_ref, nw_ref, o_ref, s_ref, cum_col, cum_row, beta_col,
                chunks_per_step, fillers):
    C = CHUNK
    fillers = list(fillers)

    def fill():
        if fillers:
            fillers.pop(0)()
    ti = lax.broadcasted_iota(jnp.int32, (C, 2 * C), 0)
    li = lax.broadcasted_iota(jnp.int32, (C, 2 * C), 1)
    si = li & (C - 1)
    right = li >= C
    incl = ti >= si
    strict = ti > si
    eye_right = jnp.where((ti == si) & right, 1.0, 0.0)
    tasks = [(ci, h) for ci in range(chunks_per_step) for h in range(GDN_HEADS)]

    wmat, attn, rhs, wq, kd, g_last = {}, {}, {}, {}, {}, {}
    for t in tasks:
        ci, h = t
        r0 = ci * C
        hs = slice(h * GDN_DK, (h + 1) * GDN_DK)
        qnb = q_ref[pl.ds(r0, C), hs]
        knb = k_ref[pl.ds(r0, C), hs]
        kn = knb.astype(F32)
        cv = v_ref[pl.ds(r0, C), hs].astype(F32)
        cc = cum_col[r0:r0 + C, COL_A + h:COL_A + h + 1]
        cr = cum_row[COL_A + h:COL_A + h + 1, 2 * r0:2 * r0 + 2 * C]
        beta = beta_col[r0:r0 + C, COL_B + h:COL_B + h + 1]
        gamma = jnp.exp(jnp.where(incl, cc - cr, -jnp.inf))
        kb = kn * beta
        x = _dot_nt(jnp.concatenate([qnb, kb.astype(BF16)], axis=0),
                    jnp.concatenate([knb, knb], axis=0))
        attn[t] = (x[:C, :C] * gamma[:, :C]).astype(BF16)
        wmat[t] = jnp.where(strict, x[C:] * gamma, 0.0)
        e_c = jnp.exp(cc)
        c_last = cc[C - 1:C, :]
        rhs1 = jnp.concatenate([cv * beta, kb * e_c], axis=1).astype(BF16)
        rhs[t] = jnp.concatenate([rhs1, rhs1], axis=0)
        wq[t] = (qnb.astype(F32) * e_c).astype(BF16)
        kd[t] = (kn * jnp.exp(c_last - cc)).astype(BF16)
        g_last[t] = jnp.exp(c_last)

    fill()
    for t in tasks:
        n2b = wmat[t].astype(BF16)
        wmat[t] = jnp.where(right, eye_right - wmat[t], _dot(n2b[:, :C], n2b))
    fill()
    for _ in range(5):
        for t in tasks:
            wb = wmat[t].astype(BF16)
            wmat[t] = _dot(wb[:, :C], wb) + jnp.where(right, wmat[t], 0.0)
        fill()
    sol = {t: _dot(wmat[t].astype(BF16), rhs[t]) for t in tasks}
    fill()

    for ci in range(chunks_per_step):
        rows = pl.ds(ci * C, C)
        heads = [(ci, h) for h in range(GDN_HEADS)]
        s_old = {t: s_ref[t[1]] for t in heads}
        ws = {t: _dot(jnp.concatenate([sol[t][:, GDN_DV:].astype(BF16), wq[t]], axis=0),
                      s_old[t].astype(BF16)) for t in heads}
        vnb = {t: (sol[t][:, :GDN_DV] - ws[t][:C]).astype(BF16) for t in heads}
        for t in heads:
            h = t[1]
            hs = slice(h * GDN_DK, (h + 1) * GDN_DK)
            o = ws[t][C:] + _dot(attn[t], vnb[t])
            s_ref[h] = g_last[t] * s_old[t] + _dot_tn(kd[t], vnb[t])
            ms = jnp.mean(o * o, axis=-1, keepdims=True)
            on = o * lax.rsqrt(ms + EPS) * nw_ref[...]
            o_ref[rows, hs] = (on * z_ref[rows, hs].astype(F32)).astype(BF16)
    while fillers:
        fill()


def _gdn_kernel(q_ref, k_ref, v_ref, z_ref, psm_ref, psmt_ref, arow_ref,
                acol_ref, nw_ref, base_ref, wproj_ref, *rest, chunks_per_step, n_side):
    side_in = rest[:n_side]
    xo_ref = rest[n_side]
    side_out = rest[n_side + 1:2 * n_side + 1]
    s_ref, o_scr = rest[2 * n_side + 1:]
    R = CHUNK * chunks_per_step
    slot = pl.program_id(0) & 1

    @pl.when(pl.program_id(0) == 0)
    def _():
        s_ref[...] = jnp.zeros_like(s_ref)
        o_scr[1] = jnp.zeros(o_scr.shape[1:], o_scr.dtype)

    for src, dst in zip(side_in, side_out):
        dst[...] = src[...].astype(BF16)

    psm = psm_ref[...]
    g_col = -jnp.exp(arow_ref[0:1, :]) * _softplus(psm + arow_ref[1:2, :])
    beta_col = _sigmoid(psm)
    cum_col = _dot_exact_lhs(_ones_where(_chunk_tril(R)), g_col)
    g_row = -jnp.exp(acol_ref[:, 0:1]) * _softplus(psmt_ref[...] + acol_ref[:, 1:2])
    ji = lax.broadcasted_iota(jnp.int32, (R, 2 * R), 0)
    li = lax.broadcasted_iota(jnp.int32, (R, 2 * R), 1)
    dup = ((ji >> 6) == (li >> 7)) & ((ji & (CHUNK - 1)) <= (li & (CHUNK - 1)))
    cum_row = _dot_exact_rhs(g_row, _ones_where(dup))

    _gdn_chunks(q_ref, k_ref, v_ref, z_ref, nw_ref, o_scr.at[slot], s_ref, cum_col, cum_row,
                beta_col, chunks_per_step,
                _lagged_proj_pieces(o_scr, base_ref, wproj_ref, xo_ref, 8))


def _gdn(p_big, p_sm, p_smt, a_row, a_col, norm_w, x1a, w_out_b, side_weights, chunks_per_step):
    L, D = x1a.shape
    R = CHUNK * chunks_per_step
    base = (GLA_QK * 2 + GLA_V * 2) // GDN_QK
    last = L // R - 1
    cur = lambda n: jnp.minimum(n, last)
    prv = lambda n: jnp.maximum(n - 1, 0)
    kern = functools.partial(_gdn_kernel, chunks_per_step=chunks_per_step,
                             n_side=len(side_weights))
    side_specs = [_row_cast_spec(L // R, w) for w in side_weights]
    outs = pl.pallas_call(
        kern,
        grid=(L // R + 1,),
        in_specs=[
            pl.BlockSpec((R, GDN_QK), lambda n: (cur(n), base)),
            pl.BlockSpec((R, GDN_QK), lambda n: (cur(n), base + 1)),
            pl.BlockSpec((R, GDN_V), lambda n: (cur(n), base + 2)),
            pl.BlockSpec((R, GDN_V), lambda n: (cur(n), base + 3)),
            pl.BlockSpec((R, LANES), lambda n: (cur(n), 0)),
            pl.BlockSpec((LANES, R), lambda n: (0, cur(n))),
            pl.BlockSpec((2, LANES), lambda n: (0, 0)),
            pl.BlockSpec((LANES, 2), lambda n: (0, 0)),
            pl.BlockSpec((1, GDN_DV), lambda n: (0, 0)),
            pl.BlockSpec((R, D), lambda n: (prv(n), 0)),
            pl.BlockSpec((GDN_V, D), lambda n: (GLA_V // GDN_V, 0)),
        ] + side_specs,
        out_specs=[pl.BlockSpec((R, D), lambda n: (prv(n), 0))] + side_specs,
        out_shape=[jax.ShapeDtypeStruct((L, D), F32)]
        + [jax.ShapeDtypeStruct(w.shape, BF16) for w in side_weights],
        scratch_shapes=[pltpu.VMEM((GDN_HEADS, GDN_DK, GDN_DV), F32),
                        pltpu.VMEM((2, R, GDN_V), BF16)],
        compiler_params=pltpu.CompilerParams(
            dimension_semantics=("arbitrary",),
            vmem_limit_bytes=VMEM_LIMIT),
        name="gdn",
    )(p_big, p_big, p_big, p_big, p_sm, p_smt, a_row, a_col, norm_w, x1a, w_out_b,
      *side_weights)
    return outs[0], outs[1:]


def _ffn_kernel(x_ref, fnw_ref, wg_ref, wu_ref, wd_ref, onw_ref, y_ref, hf_ref, acc_ref):
    f = pl.program_id(1)

    @pl.when(f == 0)
    def _():
        x = x_ref[...]
        ms = jnp.mean(x * x, axis=-1, keepdims=True)
        hf_ref[...] = (x * lax.rsqrt(ms + EPS) * fnw_ref[...]).astype(BF16)
        acc_ref[...] = jnp.zeros_like(acc_ref)

    hf = hf_ref[...]
    g = _dot(hf, wg_ref[...])
    u = _dot(hf, wu_ref[...])
    acc_ref[...] += _dot((_silu(g) * u).astype(BF16), wd_ref[...])

    @pl.when(f == pl.num_programs(1) - 1)
    def _():
        r = x_ref[...] + acc_ref[...]
        ms = jnp.mean(r * r, axis=-1, keepdims=True)
        y_ref[...] = r * lax.rsqrt(ms + EPS) * onw_ref[...]


def _ffn(x1, ffn_norm_w, wg, wu, wd, final_norm_w, tm, tf):
    L, D = x1.shape
    F = wg.shape[1]
    return pl.pallas_call(
        _ffn_kernel,
        grid=(L // tm, F // tf),
        in_specs=[
            pl.BlockSpec((tm, D), lambda i, f: (i, 0)),
            pl.BlockSpec((1, D), lambda i, f: (0, 0)),
            pl.BlockSpec((D, tf), lambda i, f: (0, f)),
            pl.BlockSpec((D, tf), lambda i, f: (0, f)),
            pl.BlockSpec((tf, D), lambda i, f: (f, 0)),
            pl.BlockSpec((1, D), lambda i, f: (0, 0)),
        ],
        out_specs=pl.BlockSpec((tm, D), lambda i, f: (i, 0)),
        out_shape=jax.ShapeDtypeStruct((L, D), F32),
        scratch_shapes=[pltpu.VMEM((tm, D), BF16), pltpu.VMEM((tm, D), F32)],
        compiler_params=pltpu.CompilerParams(
            dimension_semantics=("arbitrary", "arbitrary"),
            vmem_limit_bytes=VMEM_LIMIT),
        name="ffn",
    )(x1, ffn_norm_w, wg, wu, wd, final_norm_w)


def _hi_lo(w):
    hi = w.astype(BF16)
    lo = (w - hi.astype(F32)).astype(BF16)
    return jnp.stack([hi, lo])


def _pick(n, candidates):
    for c in candidates:
        if n % c == 0:
            return c
    raise ValueError(f"no tile in {candidates} divides {n}")


def kernel(x, attn_norm_w, w_in, gla_gate_w2, gla_gate_b, gla_norm_w, gdn_conv_w,
           gdn_a_log, gdn_dt_bias, gdn_norm_w, w_out, ffn_norm_w, w_gate, w_up,
           w_down, final_norm_w):
    bsz, seq, d_model = x.shape
    assert bsz == 1 and seq % CHUNK == 0
    x2 = x.reshape(seq, d_model).astype(F32)

    sizes = (GLA_QK, GLA_QK, GLA_V, GLA_GATE_RANK, GLA_V,
             GDN_QK, GDN_QK, GDN_V, GDN_HEADS, GDN_HEADS, GDN_V)
    offs = np.concatenate([[0], np.cumsum(sizes)])
    assert w_in.shape == (d_model, offs[-1])
    w_in_t = w_in.astype(F32).T
    row = lambda i: w_in_t[offs[i]:offs[i + 1]]
    n_small = GLA_GATE_RANK + 2 * GDN_HEADS
    w_small_t = jnp.concatenate([row(3), row(8), row(9)], axis=0)
    w_small_t = jnp.pad(w_small_t, ((0, LANES - n_small), (0, 0)))
    w_smt = _hi_lo(w_small_t).reshape(2 * LANES, d_model)

    tm1 = _pick(seq, (1024, 512, 256, 128, 64))
    w_big_t, w_out_b = _wprep(w_in_t, w_out.astype(F32), 512)
    p_big, p_sm, p_smt = _inproj(
        x2, attn_norm_w.reshape(1, -1).astype(F32), w_big_t, w_smt,
        gdn_conv_w.astype(F32), tm1, 1024)

    cps = _pick(seq // CHUNK, (4, 2, 1))

    w2_pad = jnp.pad(gla_gate_w2.astype(F32), ((0, LANES - GLA_GATE_RANK), (0, 0)))
    x1a = _gla(p_big, p_sm, _hi_lo(w2_pad), gla_gate_b.reshape(1, -1).astype(F32),
               gla_norm_w.reshape(1, -1).astype(F32), x2, w_out_b, cps)

    a_log_pad = jnp.zeros((LANES,), F32).at[COL_A:COL_A + GDN_HEADS].set(gdn_a_log.astype(F32))
    dt_pad = jnp.zeros((LANES,), F32).at[COL_A:COL_A + GDN_HEADS].set(gdn_dt_bias.astype(F32))
    a_row = jnp.stack([a_log_pad, dt_pad])
    x1, (w_gate_b, w_up_b, w_down_b) = _gdn(
        p_big, p_sm, p_smt, a_row, a_row.T, gdn_norm_w.reshape(1, -1).astype(F32), x1a, w_out_b,
        [w.astype(F32) for w in (w_gate, w_up, w_down)], cps)

    tm4 = _pick(seq, (512, 256, 128, 64))

    y = _ffn(x1, ffn_norm_w.reshape(1, -1).astype(F32), w_gate_b, w_up_b, w_down_b,
             final_norm_w.reshape(1, -1).astype(F32), tm4, 512)
    return y.reshape(bsz, seq, d_model).astype(x.dtype)
```

```python
import functools

import numpy as np
import jax
import jax.numpy as jnp
from jax import lax
from jax.experimental import pallas as pl
from jax.experimental.pallas import tpu as pltpu

F32 = jnp.float32
BF16 = jnp.bfloat16

EPS = 1e-6
CHUNK = 64

GLA_HEADS = 4
GLA_DK = 128
GLA_DV = 256
GLA_GATE_RANK = 16
GLA_GATE_TAU = 16.0
GLA_QK = GLA_HEADS * GLA_DK
GLA_V = GLA_HEADS * GLA_DV

GDN_HEADS = 8
GDN_DK = 128
GDN_DV = 128
GDN_CONV = 4
GDN_QK = GDN_HEADS * GDN_DK
GDN_V = GDN_HEADS * GDN_DV

LANES = 128
VMEM_LIMIT = 48 * 1024 * 1024
VMEM_LIMIT_INPROJ = 56 * 1024 * 1024
N_BIG = 2 * GLA_QK + 2 * GLA_V + 2 * GDN_QK + 2 * GDN_V

COL_LR = 0
COL_A = GLA_GATE_RANK
COL_B = GLA_GATE_RANK + GDN_HEADS


def _dot(a, b):
    return jnp.dot(a, b, preferred_element_type=F32)


def _dot_nt(a, b):
    return lax.dot_general(a, b, (((1,), (1,)), ((), ())), preferred_element_type=F32)


def _dot_tn(a, b):
    return lax.dot_general(a, b, (((0,), (0,)), ((), ())), preferred_element_type=F32)


def _split3(a):
    hi = a.astype(BF16)
    r1 = a - hi.astype(F32)
    mid = r1.astype(BF16)
    lo = (r1 - mid.astype(F32)).astype(BF16)
    return hi, mid, lo


def _dot_exact_rhs(a, b_exact):
    hi, mid, lo = _split3(a)
    return _dot(hi, b_exact) + _dot(mid, b_exact) + _dot(lo, b_exact)


def _dot_exact_lhs(a_exact, b):
    hi, mid, lo = _split3(b)
    return _dot(a_exact, hi) + _dot(a_exact, mid) + _dot(a_exact, lo)


def _sigmoid(x):
    return 1.0 / (1.0 + jnp.exp(-x))


def _silu(x):
    return x * _sigmoid(x)


def _softplus(x):
    return jnp.maximum(x, 0.0) + jnp.log1p(jnp.exp(-jnp.abs(x)))


def _log_sigmoid(x):
    return -_softplus(-x)


def _wprep_kernel(wa_ref, wb_ref, wout_ref, o_ref, wout_b_ref, *, shift_steps):
    j = pl.program_id(0)
    j1, j2 = shift_steps
    wout_b_ref[...] = wout_ref[...].astype(BF16)

    def realigned(delta):
        if delta == 0:
            return wa_ref[...].astype(BF16)
        return jnp.concatenate([wa_ref[pl.ds(delta, wa_ref.shape[0] - delta), :],
                                wb_ref[pl.ds(0, delta), :]], axis=0).astype(BF16)

    @pl.when(j < j1)
    def _():
        o_ref[...] = realigned(0)

    @pl.when((j >= j1) & (j < j2))
    def _():
        o_ref[...] = realigned(GLA_GATE_RANK)

    @pl.when(j >= j2)
    def _():
        o_ref[...] = realigned(GLA_GATE_RANK + 2 * GDN_HEADS)


def _wprep(w_in_t, w_out, tn):
    D = w_in_t.shape[1]
    wout_spec = _row_cast_spec(N_BIG // tn, w_out)
    n_small = GLA_GATE_RANK + 2 * GDN_HEADS
    g1 = GLA_QK * 2 + GLA_V
    g2 = g1 + GLA_V + GDN_QK * 2 + GDN_V
    assert g1 % tn == 0 and g2 % tn == 0 and N_BIG % tn == 0 and tn % n_small == 0
    return pl.pallas_call(
        functools.partial(_wprep_kernel, shift_steps=(g1 // tn, g2 // tn)),
        grid=(N_BIG // tn,),
        in_specs=[
            pl.BlockSpec((tn, D), lambda j: (j, 0)),
            pl.BlockSpec((n_small, D), lambda j: ((tn // n_small) * (j + 1), 0)),
            wout_spec,
        ],
        out_specs=[pl.BlockSpec((tn, D), lambda j: (j, 0)), wout_spec],
        out_shape=[jax.ShapeDtypeStruct((N_BIG, D), BF16),
                   jax.ShapeDtypeStruct(w_out.shape, BF16)],
        compiler_params=pltpu.CompilerParams(
            dimension_semantics=("arbitrary",),
            vmem_limit_bytes=VMEM_LIMIT),
        name="wprep",
    )(w_in_t, w_in_t, w_out)


X_PARTS = 4


def _staggered_row_specs(tm, d, n_row_tiles, flip_steps):
    width = d // len(flip_steps)

    def spec(c, flip):
        def index(i, j):
            return jnp.minimum(i + jnp.where(j > flip, 1, 0), n_row_tiles - 1), c
        return pl.BlockSpec((tm, width), index)
    return [spec(c, flip) for c, flip in enumerate(flip_steps)]


def _inproj_kernel(*refs, tile_kinds):
    x_refs = refs[:X_PARTS]
    (nw_ref, wbig_ref, wsmt_ref, cw_ref, pbig_ref, psm_ref, psmt_ref,
     h_ref, ext_scr, tail_scr) = refs[X_PARTS:]
    i = pl.program_id(0)
    j = pl.program_id(1)
    tm = h_ref.shape[0]
    tn = wbig_ref.shape[0]
    piece = min(tm, 256)

    @pl.when(j == 0)
    def _():
        x = jnp.concatenate([r[...] for r in x_refs], axis=1)
        ms = jnp.mean(x * x, axis=-1, keepdims=True)
        h = x * lax.rsqrt(ms + EPS) * nw_ref[...]
        hb = h.astype(BF16)
        h_ref[...] = hb
        hl = (h - hb.astype(F32)).astype(BF16)
        r = _dot_nt(wsmt_ref[...], hb)
        pt = r[:LANES] + r[LANES:] + _dot_nt(wsmt_ref[pl.ds(0, LANES), :], hl)
        psmt_ref[...] = pt
        psm_ref[...] = pt.T

    def piece_dot(r0):
        return _dot_nt(h_ref[pl.ds(r0, piece), :], wbig_ref[...])

    def steps_of(*kinds):
        pred = None
        for jj, kind in enumerate(tile_kinds):
            if kind in kinds:
                pred = (j == jj) if pred is None else (pred | (j == jj))
        return pred

    @pl.when(steps_of('plain'))
    def _():
        pbig_ref[...] = _dot_nt(h_ref[...], wbig_ref[...]).astype(BF16)

    @pl.when(steps_of('silu'))
    def _():
        for r0 in range(0, tm, piece):
            pbig_ref[pl.ds(r0, piece), :] = _silu(piece_dot(r0)).astype(BF16)

    conv_kinds = ('conv_norm_q', 'conv_norm_k', 'conv')
    first_conv = min(jj for jj, kind in enumerate(tile_kinds) if kind in conv_kinds)

    def conv_steps(normalise):
        jc = j - first_conv
        ext_scr[pl.ds(0, 8), :] = jnp.where(i == 0, 0.0, tail_scr[jc])
        scale = jnp.where(steps_of('conv_norm_q'), GDN_DK ** -0.5, 1.0) if normalise else None
        for r0 in range(0, tm, piece):
            ext_scr[pl.ds(8 + r0, piece), :] = piece_dot(r0)
            acc = cw_ref[GDN_CONV - 1:GDN_CONV, :] * ext_scr[pl.ds(8 + r0, piece), :]
            for t in range(GDN_CONV - 1):
                acc = acc + cw_ref[t:t + 1, :] * ext_scr[pl.ds(8 - (GDN_CONV - 1) + t + r0, piece), :]
            y = _silu(acc)
            if normalise:
                heads = []
                for hh in range(tn // GDN_DK):
                    yh = y[:, hh * GDN_DK:(hh + 1) * GDN_DK]
                    ssq = jnp.sum(yh * yh, axis=-1, keepdims=True)
                    heads.append(yh * (lax.rsqrt(ssq + EPS) * scale))
                y = jnp.concatenate(heads, axis=1)
            pbig_ref[pl.ds(r0, piece), :] = y.astype(BF16)
        tail_scr[jc] = ext_scr[pl.ds(tm, 8), :]

    @pl.when(steps_of('conv_norm_q', 'conv_norm_k'))
    def _():
        conv_steps(True)

    @pl.when(steps_of('conv'))
    def _():
        conv_steps(False)


def _max_split(n_steps, size, quantum):
    for nb in range(n_steps, 0, -1):
        if size % (nb * quantum) == 0:
            return nb
    raise ValueError(f"{size} is not a multiple of {quantum}")


def _inproj(x2, norm_w, w_big, w_smt, conv_w, tm, tn):
    L, D = x2.shape
    widths = (('plain', 2 * GLA_QK + GLA_V), ('silu', GLA_V), ('conv_norm_q', GDN_QK),
              ('conv_norm_k', GDN_QK), ('conv', GDN_V), ('silu', GDN_V))
    assert all(width % tn == 0 for _, width in widths) and tn % GDN_DK == 0
    tile_kinds = tuple(kind for kind, width in widths for _ in range(width // tn))
    conv_tiles = [jj for jj, kind in enumerate(tile_kinds) if kind.startswith('conv')]
    assert conv_tiles == list(range(conv_tiles[0], conv_tiles[-1] + 1))
    n_steps = N_BIG // tn
    x_flips = [min(c, n_steps - 1) for c in range(X_PARTS)]
    return pl.pallas_call(
        functools.partial(_inproj_kernel, tile_kinds=tile_kinds),
        grid=(L // tm, N_BIG // tn),
        in_specs=_staggered_row_specs(tm, D, L // tm, x_flips) + [
            pl.BlockSpec((1, D), lambda i, j: (0, 0)),
            pl.BlockSpec((tn, D), lambda i, j: (j, 0)),
            pl.BlockSpec((2 * LANES, D), lambda i, j: (0, 0)),
            pl.BlockSpec((GDN_CONV, tn),
                         lambda i, j: (0, jnp.clip(j - conv_tiles[0], 0, len(conv_tiles) - 1))),
        ],
        out_specs=[
            pl.BlockSpec((tm, tn), lambda i, j: (i, j)),
            pl.BlockSpec((tm, LANES), lambda i, j: (i, 0)),
            pl.BlockSpec((LANES, tm), lambda i, j: (0, i)),
        ],
        out_shape=[
            jax.ShapeDtypeStruct((L, N_BIG), BF16),
            jax.ShapeDtypeStruct((L, LANES), F32),
            jax.ShapeDtypeStruct((LANES, L), F32),
        ],
        scratch_shapes=[pltpu.VMEM((tm, D), BF16),
                        pltpu.VMEM((tm + 8, tn), F32),
                        pltpu.VMEM((len(conv_tiles), 8, tn), F32)],
        compiler_params=pltpu.CompilerParams(
            dimension_semantics=("arbitrary", "arbitrary"),
            vmem_limit_bytes=VMEM_LIMIT_INPROJ),
        name="inproj",
    )(*([x2] * X_PARTS), norm_w, w_big, w_smt, conv_w)


def _ones_where(mask):
    return jnp.where(mask, 1.0, 0.0).astype(BF16)


def _chunk_tril(rows, upper=False):
    ti = lax.broadcasted_iota(jnp.int32, (rows, rows), 0)
    si = lax.broadcasted_iota(jnp.int32, (rows, rows), 1)
    same = (ti >> 6) == (si >> 6)
    return same & ((ti <= si) if upper else (ti >= si))


def _gla_ref_rows(c, c_scr, r0, ks, blk):
    C = CHUNK
    if 2 * blk >= 8:
        return jnp.concatenate(
            [jnp.broadcast_to(c_scr[pl.ds(r0 + g + blk - 1, 1), ks], (2 * blk, GLA_DK))
             for g in range(0, C, 2 * blk)], axis=0)
    pos = lax.broadcasted_iota(jnp.int32, (C, GLA_DK), 0) & (2 * blk - 1)
    out = c
    for off in range(2 * blk):
        if off != blk - 1:
            shift = (off - (blk - 1)) % C
            out = jnp.where(pos == off, pltpu.roll(c, shift, axis=0), out)
    return out


def _gla_scores(q, k, c, c_scr, r0, ks):
    C = CHUNK
    row = lax.broadcasted_iota(jnp.int32, (C, GLA_DK), 0)
    ti = lax.broadcasted_iota(jnp.int32, (C, C), 0)
    si = lax.broadcasted_iota(jnp.int32, (C, C), 1)
    a = jnp.where(ti == si, jnp.sum(q * k, axis=-1, keepdims=True), 0.0)
    blk = C // 2
    while blk >= 1:
        right = (row & (2 * blk - 1)) >= blk
        d = c - _gla_ref_rows(c, c_scr, r0, ks, blk)
        e = jnp.exp(jnp.where(right, d, -d))
        qt = jnp.where(right, q * e, 0.0).astype(BF16)
        kt = jnp.where(right, 0.0, k * e).astype(BF16)
        part = _dot_nt(qt, kt)
        if 2 * blk < C:
            shift = (2 * blk).bit_length() - 1
            part = jnp.where((ti >> shift) == (si >> shift), part, 0.0)
        a = a + part
        blk //= 2
    return a


def _lagged_proj_pieces(o_scr, base_ref, w_ref, out_ref, n_pieces):
    prev = 1 - (pl.program_id(0) & 1)
    width = w_ref.shape[1] // n_pieces

    def make(p):
        cols = slice(p * width, (p + 1) * width)

        def run():
            out_ref[:, cols] = base_ref[:, cols] + _dot(o_scr[prev], w_ref[:, cols])
        return run
    return [make(p) for p in range(n_pieces)]


def _gla_chunks(q_ref, v_ref, gate_ref, nw_ref, o_ref, st_ref, c_scr, k_scr, chunks_per_step,
                fillers):
    C = CHUNK
    tasks = [(ci, h) for ci in range(chunks_per_step) for h in range(GLA_HEADS)]

    def front(t):
        ci, h = t
        r0 = ci * C
        ks = slice(h * GLA_DK, (h + 1) * GLA_DK)
        q = q_ref[pl.ds(r0, C), ks].astype(F32) * (GLA_DK ** -0.5)
        k = k_scr[pl.ds(r0, C), ks]
        c = c_scr[pl.ds(r0, C), ks]
        a = _gla_scores(q, k, c, c_scr, r0, ks)
        c_last = c_scr[pl.ds(r0 + C - 1, 1), ks]
        qd = (q * jnp.exp(c)).astype(BF16)
        kd = (k * jnp.exp(c_last - c)).astype(BF16)
        return a.astype(BF16), qd, kd, jnp.exp(c_last)

    def back(t, a, qd, kd, g_last):
        ci, h = t
        rows = pl.ds(ci * C, C)
        vs = slice(h * GLA_DV, (h + 1) * GLA_DV)
        v = v_ref[rows, vs]
        st = st_ref[h]
        o = _dot(a, v) + _dot_nt(qd, st.astype(BF16))
        st_ref[h] = g_last * st + _dot_tn(v, kd)
        ms = jnp.mean(o * o, axis=-1, keepdims=True)
        on = o * lax.rsqrt(ms + EPS) * nw_ref[...]
        o_ref[rows, vs] = (on * gate_ref[rows, vs].astype(F32)).astype(BF16)

    lag = 4
    fronts = {}
    for f in fillers:
        f()
    for idx, t in enumerate(tasks):
        fronts[t] = front(t)
        if idx >= lag:
            back(tasks[idx - lag], *fronts.pop(tasks[idx - lag]))
    for t in tasks[-lag:]:
        back(t, *fronts.pop(t))


def _gla_kernel(q_ref, k_ref, v_ref, gate_ref, psm_ref, w2_ref, gb_ref, nw_ref,
                base_ref, wproj_ref, xo_ref, st_ref, c_scr, k_scr, o_scr, *, chunks_per_step):
    slot = pl.program_id(0) & 1

    @pl.when(pl.program_id(0) == 0)
    def _():
        st_ref[...] = jnp.zeros_like(st_ref)
        o_scr[1] = jnp.zeros(o_scr.shape[1:], o_scr.dtype)

    proj = _lagged_proj_pieces(o_scr, base_ref, wproj_ref, xo_ref, 8)

    a_hi, a_mid, _ = _split3(psm_ref[...])
    z = (_dot(a_hi, w2_ref[0]) + _dot(a_mid, w2_ref[0]) + _dot(a_hi, w2_ref[1])
         + gb_ref[...])
    for f in proj[:3]:
        f()
    log_a = _log_sigmoid(z) * (1.0 / GLA_GATE_TAU)
    tril = _ones_where(_chunk_tril(CHUNK * chunks_per_step))
    c_scr[...] = _dot_exact_lhs(tril, log_a)
    k_scr[...] = k_ref[...].astype(F32)

    _gla_chunks(q_ref, v_ref, gate_ref, nw_ref, o_scr.at[slot], st_ref, c_scr, k_scr,
                chunks_per_step, proj[3:])


def _row_cast_spec(n_steps, w):
    nb = _max_split(n_steps, w.shape[0], 16)
    return pl.BlockSpec((w.shape[0] // nb, w.shape[1]), lambda n: (jnp.minimum(n, nb - 1), 0))


def _gla(p_big, p_sm, w2, gate_b, norm_w, x2, w_out_b, chunks_per_step):
    L, D = x2.shape
    R = CHUNK * chunks_per_step
    last = L // R - 1
    cur = lambda n: jnp.minimum(n, last)
    prv = lambda n: jnp.maximum(n - 1, 0)
    kern = functools.partial(_gla_kernel, chunks_per_step=chunks_per_step)
    return pl.pallas_call(
        kern,
        grid=(L // R + 1,),
        in_specs=[
            pl.BlockSpec((R, GLA_QK), lambda n: (cur(n), 0)),
            pl.BlockSpec((R, GLA_QK), lambda n: (cur(n), 1)),
            pl.BlockSpec((R, GLA_V), lambda n: (cur(n), 1)),
            pl.BlockSpec((R, GLA_V), lambda n: (cur(n), 2)),
            pl.BlockSpec((R, LANES), lambda n: (cur(n), 0)),
            pl.BlockSpec((2, LANES, GLA_QK), lambda n: (0, 0, 0)),
            pl.BlockSpec((1, GLA_QK), lambda n: (0, 0)),
            pl.BlockSpec((1, GLA_DV), lambda n: (0, 0)),
            pl.BlockSpec((R, D), lambda n: (prv(n), 0)),
            pl.BlockSpec((GLA_V, D), lambda n: (0, 0)),
        ],
        out_specs=pl.BlockSpec((R, D), lambda n: (prv(n), 0)),
        out_shape=jax.ShapeDtypeStruct((L, D), F32),
        scratch_shapes=[
            pltpu.VMEM((GLA_HEADS, GLA_DV, GLA_DK), F32),
            pltpu.VMEM((R, GLA_QK), F32),
            pltpu.VMEM((R, GLA_QK), F32),
            pltpu.VMEM((2, R, GLA_V), BF16),
        ],
        compiler_params=pltpu.CompilerParams(
            dimension_semantics=("arbitrary",),
            vmem_limit_bytes=VMEM_LIMIT),
        name="gla",
    )(p_big, p_big, p_big, p_big, p_sm, w2, gate_b, norm_w, x2, w_out_b)


def _gdn_chunks(q_ref, k_ref, v_ref, z_ref, nw_ref, o_ref, s_ref, cum_col, cum_row, beta_col,
                chunks_per_step, fillers):
    C = CHUNK
    fillers = list(fillers)

    def fill():
        if fillers:
            fillers.pop(0)()
    ti = lax.broadcasted_iota(jnp.int32, (C, 2 * C), 0)
    li = lax.broadcasted_iota(jnp.int32, (C, 2 * C), 1)
    si = li & (C - 1)
    right = li >= C
    incl = ti >= si
    strict = ti > si
    eye_right = jnp.where((ti == si) & right, 1.0, 0.0)
    tasks = [(ci, h) for ci in range(chunks_per_step) for h in range(GDN_HEADS)]

    wmat, attn, rhs, wq, kd, g_last = {}, {}, {}, {}, {}, {}
    for t in tasks:
        ci, h = t
        r0 = ci * C
        hs = slice(h * GDN_DK, (h + 1) * GDN_DK)
        qnb = q_ref[pl.ds(r0, C), hs]
        knb = k_ref[pl.ds(r0, C), hs]
        kn = knb.astype(F32)
        cv = v_ref[pl.ds(r0, C), hs].astype(F32)
        cc = cum_col[r0:r0 + C, COL_A + h:COL_A + h + 1]
        cr = cum_row[COL_A + h:COL_A + h + 1, 2 * r0:2 * r0 + 2 * C]
        beta = beta_col[r0:r0 + C, COL_B + h:COL_B + h + 1]
        gamma = jnp.exp(jnp.where(incl, cc - cr, -jnp.inf))
        kb = kn * beta
        x = _dot_nt(jnp.concatenate([qnb, kb.astype(BF16)], axis=0),
                    jnp.concatenate([knb, knb], axis=0))
        attn[t] = (x[:C, :C] * gamma[:, :C]).astype(BF16)
        wmat[t] = jnp.where(strict, x[C:] * gamma, 0.0)
        e_c = jnp.exp(cc)
        c_last = cc[C - 1:C, :]
        rhs1 = jnp.concatenate([cv * beta, kb * e_c], axis=1).astype(BF16)
        rhs[t] = jnp.concatenate([rhs1, rhs1], axis=0)
        wq[t] = (qnb.astype(F32) * e_c).astype(BF16)
        kd[t] = (kn * jnp.exp(c_last - cc)).astype(BF16)
        g_last[t] = jnp.exp(c_last)

    fill()
    for t in tasks:
        n2b = wmat[t].astype(BF16)
        wmat[t] = jnp.where(right, eye_right - wmat[t], _dot(n2b[:, :C], n2b))
    fill()
    for _ in range(5):
        for t in tasks:
            wb = wmat[t].astype(BF16)
            wmat[t] = _dot(wb[:, :C], wb) + jnp.where(right, wmat[t], 0.0)
        fill()
    sol = {t: _dot(wmat[t].astype(BF16), rhs[t]) for t in tasks}
    fill()

    for ci in range(chunks_per_step):
        rows = pl.ds(ci * C, C)
        heads = [(ci, h) for h in range(GDN_HEADS)]
        s_old = {t: s_ref[t[1]] for t in heads}
        ws = {t: _dot(jnp.concatenate([sol[t][:, GDN_DV:].astype(BF16), wq[t]], axis=0),
                      s_old[t].astype(BF16)) for t in heads}
        vnb = {t: (sol[t][:, :GDN_DV] - ws[t][:C]).astype(BF16) for t in heads}
        for t in heads:
            h = t[1]
            hs = slice(h * GDN_DK, (h + 1) * GDN_DK)
            o = ws[t][C:] + _dot(attn[t], vnb[t])
            s_ref[h] = g_last[t] * s_old[t] + _dot_tn(kd[t], vnb[t])
            ms = jnp.mean(o * o, axis=-1, keepdims=True)
            on = o * lax.rsqrt(ms + EPS) * nw_ref[...]
            o_ref[rows, hs] = (on * z_ref[rows, hs].astype(F32)).astype(BF16)
    while fillers:
        fill()


def _gdn_kernel(q_ref, k_ref, v_ref, z_ref, psm_ref, psmt_ref, arow_ref,
                acol_ref, nw_ref, base_ref, wproj_ref, *rest, chunks_per_step, n_side):
    side_in = rest[:n_side]
    xo_ref = rest[n_side]
    side_out = rest[n_side + 1:2 * n_side + 1]
    s_ref, o_scr = rest[2 * n_side + 1:]
    R = CHUNK * chunks_per_step
    slot = pl.program_id(0) & 1

    @pl.when(pl.program_id(0) == 0)
    def _():
        s_ref[...] = jnp.zeros_like(s_ref)
        o_scr[1] = jnp.zeros(o_scr.shape[1:], o_scr.dtype)

    for src, dst in zip(side_in, side_out):
        dst[...] = src[...].astype(BF16)

    psm = psm_ref[...]
    g_col = -jnp.exp(arow_ref[0:1, :]) * _softplus(psm + arow_ref[1:2, :])
    beta_col = _sigmoid(psm)
    cum_col = _dot_exact_lhs(_ones_where(_chunk_tril(R)), g_col)
    g_row = -jnp.exp(acol_ref[:, 0:1]) * _softplus(psmt_ref[...] + acol_ref[:, 1:2])
    ji = lax.broadcasted_iota(jnp.int32, (R, 2 * R), 0)
    li = lax.broadcasted_iota(jnp.int32, (R, 2 * R), 1)
    dup = ((ji >> 6) == (li >> 7)) & ((ji & (CHUNK - 1)) <= (li & (CHUNK - 1)))
    cum_row = _dot_exact_rhs(g_row, _ones_where(dup))

    _gdn_chunks(q_ref, k_ref, v_ref, z_ref, nw_ref, o_scr.at[slot], s_ref, cum_col, cum_row,
                beta_col, chunks_per_step,
                _lagged_proj_pieces(o_scr, base_ref, wproj_ref, xo_ref, 8))


def _gdn(p_big, p_sm, p_smt, a_row, a_col, norm_w, x1a, w_out_b, side_weights, chunks_per_step):
    L, D = x1a.shape
    R = CHUNK * chunks_per_step
    base = (GLA_QK * 2 + GLA_V * 2) // GDN_QK
    last = L // R - 1
    cur = lambda n: jnp.minimum(n, last)
    prv = lambda n: jnp.maximum(n - 1, 0)
    kern = functools.partial(_gdn_kernel, chunks_per_step=chunks_per_step,
                             n_side=len(side_weights))
    side_specs = [_row_cast_spec(L // R, w) for w in side_weights]
    outs = pl.pallas_call(
        kern,
        grid=(L // R + 1,),
        in_specs=[
            pl.BlockSpec((R, GDN_QK), lambda n: (cur(n), base)),
            pl.BlockSpec((R, GDN_QK), lambda n: (cur(n), base + 1)),
            pl.BlockSpec((R, GDN_V), lambda n: (cur(n), base + 2)),
            pl.BlockSpec((R, GDN_V), lambda n: (cur(n), base + 3)),
            pl.BlockSpec((R, LANES), lambda n: (cur(n), 0)),
            pl.BlockSpec((LANES, R), lambda n: (0, cur(n))),
            pl.BlockSpec((2, LANES), lambda n: (0, 0)),
            pl.BlockSpec((LANES, 2), lambda n: (0, 0)),
            pl.BlockSpec((1, GDN_DV), lambda n: (0, 0)),
            pl.BlockSpec((R, D), lambda n: (prv(n), 0)),
            pl.BlockSpec((GDN_V, D), lambda n: (GLA_V // GDN_V, 0)),
        ] + side_specs,
        out_specs=[pl.BlockSpec((R, D), lambda n: (prv(n), 0))] + side_specs,
        out_shape=[jax.ShapeDtypeStruct((L, D), F32)]
        + [jax.ShapeDtypeStruct(w.shape, BF16) for w in side_weights],
        scratch_shapes=[pltpu.VMEM((GDN_HEADS, GDN_DK, GDN_DV), F32),
                        pltpu.VMEM((2, R, GDN_V), BF16)],
        compiler_params=pltpu.CompilerParams(
            dimension_semantics=("arbitrary",),
            vmem_limit_bytes=VMEM_LIMIT),
        name="gdn",
    )(p_big, p_big, p_big, p_big, p_sm, p_smt, a_row, a_col, norm_w, x1a, w_out_b,
      *side_weights)
    return outs[0], outs[1:]


def _ffn_kernel(*refs):
    x_refs = refs[:X_PARTS]
    fnw_ref, wg_ref, wu_ref, wd_ref, onw_ref, y_ref, hf_ref, acc_ref = refs[X_PARTS:]
    f = pl.program_id(1)

    @pl.when(f == 0)
    def _():
        x = jnp.concatenate([r[...] for r in x_refs], axis=1)
        ms = jnp.mean(x * x, axis=-1, keepdims=True)
        hf_ref[...] = (x * lax.rsqrt(ms + EPS) * fnw_ref[...]).astype(BF16)
        acc_ref[...] = x

    hf = hf_ref[...]
    g = _dot(hf, wg_ref[...])
    u = _dot(hf, wu_ref[...])
    acc_ref[...] += _dot((_silu(g) * u).astype(BF16), wd_ref[...])

    @pl.when(f == pl.num_programs(1) - 1)
    def _():
        r = acc_ref[...]
        ms = jnp.mean(r * r, axis=-1, keepdims=True)
        y_ref[...] = r * lax.rsqrt(ms + EPS) * onw_ref[...]


def _ffn(x1, ffn_norm_w, wg, wu, wd, final_norm_w, tm, tf):
    L, D = x1.shape
    F = wg.shape[1]
    n_steps = F // tf
    x_flips = [min(1 + 2 * c, n_steps - 1) for c in range(X_PARTS)]
    return pl.pallas_call(
        _ffn_kernel,
        grid=(L // tm, n_steps),
        in_specs=_staggered_row_specs(tm, D, L // tm, x_flips) + [
            pl.BlockSpec((1, D), lambda i, f: (0, 0)),
            pl.BlockSpec((D, tf), lambda i, f: (0, f)),
            pl.BlockSpec((D, tf), lambda i, f: (0, f)),
            pl.BlockSpec((tf, D), lambda i, f: (f, 0)),
            pl.BlockSpec((1, D), lambda i, f: (0, 0)),
        ],
        out_specs=pl.BlockSpec((tm, D), lambda i, f: (i, 0)),
        out_shape=jax.ShapeDtypeStruct((L, D), F32),
        scratch_shapes=[pltpu.VMEM((tm, D), BF16), pltpu.VMEM((tm, D), F32)],
        compiler_params=pltpu.CompilerParams(
            dimension_semantics=("arbitrary", "arbitrary"),
            vmem_limit_bytes=VMEM_LIMIT),
        name="ffn",
    )(*([x1] * X_PARTS), ffn_norm_w, wg, wu, wd, final_norm_w)


def _hi_lo(w):
    hi = w.astype(BF16)
    lo = (w - hi.astype(F32)).astype(BF16)
    return jnp.stack([hi, lo])


def _pick(n, candidates):
    for c in candidates:
        if n % c == 0:
            return c
    raise ValueError(f"no tile in {candidates} divides {n}")


def kernel(x, attn_norm_w, w_in, gla_gate_w2, gla_gate_b, gla_norm_w, gdn_conv_w,
           gdn_a_log, gdn_dt_bias, gdn_norm_w, w_out, ffn_norm_w, w_gate, w_up,
           w_down, final_norm_w):
    bsz, seq, d_model = x.shape
    assert bsz == 1 and seq % CHUNK == 0
    x2 = x.reshape(seq, d_model).astype(F32)

    sizes = (GLA_QK, GLA_QK, GLA_V, GLA_GATE_RANK, GLA_V,
             GDN_QK, GDN_QK, GDN_V, GDN_HEADS, GDN_HEADS, GDN_V)
    offs = np.concatenate([[0], np.cumsum(sizes)])
    assert w_in.shape == (d_model, offs[-1])
    w_in_t = w_in.astype(F32).T
    row = lambda i: w_in_t[offs[i]:offs[i + 1]]
    n_small = GLA_GATE_RANK + 2 * GDN_HEADS
    w_small_t = jnp.concatenate([row(3), row(8), row(9)], axis=0)
    w_small_t = jnp.pad(w_small_t, ((0, LANES - n_small), (0, 0)))
    w_smt = _hi_lo(w_small_t).reshape(2 * LANES, d_model)

    tm1 = _pick(seq, (1024, 512, 256, 128, 64))
    w_big_t, w_out_b = _wprep(w_in_t, w_out.astype(F32), 512)
    p_big, p_sm, p_smt = _inproj(
        x2, attn_norm_w.reshape(1, -1).astype(F32), w_big_t, w_smt,
        gdn_conv_w.astype(F32), tm1, 1024)

    cps = _pick(seq // CHUNK, (4, 2, 1))

    w2_pad = jnp.pad(gla_gate_w2.astype(F32), ((0, LANES - GLA_GATE_RANK), (0, 0)))
    x1a = _gla(p_big, p_sm, _hi_lo(w2_pad), gla_gate_b.reshape(1, -1).astype(F32),
               gla_norm_w.reshape(1, -1).astype(F32), x2, w_out_b, cps)

    a_log_pad = jnp.zeros((LANES,), F32).at[COL_A:COL_A + GDN_HEADS].set(gdn_a_log.astype(F32))
    dt_pad = jnp.zeros((LANES,), F32).at[COL_A:COL_A + GDN_HEADS].set(gdn_dt_bias.astype(F32))
    a_row = jnp.stack([a_log_pad, dt_pad])
    x1, (w_gate_b, w_up_b, w_down_b) = _gdn(
        p_big, p_sm, p_smt, a_row, a_row.T, gdn_norm_w.reshape(1, -1).astype(F32), x1a, w_out_b,
        [w.astype(F32) for w in (w_gate, w_up, w_down)], cps)

    tm4 = _pick(seq, (512, 256, 128, 64))

    y = _ffn(x1, ffn_norm_w.reshape(1, -1).astype(F32), w_gate_b, w_up_b, w_down_b,
             final_norm_w.reshape(1, -1).astype(F32), tm4, 512)
    return y.reshape(bsz, seq, d_model).astype(x.dtype)
```

```python
import functools

import numpy as np
import jax
import jax.numpy as jnp
from jax import lax
from jax.experimental import pallas as pl
from jax.experimental.pallas import tpu as pltpu

F32 = jnp.float32
BF16 = jnp.bfloat16

EPS = 1e-6
CHUNK = 64

GLA_HEADS = 4
GLA_DK = 128
GLA_DV = 256
GLA_GATE_RANK = 16
GLA_GATE_TAU = 16.0
GLA_QK = GLA_HEADS * GLA_DK
GLA_V = GLA_HEADS * GLA_DV

GDN_HEADS = 8
GDN_DK = 128
GDN_DV = 128
GDN_CONV = 4
GDN_QK = GDN_HEADS * GDN_DK
GDN_V = GDN_HEADS * GDN_DV

LANES = 128
VMEM_LIMIT = 48 * 1024 * 1024
VMEM_LIMIT_INPROJ = 56 * 1024 * 1024
N_BIG = 2 * GLA_QK + 2 * GLA_V + 2 * GDN_QK + 2 * GDN_V

COL_A = GLA_GATE_RANK
COL_B = GLA_GATE_RANK + GDN_HEADS


def _dot(a, b):
    return jnp.dot(a, b, preferred_element_type=F32)


def _dot_nt(a, b):
    return lax.dot_general(a, b, (((1,), (1,)), ((), ())), preferred_element_type=F32)


def _dot_tn(a, b):
    return lax.dot_general(a, b, (((0,), (0,)), ((), ())), preferred_element_type=F32)


def _split3(a):
    hi = a.astype(BF16)
    r1 = a - hi.astype(F32)
    mid = r1.astype(BF16)
    lo = (r1 - mid.astype(F32)).astype(BF16)
    return hi, mid, lo


def _dot_exact_rhs(a, b_exact):
    hi, mid, lo = _split3(a)
    return _dot(hi, b_exact) + _dot(mid, b_exact) + _dot(lo, b_exact)


def _dot_exact_lhs(a_exact, b):
    hi, mid, lo = _split3(b)
    return _dot(a_exact, hi) + _dot(a_exact, mid) + _dot(a_exact, lo)


def _sigmoid(x):
    return 1.0 / (1.0 + jnp.exp(-x))


def _silu(x):
    return x * _sigmoid(x)


def _softplus(x):
    return jnp.maximum(x, 0.0) + jnp.log1p(jnp.exp(-jnp.abs(x)))


def _log_sigmoid(x):
    return -_softplus(-x)


def _wprep_kernel(wa_ref, wb_ref, wout_ref, o_ref, wout_b_ref, *, shift_steps):
    j = pl.program_id(0)
    j1, j2 = shift_steps
    wout_b_ref[...] = wout_ref[...].astype(BF16)

    def realigned(delta):
        if delta == 0:
            return wa_ref[...].astype(BF16)
        return jnp.concatenate([wa_ref[pl.ds(delta, wa_ref.shape[0] - delta), :],
                                wb_ref[pl.ds(0, delta), :]], axis=0).astype(BF16)

    @pl.when(j < j1)
    def _():
        o_ref[...] = realigned(0)

    @pl.when((j >= j1) & (j < j2))
    def _():
        o_ref[...] = realigned(GLA_GATE_RANK)

    @pl.when(j >= j2)
    def _():
        o_ref[...] = realigned(GLA_GATE_RANK + 2 * GDN_HEADS)


def _wprep(w_in_t, w_out, tn):
    D = w_in_t.shape[1]
    wout_spec = _row_cast_spec(N_BIG // tn, w_out)
    n_small = GLA_GATE_RANK + 2 * GDN_HEADS
    g1 = GLA_QK * 2 + GLA_V
    g2 = g1 + GLA_V + GDN_QK * 2 + GDN_V
    assert g1 % tn == 0 and g2 % tn == 0 and N_BIG % tn == 0 and tn % n_small == 0
    return pl.pallas_call(
        functools.partial(_wprep_kernel, shift_steps=(g1 // tn, g2 // tn)),
        grid=(N_BIG // tn,),
        in_specs=[
            pl.BlockSpec((tn, D), lambda j: (j, 0)),
            pl.BlockSpec((n_small, D), lambda j: ((tn // n_small) * (j + 1), 0)),
            wout_spec,
        ],
        out_specs=[pl.BlockSpec((tn, D), lambda j: (j, 0)), wout_spec],
        out_shape=[jax.ShapeDtypeStruct((N_BIG, D), BF16),
                   jax.ShapeDtypeStruct(w_out.shape, BF16)],
        compiler_params=pltpu.CompilerParams(
            dimension_semantics=("arbitrary",),
            vmem_limit_bytes=VMEM_LIMIT),
        name="wprep",
    )(w_in_t, w_in_t, w_out)


def _inproj_kernel(x_ref, nw_ref, wbig_ref, wsmt_ref, cw_ref, pbig_ref, psm_ref, psmt_ref,
                   h_ref, ext_scr, tail_scr, *, tile_kinds):
    i = pl.program_id(0)
    j = pl.program_id(1)
    tm = h_ref.shape[0]
    tn = wbig_ref.shape[0]
    piece = min(tm, 256)

    @pl.when(j == 0)
    def _():
        x = x_ref[...]
        ms = jnp.mean(x * x, axis=-1, keepdims=True)
        h = x * lax.rsqrt(ms + EPS) * nw_ref[...]
        hb = h.astype(BF16)
        h_ref[...] = hb
        hl = (h - hb.astype(F32)).astype(BF16)
        r = _dot_nt(wsmt_ref[...], hb)
        pt = r[:LANES] + r[LANES:] + _dot_nt(wsmt_ref[pl.ds(0, LANES), :], hl)
        psmt_ref[...] = pt
        psm_ref[...] = pt.T

    def piece_dot(r0):
        return _dot_nt(h_ref[pl.ds(r0, piece), :], wbig_ref[...])

    def steps_of(*kinds):
        pred = None
        for jj, kind in enumerate(tile_kinds):
            if kind in kinds:
                pred = (j == jj) if pred is None else (pred | (j == jj))
        return pred

    @pl.when(steps_of('plain'))
    def _():
        pbig_ref[...] = _dot_nt(h_ref[...], wbig_ref[...]).astype(BF16)

    @pl.when(steps_of('silu'))
    def _():
        for r0 in range(0, tm, piece):
            pbig_ref[pl.ds(r0, piece), :] = _silu(piece_dot(r0)).astype(BF16)

    conv_kinds = ('conv_norm_q', 'conv_norm_k', 'conv')
    first_conv = min(jj for jj, kind in enumerate(tile_kinds) if kind in conv_kinds)

    def conv_steps(normalise):
        jc = j - first_conv
        ext_scr[pl.ds(0, 8), :] = jnp.where(i == 0, 0.0, tail_scr[jc])
        scale = jnp.where(steps_of('conv_norm_q'), GDN_DK ** -0.5, 1.0) if normalise else None
        for r0 in range(0, tm, piece):
            ext_scr[pl.ds(8 + r0, piece), :] = piece_dot(r0)
            acc = cw_ref[GDN_CONV - 1:GDN_CONV, :] * ext_scr[pl.ds(8 + r0, piece), :]
            for t in range(GDN_CONV - 1):
                acc = acc + cw_ref[t:t + 1, :] * ext_scr[pl.ds(8 - (GDN_CONV - 1) + t + r0, piece), :]
            y = _silu(acc)
            if normalise:
                heads = []
                for hh in range(tn // GDN_DK):
                    yh = y[:, hh * GDN_DK:(hh + 1) * GDN_DK]
                    ssq = jnp.sum(yh * yh, axis=-1, keepdims=True)
                    heads.append(yh * (lax.rsqrt(ssq + EPS) * scale))
                y = jnp.concatenate(heads, axis=1)
            pbig_ref[pl.ds(r0, piece), :] = y.astype(BF16)
        tail_scr[jc] = ext_scr[pl.ds(tm, 8), :]

    @pl.when(steps_of('conv_norm_q', 'conv_norm_k'))
    def _():
        conv_steps(True)

    @pl.when(steps_of('conv'))
    def _():
        conv_steps(False)


def _max_split(n_steps, size, quantum):
    for nb in range(n_steps, 0, -1):
        if size % (nb * quantum) == 0:
            return nb
    raise ValueError(f"{size} is not a multiple of {quantum}")


def _inproj(x2, norm_w, w_big, w_smt, conv_w, tm, tn):
    L, D = x2.shape
    widths = (('plain', 2 * GLA_QK + GLA_V), ('silu', GLA_V), ('conv_norm_q', GDN_QK),
              ('conv_norm_k', GDN_QK), ('conv', GDN_V), ('silu', GDN_V))
    assert all(width % tn == 0 for _, width in widths) and tn % GDN_DK == 0
    tile_kinds = tuple(kind for kind, width in widths for _ in range(width // tn))
    conv_tiles = [jj for jj, kind in enumerate(tile_kinds) if kind.startswith('conv')]
    assert conv_tiles == list(range(conv_tiles[0], conv_tiles[-1] + 1))
    return pl.pallas_call(
        functools.partial(_inproj_kernel, tile_kinds=tile_kinds),
        grid=(L // tm, N_BIG // tn),
        in_specs=[
            pl.BlockSpec((tm, D), lambda i, j: (i, 0)),
            pl.BlockSpec((1, D), lambda i, j: (0, 0)),
            pl.BlockSpec((tn, D), lambda i, j: (j, 0)),
            pl.BlockSpec((2 * LANES, D), lambda i, j: (0, 0)),
            pl.BlockSpec((GDN_CONV, tn),
                         lambda i, j: (0, jnp.clip(j - conv_tiles[0], 0, len(conv_tiles) - 1))),
        ],
        out_specs=[
            pl.BlockSpec((tm, tn), lambda i, j: (i, j)),
            pl.BlockSpec((tm, LANES), lambda i, j: (i, 0)),
            pl.BlockSpec((LANES, tm), lambda i, j: (0, i)),
        ],
        out_shape=[
            jax.ShapeDtypeStruct((L, N_BIG), BF16),
            jax.ShapeDtypeStruct((L, LANES), F32),
            jax.ShapeDtypeStruct((LANES, L), F32),
        ],
        scratch_shapes=[pltpu.VMEM((tm, D), BF16),
                        pltpu.VMEM((tm + 8, tn), F32),
                        pltpu.VMEM((len(conv_tiles), 8, tn), F32)],
        compiler_params=pltpu.CompilerParams(
            dimension_semantics=("arbitrary", "arbitrary"),
            vmem_limit_bytes=VMEM_LIMIT_INPROJ),
        name="inproj",
    )(x2, norm_w, w_big, w_smt, conv_w)


def _ones_where(mask):
    return jnp.where(mask, 1.0, 0.0).astype(BF16)


def _chunk_tril(rows):
    ti = lax.broadcasted_iota(jnp.int32, (rows, rows), 0)
    si = lax.broadcasted_iota(jnp.int32, (rows, rows), 1)
    return ((ti >> 6) == (si >> 6)) & (ti >= si)


def _gla_ref_rows(c, c_scr, r0, ks, blk):
    C = CHUNK
    if 2 * blk >= 8:
        return jnp.concatenate(
            [jnp.broadcast_to(c_scr[pl.ds(r0 + g + blk - 1, 1), ks], (2 * blk, GLA_DK))
             for g in range(0, C, 2 * blk)], axis=0)
    pos = lax.broadcasted_iota(jnp.int32, (C, GLA_DK), 0) & (2 * blk - 1)
    out = c
    for off in range(2 * blk):
        if off != blk - 1:
            shift = (off - (blk - 1)) % C
            out = jnp.where(pos == off, pltpu.roll(c, shift, axis=0), out)
    return out


def _gla_scores(q, k, c, c_scr, r0, ks):
    C = CHUNK
    row = lax.broadcasted_iota(jnp.int32, (C, GLA_DK), 0)
    ti = lax.broadcasted_iota(jnp.int32, (C, C), 0)
    si = lax.broadcasted_iota(jnp.int32, (C, C), 1)
    a = jnp.where(ti == si, jnp.sum(q * k, axis=-1, keepdims=True), 0.0)
    blk = C // 2
    while blk >= 1:
        right = (row & (2 * blk - 1)) >= blk
        d = c - _gla_ref_rows(c, c_scr, r0, ks, blk)
        e = jnp.exp(jnp.where(right, d, -d))
        qt = jnp.where(right, q * e, 0.0).astype(BF16)
        kt = jnp.where(right, 0.0, k * e).astype(BF16)
        part = _dot_nt(qt, kt)
        if 2 * blk < C:
            shift = (2 * blk).bit_length() - 1
            part = jnp.where((ti >> shift) == (si >> shift), part, 0.0)
        a = a + part
        blk //= 2
    return a


def _lagged_proj_pieces(o_scr, base_ref, w_ref, out_ref, n_pieces):
    prev = 1 - (pl.program_id(0) & 1)
    width = w_ref.shape[1] // n_pieces

    def make(p):
        cols = slice(p * width, (p + 1) * width)

        def run():
            out_ref[:, cols] = base_ref[:, cols] + _dot(o_scr[prev], w_ref[:, cols])
        return run
    return [make(p) for p in range(n_pieces)]


def _gla_chunks(q_ref, v_ref, gate_ref, nw_ref, o_ref, st_ref, c_scr, k_scr, chunks_per_step,
                fillers):
    C = CHUNK
    tasks = [(ci, h) for ci in range(chunks_per_step) for h in range(GLA_HEADS)]

    def front(t):
        ci, h = t
        r0 = ci * C
        ks = slice(h * GLA_DK, (h + 1) * GLA_DK)
        q = q_ref[pl.ds(r0, C), ks].astype(F32) * (GLA_DK ** -0.5)
        k = k_scr[pl.ds(r0, C), ks]
        c = c_scr[pl.ds(r0, C), ks]
        a = _gla_scores(q, k, c, c_scr, r0, ks)
        c_last = c_scr[pl.ds(r0 + C - 1, 1), ks]
        qd = (q * jnp.exp(c)).astype(BF16)
        kd = (k * jnp.exp(c_last - c)).astype(BF16)
        return a.astype(BF16), qd, kd, jnp.exp(c_last)

    def back(t, a, qd, kd, g_last):
        ci, h = t
        rows = pl.ds(ci * C, C)
        vs = slice(h * GLA_DV, (h + 1) * GLA_DV)
        v = v_ref[rows, vs]
        st = st_ref[h]
        o = _dot(a, v) + _dot_nt(qd, st.astype(BF16))
        st_ref[h] = g_last * st + _dot_tn(v, kd)
        ms = jnp.mean(o * o, axis=-1, keepdims=True)
        on = o * lax.rsqrt(ms + EPS) * nw_ref[...]
        o_ref[rows, vs] = (on * gate_ref[rows, vs].astype(F32)).astype(BF16)

    lag = 4
    fronts = {}
    for f in fillers:
        f()
    for idx, t in enumerate(tasks):
        fronts[t] = front(t)
        if idx >= lag:
            back(tasks[idx - lag], *fronts.pop(tasks[idx - lag]))
    for t in tasks[-lag:]:
        back(t, *fronts.pop(t))


def _gla_kernel(q_ref, k_ref, v_ref, gate_ref, psm_ref, w2_ref, gb_ref, nw_ref,
                base_ref, wproj_ref, xo_ref, st_ref, c_scr, k_scr, o_scr, *, chunks_per_step):
    slot = pl.program_id(0) & 1

    @pl.when(pl.program_id(0) == 0)
    def _():
        st_ref[...] = jnp.zeros_like(st_ref)
        o_scr[1] = jnp.zeros(o_scr.shape[1:], o_scr.dtype)

    proj = _lagged_proj_pieces(o_scr, base_ref, wproj_ref, xo_ref, 8)

    a_hi, a_mid, _ = _split3(psm_ref[...])
    z = (_dot(a_hi, w2_ref[0]) + _dot(a_mid, w2_ref[0]) + _dot(a_hi, w2_ref[1])
         + gb_ref[...])
    for f in proj[:3]:
        f()
    log_a = _log_sigmoid(z) * (1.0 / GLA_GATE_TAU)
    tril = _ones_where(_chunk_tril(CHUNK * chunks_per_step))
    c_scr[...] = _dot_exact_lhs(tril, log_a)
    k_scr[...] = k_ref[...].astype(F32)

    _gla_chunks(q_ref, v_ref, gate_ref, nw_ref, o_scr.at[slot], st_ref, c_scr, k_scr,
                chunks_per_step, proj[3:])


def _row_cast_spec(n_steps, w):
    nb = _max_split(n_steps, w.shape[0], 16)
    return pl.BlockSpec((w.shape[0] // nb, w.shape[1]), lambda n: (jnp.minimum(n, nb - 1), 0))


def _gla(p_big, p_sm, w2, gate_b, norm_w, x2, w_out_b, chunks_per_step):
    L, D = x2.shape
    R = CHUNK * chunks_per_step
    last = L // R - 1
    cur = lambda n: jnp.minimum(n, last)
    prv = lambda n: jnp.maximum(n - 1, 0)
    kern = functools.partial(_gla_kernel, chunks_per_step=chunks_per_step)
    return pl.pallas_call(
        kern,
        grid=(L // R + 1,),
        in_specs=[
            pl.BlockSpec((R, GLA_QK), lambda n: (cur(n), 0)),
            pl.BlockSpec((R, GLA_QK), lambda n: (cur(n), 1)),
            pl.BlockSpec((R, GLA_V), lambda n: (cur(n), 1)),
            pl.BlockSpec((R, GLA_V), lambda n: (cur(n), 2)),
            pl.BlockSpec((R, LANES), lambda n: (cur(n), 0)),
            pl.BlockSpec((2, LANES, GLA_QK), lambda n: (0, 0, 0)),
            pl.BlockSpec((1, GLA_QK), lambda n: (0, 0)),
            pl.BlockSpec((1, GLA_DV), lambda n: (0, 0)),
            pl.BlockSpec((R, D), lambda n: (prv(n), 0)),
            pl.BlockSpec((GLA_V, D), lambda n: (0, 0)),
        ],
        out_specs=pl.BlockSpec((R, D), lambda n: (prv(n), 0)),
        out_shape=jax.ShapeDtypeStruct((L, D), F32),
        scratch_shapes=[
            pltpu.VMEM((GLA_HEADS, GLA_DV, GLA_DK), F32),
            pltpu.VMEM((R, GLA_QK), F32),
            pltpu.VMEM((R, GLA_QK), F32),
            pltpu.VMEM((2, R, GLA_V), BF16),
        ],
        compiler_params=pltpu.CompilerParams(
            dimension_semantics=("arbitrary",),
            vmem_limit_bytes=VMEM_LIMIT),
        name="gla",
    )(p_big, p_big, p_big, p_big, p_sm, w2, gate_b, norm_w, x2, w_out_b)


def _gdn_chunks(q_ref, k_ref, v_ref, z_ref, nw_ref, o_ref, s_ref, cum_col, cum_row, beta_col,
                chunks_per_step, fillers):
    C = CHUNK
    fillers = list(fillers)

    def fill():
        if fillers:
            fillers.pop(0)()
    ti = lax.broadcasted_iota(jnp.int32, (C, 2 * C), 0)
    li = lax.broadcasted_iota(jnp.int32, (C, 2 * C), 1)
    si = li & (C - 1)
    right = li >= C
    incl = ti >= si
    strict = ti > si
    eye_right = jnp.where((ti == si) & right, 1.0, 0.0)
    tasks = [(ci, h) for ci in range(chunks_per_step) for h in range(GDN_HEADS)]

    wmat, attn, rhs, wq, kd, g_last = {}, {}, {}, {}, {}, {}
    for t in tasks:
        ci, h = t
        r0 = ci * C
        hs = slice(h * GDN_DK, (h + 1) * GDN_DK)
        qnb = q_ref[pl.ds(r0, C), hs]
        knb = k_ref[pl.ds(r0, C), hs]
        kn = knb.astype(F32)
        cv = v_ref[pl.ds(r0, C), hs].astype(F32)
        cc = cum_col[r0:r0 + C, COL_A + h:COL_A + h + 1]
        cr = cum_row[COL_A + h:COL_A + h + 1, 2 * r0:2 * r0 + 2 * C]
        beta = beta_col[r0:r0 + C, COL_B + h:COL_B + h + 1]
        gamma = jnp.exp(jnp.where(incl, cc - cr, -jnp.inf))
        kb = kn * beta
        x = _dot_nt(jnp.concatenate([qnb, kb.astype(BF16)], axis=0),
                    jnp.concatenate([knb, knb], axis=0))
        attn[t] = (x[:C, :C] * gamma[:, :C]).astype(BF16)
        wmat[t] = jnp.where(strict, x[C:] * gamma, 0.0)
        e_c = jnp.exp(cc)
        c_last = cc[C - 1:C, :]
        rhs1 = jnp.concatenate([cv * beta, kb * e_c], axis=1).astype(BF16)
        rhs[t] = jnp.concatenate([rhs1, rhs1], axis=0)
        wq[t] = (qnb.astype(F32) * e_c).astype(BF16)
        kd[t] = (kn * jnp.exp(c_last - cc)).astype(BF16)
        g_last[t] = jnp.exp(c_last)

    fill()
    for t in tasks:
        n2b = wmat[t].astype(BF16)
        wmat[t] = jnp.where(right, eye_right - wmat[t], _dot(n2b[:, :C], n2b))
    fill()
    for _ in range(5):
        for t in tasks:
            wb = wmat[t].astype(BF16)
            wmat[t] = _dot(wb[:, :C], wb) + jnp.where(right, wmat[t], 0.0)
        fill()
    sol = {t: _dot(wmat[t].astype(BF16), rhs[t]) for t in tasks}
    fill()

    for ci in range(chunks_per_step):
        rows = pl.ds(ci * C, C)
        heads = [(ci, h) for h in range(GDN_HEADS)]
        s_old = {t: s_ref[t[1]] for t in heads}
        ws = {t: _dot(jnp.concatenate([sol[t][:, GDN_DV:].astype(BF16), wq[t]], axis=0),
                      s_old[t].astype(BF16)) for t in heads}
        vnb = {t: (sol[t][:, :GDN_DV] - ws[t][:C]).astype(BF16) for t in heads}
        for t in heads:
            h = t[1]
            hs = slice(h * GDN_DK, (h + 1) * GDN_DK)
            o = ws[t][C:] + _dot(attn[t], vnb[t])
            s_ref[h] = g_last[t] * s_old[t] + _dot_tn(kd[t], vnb[t])
            ms = jnp.mean(o * o, axis=-1, keepdims=True)
            on = o * lax.rsqrt(ms + EPS) * nw_ref[...]
            o_ref[rows, hs] = (on * z_ref[rows, hs].astype(F32)).astype(BF16)
    while fillers:
        fill()


def _gdn_kernel(q_ref, k_ref, v_ref, z_ref, psm_ref, psmt_ref, arow_ref,
                acol_ref, nw_ref, base_ref, wproj_ref, *rest, chunks_per_step, n_side):
    side_in = rest[:n_side]
    xo_ref = rest[n_side]
    side_out = rest[n_side + 1:2 * n_side + 1]
    s_ref, o_scr = rest[2 * n_side + 1:]
    R = CHUNK * chunks_per_step
    slot = pl.program_id(0) & 1

    @pl.when(pl.program_id(0) == 0)
    def _():
        s_ref[...] = jnp.zeros_like(s_ref)
        o_scr[1] = jnp.zeros(o_scr.shape[1:], o_scr.dtype)

    for src, dst in zip(side_in, side_out):
        dst[...] = src[...].astype(BF16)

    psm = psm_ref[...]
    g_col = -jnp.exp(arow_ref[0:1, :]) * _softplus(psm + arow_ref[1:2, :])
    beta_col = _sigmoid(psm)
    cum_col = _dot_exact_lhs(_ones_where(_chunk_tril(R)), g_col)
    g_row = -jnp.exp(acol_ref[:, 0:1]) * _softplus(psmt_ref[...] + acol_ref[:, 1:2])
    ji = lax.broadcasted_iota(jnp.int32, (R, 2 * R), 0)
    li = lax.broadcasted_iota(jnp.int32, (R, 2 * R), 1)
    dup = ((ji >> 6) == (li >> 7)) & ((ji & (CHUNK - 1)) <= (li & (CHUNK - 1)))
    cum_row = _dot_exact_rhs(g_row, _ones_where(dup))

    _gdn_chunks(q_ref, k_ref, v_ref, z_ref, nw_ref, o_scr.at[slot], s_ref, cum_col, cum_row,
                beta_col, chunks_per_step,
                _lagged_proj_pieces(o_scr, base_ref, wproj_ref, xo_ref, 8))


def _gdn(p_big, p_sm, p_smt, a_row, a_col, norm_w, x1a, w_out_b, side_weights, chunks_per_step):
    L, D = x1a.shape
    R = CHUNK * chunks_per_step
    base = (GLA_QK * 2 + GLA_V * 2) // GDN_QK
    last = L // R - 1
    cur = lambda n: jnp.minimum(n, last)
    prv = lambda n: jnp.maximum(n - 1, 0)
    kern = functools.partial(_gdn_kernel, chunks_per_step=chunks_per_step,
                             n_side=len(side_weights))
    side_specs = [_row_cast_spec(L // R, w) for w in side_weights]
    outs = pl.pallas_call(
        kern,
        grid=(L // R + 1,),
        in_specs=[
            pl.BlockSpec((R, GDN_QK), lambda n: (cur(n), base)),
            pl.BlockSpec((R, GDN_QK), lambda n: (cur(n), base + 1)),
            pl.BlockSpec((R, GDN_V), lambda n: (cur(n), base + 2)),
            pl.BlockSpec((R, GDN_V), lambda n: (cur(n), base + 3)),
            pl.BlockSpec((R, LANES), lambda n: (cur(n), 0)),
            pl.BlockSpec((LANES, R), lambda n: (0, cur(n))),
            pl.BlockSpec((2, LANES), lambda n: (0, 0)),
            pl.BlockSpec((LANES, 2), lambda n: (0, 0)),
            pl.BlockSpec((1, GDN_DV), lambda n: (0, 0)),
            pl.BlockSpec((R, D), lambda n: (prv(n), 0)),
            pl.BlockSpec((GDN_V, D), lambda n: (GLA_V // GDN_V, 0)),
        ] + side_specs,
        out_specs=[pl.BlockSpec((R, D), lambda n: (prv(n), 0))] + side_specs,
        out_shape=[jax.ShapeDtypeStruct((L, D), F32)]
        + [jax.ShapeDtypeStruct(w.shape, BF16) for w in side_weights],
        scratch_shapes=[pltpu.VMEM((GDN_HEADS, GDN_DK, GDN_DV), F32),
                        pltpu.VMEM((2, R, GDN_V), BF16)],
        compiler_params=pltpu.CompilerParams(
            dimension_semantics=("arbitrary",),
            vmem_limit_bytes=VMEM_LIMIT),
        name="gdn",
    )(p_big, p_big, p_big, p_big, p_sm, p_smt, a_row, a_col, norm_w, x1a, w_out_b,
      *side_weights)
    return outs[0], outs[1:]


def _ffn_kernel(x_ref, fnw_ref, wg_ref, wu_ref, wd_ref, onw_ref, y_ref, hf_ref, acc_ref):
    f = pl.program_id(1)

    @pl.when(f == 0)
    def _():
        x = x_ref[...]
        ms = jnp.mean(x * x, axis=-1, keepdims=True)
        hf_ref[...] = (x * lax.rsqrt(ms + EPS) * fnw_ref[...]).astype(BF16)
        acc_ref[...] = jnp.zeros_like(acc_ref)

    hf = hf_ref[...]
    g = _dot(hf, wg_ref[...])
    u = _dot(hf, wu_ref[...])
    acc_ref[...] += _dot((_silu(g) * u).astype(BF16), wd_ref[...])

    @pl.when(f == pl.num_programs(1) - 1)
    def _():
        r = x_ref[...] + acc_ref[...]
        ms = jnp.mean(r * r, axis=-1, keepdims=True)
        y_ref[...] = r * lax.rsqrt(ms + EPS) * onw_ref[...]


def _ffn(x1, ffn_norm_w, wg, wu, wd, final_norm_w, tm, tf):
    L, D = x1.shape
    F = wg.shape[1]
    return pl.pallas_call(
        _ffn_kernel,
        grid=(L // tm, F // tf),
        in_specs=[
            pl.BlockSpec((tm, D), lambda i, f: (i, 0)),
            pl.BlockSpec((1, D), lambda i, f: (0, 0)),
            pl.BlockSpec((D, tf), lambda i, f: (0, f)),
            pl.BlockSpec((D, tf), lambda i, f: (0, f)),
            pl.BlockSpec((tf, D), lambda i, f: (f, 0)),
            pl.BlockSpec((1, D), lambda i, f: (0, 0)),
        ],
        out_specs=pl.BlockSpec((tm, D), lambda i, f: (i, 0)),
        out_shape=jax.ShapeDtypeStruct((L, D), F32),
        scratch_shapes=[pltpu.VMEM((tm, D), BF16), pltpu.VMEM((tm, D), F32)],
        compiler_params=pltpu.CompilerParams(
            dimension_semantics=("arbitrary", "arbitrary"),
            vmem_limit_bytes=VMEM_LIMIT),
        name="ffn",
    )(x1, ffn_norm_w, wg, wu, wd, final_norm_w)


def _hi_lo(w):
    hi = w.astype(BF16)
    lo = (w - hi.astype(F32)).astype(BF16)
    return jnp.stack([hi, lo])


def _pick(n, candidates):
    for c in candidates:
        if n % c == 0:
            return c
    raise ValueError(f"no tile in {candidates} divides {n}")


def kernel(x, attn_norm_w, w_in, gla_gate_w2, gla_gate_b, gla_norm_w, gdn_conv_w,
           gdn_a_log, gdn_dt_bias, gdn_norm_w, w_out, ffn_norm_w, w_gate, w_up,
           w_down, final_norm_w):
    bsz, seq, d_model = x.shape
    assert bsz == 1 and seq % CHUNK == 0
    x2 = x.reshape(seq, d_model).astype(F32)

    sizes = (GLA_QK, GLA_QK, GLA_V, GLA_GATE_RANK, GLA_V,
             GDN_QK, GDN_QK, GDN_V, GDN_HEADS, GDN_HEADS, GDN_V)
    offs = np.concatenate([[0], np.cumsum(sizes)])
    assert w_in.shape == (d_model, offs[-1])
    w_in_t = w_in.astype(F32).T
    row = lambda i: w_in_t[offs[i]:offs[i + 1]]
    n_small = GLA_GATE_RANK + 2 * GDN_HEADS
    w_small_t = jnp.concatenate([row(3), row(8), row(9)], axis=0)
    w_small_t = jnp.pad(w_small_t, ((0, LANES - n_small), (0, 0)))
    w_smt = _hi_lo(w_small_t).reshape(2 * LANES, d_model)

    tm1 = _pick(seq, (1024, 512, 256, 128, 64))
    w_big_t, w_out_b = _wprep(w_in_t, w_out.astype(F32), 512)
    p_big, p_sm, p_smt = _inproj(
        x2, attn_norm_w.reshape(1, -1).astype(F32), w_big_t, w_smt,
        gdn_conv_w.astype(F32), tm1, 1024)

    cps = _pick(seq // CHUNK, (4, 2, 1))

    w2_pad = jnp.pad(gla_gate_w2.astype(F32), ((0, LANES - GLA_GATE_RANK), (0, 0)))
    x1a = _gla(p_big, p_sm, _hi_lo(w2_pad), gla_gate_b.reshape(1, -1).astype(F32),
               gla_norm_w.reshape(1, -1).astype(F32), x2, w_out_b, cps)

    a_log_pad = jnp.zeros((LANES,), F32).at[COL_A:COL_A + GDN_HEADS].set(gdn_a_log.astype(F32))
    dt_pad = jnp.zeros((LANES,), F32).at[COL_A:COL_A + GDN_HEADS].set(gdn_dt_bias.astype(F32))
    a_row = jnp.stack([a_log_pad, dt_pad])
    x1, (w_gate_b, w_up_b, w_down_b) = _gdn(
        p_big, p_sm, p_smt, a_row, a_row.T, gdn_norm_w.reshape(1, -1).astype(F32), x1a, w_out_b,
        [w.astype(F32) for w in (w_gate, w_up, w_down)], cps)

    tm4 = _pick(seq, (512, 256, 128, 64))

    y = _ffn(x1, ffn_norm_w.reshape(1, -1).astype(F32), w_gate_b, w_up_b, w_down_b,
             final_norm_w.reshape(1, -1).astype(F32), tm4, 512)
    return y.reshape(bsz, seq, d_model).astype(x.dtype)
```

```python
import functools

import numpy as np
import jax
import jax.numpy as jnp
from jax import lax
from jax.experimental import pallas as pl
from jax.experimental.pallas import tpu as pltpu

F32 = jnp.float32
BF16 = jnp.bfloat16

EPS = 1e-6
CHUNK = 64

GLA_HEADS = 4
GLA_DK = 128
GLA_DV = 256
GLA_GATE_RANK = 16
GLA_GATE_TAU = 16.0
GLA_QK = GLA_HEADS * GLA_DK
GLA_V = GLA_HEADS * GLA_DV

GDN_HEADS = 8
GDN_DK = 128
GDN_DV = 128
GDN_CONV = 4
GDN_QK = GDN_HEADS * GDN_DK
GDN_V = GDN_HEADS * GDN_DV

LANES = 128
VMEM_LIMIT = 48 * 1024 * 1024
VMEM_LIMIT_INPROJ = 56 * 1024 * 1024
VMEM_LIMIT_FFN = 60 * 1024 * 1024
N_BIG = 2 * GLA_QK + 2 * GLA_V + 2 * GDN_QK + 2 * GDN_V

COL_A = GLA_GATE_RANK
COL_B = GLA_GATE_RANK + GDN_HEADS


def _dot(a, b):
    return jnp.dot(a, b, preferred_element_type=F32)


def _dot_nt(a, b):
    return lax.dot_general(a, b, (((1,), (1,)), ((), ())), preferred_element_type=F32)


def _dot_tn(a, b):
    return lax.dot_general(a, b, (((0,), (0,)), ((), ())), preferred_element_type=F32)


def _split3(a):
    hi = a.astype(BF16)
    r1 = a - hi.astype(F32)
    mid = r1.astype(BF16)
    lo = (r1 - mid.astype(F32)).astype(BF16)
    return hi, mid, lo


def _dot_exact_rhs(a, b_exact):
    hi, mid, lo = _split3(a)
    return _dot(hi, b_exact) + _dot(mid, b_exact) + _dot(lo, b_exact)


def _dot_exact_lhs(a_exact, b):
    hi, mid, lo = _split3(b)
    return _dot(a_exact, hi) + _dot(a_exact, mid) + _dot(a_exact, lo)


def _sigmoid(x):
    return 1.0 / (1.0 + jnp.exp(-x))


def _silu(x):
    return x * _sigmoid(x)


def _softplus(x):
    return jnp.maximum(x, 0.0) + jnp.log1p(jnp.exp(-jnp.abs(x)))


def _log_sigmoid(x):
    return -_softplus(-x)


def _wprep_kernel(wa_ref, wb_ref, wout_ref, o_ref, wout_b_ref, *, shift_steps):
    j = pl.program_id(0)
    j1, j2 = shift_steps
    wout_b_ref[...] = wout_ref[...].astype(BF16)

    def realigned(delta):
        if delta == 0:
            return wa_ref[...].astype(BF16)
        return jnp.concatenate([wa_ref[pl.ds(delta, wa_ref.shape[0] - delta), :],
                                wb_ref[pl.ds(0, delta), :]], axis=0).astype(BF16)

    @pl.when(j < j1)
    def _():
        o_ref[...] = realigned(0)

    @pl.when((j >= j1) & (j < j2))
    def _():
        o_ref[...] = realigned(GLA_GATE_RANK)

    @pl.when(j >= j2)
    def _():
        o_ref[...] = realigned(GLA_GATE_RANK + 2 * GDN_HEADS)


def _wprep(w_in_t, w_out, tn):
    D = w_in_t.shape[1]
    wout_spec = _row_cast_spec(N_BIG // tn, w_out)
    n_small = GLA_GATE_RANK + 2 * GDN_HEADS
    g1 = GLA_QK * 2 + GLA_V
    g2 = g1 + GLA_V + GDN_QK * 2 + GDN_V
    assert g1 % tn == 0 and g2 % tn == 0 and N_BIG % tn == 0 and tn % n_small == 0
    return pl.pallas_call(
        functools.partial(_wprep_kernel, shift_steps=(g1 // tn, g2 // tn)),
        grid=(N_BIG // tn,),
        in_specs=[
            pl.BlockSpec((tn, D), lambda j: (j, 0)),
            pl.BlockSpec((n_small, D), lambda j: ((tn // n_small) * (j + 1), 0)),
            wout_spec,
        ],
        out_specs=[pl.BlockSpec((tn, D), lambda j: (j, 0)), wout_spec],
        out_shape=[jax.ShapeDtypeStruct((N_BIG, D), BF16),
                   jax.ShapeDtypeStruct(w_out.shape, BF16)],
        compiler_params=pltpu.CompilerParams(
            dimension_semantics=("arbitrary",),
            vmem_limit_bytes=VMEM_LIMIT),
        name="wprep",
    )(w_in_t, w_in_t, w_out)


def _inproj_kernel(x_ref, nw_ref, wbig_ref, wsmt_ref, cw_ref, pbig_ref, psm_ref, psmt_ref,
                   h_ref, ext_scr, tail_scr, *, tile_kinds):
    i = pl.program_id(0)
    j = pl.program_id(1)
    tm = h_ref.shape[0]
    tn = wbig_ref.shape[0]
    piece = min(tm, 256)

    @pl.when(j == 0)
    def _():
        x = x_ref[...]
        ms = jnp.mean(x * x, axis=-1, keepdims=True)
        h = x * lax.rsqrt(ms + EPS) * nw_ref[...]
        hb = h.astype(BF16)
        h_ref[...] = hb
        hl = (h - hb.astype(F32)).astype(BF16)
        r = _dot_nt(wsmt_ref[...], hb)
        pt = r[:LANES] + r[LANES:] + _dot_nt(wsmt_ref[pl.ds(0, LANES), :], hl)
        psmt_ref[...] = pt
        psm_ref[...] = pt.T

    def piece_dot(r0):
        return _dot_nt(h_ref[pl.ds(r0, piece), :], wbig_ref[...])

    def steps_of(*kinds):
        pred = None
        for jj, kind in enumerate(tile_kinds):
            if kind in kinds:
                pred = (j == jj) if pred is None else (pred | (j == jj))
        return pred

    @pl.when(steps_of('plain'))
    def _():
        pbig_ref[...] = _dot_nt(h_ref[...], wbig_ref[...]).astype(BF16)

    @pl.when(steps_of('silu'))
    def _():
        for r0 in range(0, tm, piece):
            pbig_ref[pl.ds(r0, piece), :] = _silu(piece_dot(r0)).astype(BF16)

    conv_kinds = ('conv_norm_q', 'conv_norm_k', 'conv')
    first_conv = min(jj for jj, kind in enumerate(tile_kinds) if kind in conv_kinds)

    def conv_steps(normalise):
        jc = j - first_conv
        ext_scr[pl.ds(0, 8), :] = jnp.where(i == 0, 0.0, tail_scr[jc])
        scale = jnp.where(steps_of('conv_norm_q'), GDN_DK ** -0.5, 1.0) if normalise else None
        for r0 in range(0, tm, piece):
            ext_scr[pl.ds(8 + r0, piece), :] = piece_dot(r0)
            acc = cw_ref[GDN_CONV - 1:GDN_CONV, :] * ext_scr[pl.ds(8 + r0, piece), :]
            for t in range(GDN_CONV - 1):
                acc = acc + cw_ref[t:t + 1, :] * ext_scr[pl.ds(8 - (GDN_CONV - 1) + t + r0, piece), :]
            y = _silu(acc)
            if normalise:
                heads = []
                for hh in range(tn // GDN_DK):
                    yh = y[:, hh * GDN_DK:(hh + 1) * GDN_DK]
                    ssq = jnp.sum(yh * yh, axis=-1, keepdims=True)
                    heads.append(yh * (lax.rsqrt(ssq + EPS) * scale))
                y = jnp.concatenate(heads, axis=1)
            pbig_ref[pl.ds(r0, piece), :] = y.astype(BF16)
        tail_scr[jc] = ext_scr[pl.ds(tm, 8), :]

    @pl.when(steps_of('conv_norm_q', 'conv_norm_k'))
    def _():
        conv_steps(True)

    @pl.when(steps_of('conv'))
    def _():
        conv_steps(False)


def _max_split(n_steps, size, quantum):
    for nb in range(n_steps, 0, -1):
        if size % (nb * quantum) == 0:
            return nb
    raise ValueError(f"{size} is not a multiple of {quantum}")


def _inproj(x2, norm_w, w_big, w_smt, conv_w, tm, tn):
    L, D = x2.shape
    widths = (('plain', 2 * GLA_QK + GLA_V), ('silu', GLA_V), ('conv_norm_q', GDN_QK),
              ('conv_norm_k', GDN_QK), ('conv', GDN_V), ('silu', GDN_V))
    assert all(width % tn == 0 for _, width in widths) and tn % GDN_DK == 0
    tile_kinds = tuple(kind for kind, width in widths for _ in range(width // tn))
    conv_tiles = [jj for jj, kind in enumerate(tile_kinds) if kind.startswith('conv')]
    assert conv_tiles == list(range(conv_tiles[0], conv_tiles[-1] + 1))
    return pl.pallas_call(
        functools.partial(_inproj_kernel, tile_kinds=tile_kinds),
        grid=(L // tm, N_BIG // tn),
        in_specs=[
            pl.BlockSpec((tm, D), lambda i, j: (i, 0)),
            pl.BlockSpec((1, D), lambda i, j: (0, 0)),
            pl.BlockSpec((tn, D), lambda i, j: (j, 0)),
            pl.BlockSpec((2 * LANES, D), lambda i, j: (0, 0)),
            pl.BlockSpec((GDN_CONV, tn),
                         lambda i, j: (0, jnp.clip(j - conv_tiles[0], 0, len(conv_tiles) - 1))),
        ],
        out_specs=[
            pl.BlockSpec((tm, tn), lambda i, j: (i, j)),
            pl.BlockSpec((tm, LANES), lambda i, j: (i, 0)),
            pl.BlockSpec((LANES, tm), lambda i, j: (0, i)),
        ],
        out_shape=[
            jax.ShapeDtypeStruct((L, N_BIG), BF16),
            jax.ShapeDtypeStruct((L, LANES), F32),
            jax.ShapeDtypeStruct((LANES, L), F32),
        ],
        scratch_shapes=[pltpu.VMEM((tm, D), BF16),
                        pltpu.VMEM((tm + 8, tn), F32),
                        pltpu.VMEM((len(conv_tiles), 8, tn), F32)],
        compiler_params=pltpu.CompilerParams(
            dimension_semantics=("arbitrary", "arbitrary"),
            vmem_limit_bytes=VMEM_LIMIT_INPROJ),
        name="inproj",
    )(x2, norm_w, w_big, w_smt, conv_w)


def _ones_where(mask):
    return jnp.where(mask, 1.0, 0.0).astype(BF16)


def _chunk_tril(rows):
    ti = lax.broadcasted_iota(jnp.int32, (rows, rows), 0)
    si = lax.broadcasted_iota(jnp.int32, (rows, rows), 1)
    return ((ti >> 6) == (si >> 6)) & (ti >= si)


def _gla_ref_rows(c, c_scr, r0, ks, blk):
    C = CHUNK
    if 2 * blk >= 8:
        return jnp.concatenate(
            [jnp.broadcast_to(c_scr[pl.ds(r0 + g + blk - 1, 1), ks], (2 * blk, GLA_DK))
             for g in range(0, C, 2 * blk)], axis=0)
    pos = lax.broadcasted_iota(jnp.int32, (C, GLA_DK), 0) & (2 * blk - 1)
    out = c
    for off in range(2 * blk):
        if off != blk - 1:
            shift = (off - (blk - 1)) % C
            out = jnp.where(pos == off, pltpu.roll(c, shift, axis=0), out)
    return out


def _gla_scores(q, k, c, c_scr, r0, ks):
    C = CHUNK
    row = lax.broadcasted_iota(jnp.int32, (C, GLA_DK), 0)
    ti = lax.broadcasted_iota(jnp.int32, (C, C), 0)
    si = lax.broadcasted_iota(jnp.int32, (C, C), 1)
    a = jnp.where(ti == si, jnp.sum(q * k, axis=-1, keepdims=True), 0.0)
    blk = C // 2
    while blk >= 1:
        right = (row & (2 * blk - 1)) >= blk
        d = c - _gla_ref_rows(c, c_scr, r0, ks, blk)
        e = jnp.exp(jnp.where(right, d, -d))
        qt = jnp.where(right, q * e, 0.0).astype(BF16)
        kt = jnp.where(right, 0.0, k * e).astype(BF16)
        part = _dot_nt(qt, kt)
        if 2 * blk < C:
            shift = (2 * blk).bit_length() - 1
            part = jnp.where((ti >> shift) == (si >> shift), part, 0.0)
        a = a + part
        blk //= 2
    return a


def _lagged_proj_pieces(o_scr, base_ref, w_ref, out_ref, n_pieces):
    prev = 1 - (pl.program_id(0) & 1)
    width = w_ref.shape[1] // n_pieces

    def make(p):
        cols = slice(p * width, (p + 1) * width)

        def run():
            out_ref[:, cols] = base_ref[:, cols] + _dot(o_scr[prev], w_ref[:, cols])
        return run
    return [make(p) for p in range(n_pieces)]


def _gla_chunks(q_ref, v_ref, gate_ref, nw_ref, o_ref, st_ref, c_scr, k_scr, chunks_per_step,
                fillers):
    C = CHUNK
    tasks = [(ci, h) for ci in range(chunks_per_step) for h in range(GLA_HEADS)]

    def front(t):
        ci, h = t
        r0 = ci * C
        ks = slice(h * GLA_DK, (h + 1) * GLA_DK)
        q = q_ref[pl.ds(r0, C), ks].astype(F32) * (GLA_DK ** -0.5)
        k = k_scr[pl.ds(r0, C), ks]
        c = c_scr[pl.ds(r0, C), ks]
        a = _gla_scores(q, k, c, c_scr, r0, ks)
        c_last = c_scr[pl.ds(r0 + C - 1, 1), ks]
        qd = (q * jnp.exp(c)).astype(BF16)
        kd = (k * jnp.exp(c_last - c)).astype(BF16)
        return a.astype(BF16), qd, kd, jnp.exp(c_last)

    def back(t, a, qd, kd, g_last):
        ci, h = t
        rows = pl.ds(ci * C, C)
        vs = slice(h * GLA_DV, (h + 1) * GLA_DV)
        v = v_ref[rows, vs]
        st = st_ref[h]
        o = _dot(a, v) + _dot_nt(qd, st.astype(BF16))
        st_ref[h] = g_last * st + _dot_tn(v, kd)
        ms = jnp.mean(o * o, axis=-1, keepdims=True)
        on = o * lax.rsqrt(ms + EPS) * nw_ref[...]
        o_ref[rows, vs] = (on * gate_ref[rows, vs].astype(F32)).astype(BF16)

    lag = 4
    fronts = {}
    for f in fillers:
        f()
    for idx, t in enumerate(tasks):
        fronts[t] = front(t)
        if idx >= lag:
            back(tasks[idx - lag], *fronts.pop(tasks[idx - lag]))
    for t in tasks[-lag:]:
        back(t, *fronts.pop(t))


def _gla_kernel(q_ref, k_ref, v_ref, gate_ref, psm_ref, w2_ref, gb_ref, nw_ref,
                base_ref, wproj_ref, xo_ref, st_ref, c_scr, k_scr, o_scr, *, chunks_per_step):
    slot = pl.program_id(0) & 1

    @pl.when(pl.program_id(0) == 0)
    def _():
        st_ref[...] = jnp.zeros_like(st_ref)
        o_scr[1] = jnp.zeros(o_scr.shape[1:], o_scr.dtype)

    proj = _lagged_proj_pieces(o_scr, base_ref, wproj_ref, xo_ref, 8)

    a_hi, a_mid, _ = _split3(psm_ref[...])
    z = (_dot(a_hi, w2_ref[0]) + _dot(a_mid, w2_ref[0]) + _dot(a_hi, w2_ref[1])
         + gb_ref[...])
    for f in proj[:3]:
        f()
    log_a = _log_sigmoid(z) * (1.0 / GLA_GATE_TAU)
    tril = _ones_where(_chunk_tril(CHUNK * chunks_per_step))
    c_scr[...] = _dot_exact_lhs(tril, log_a)
    k_scr[...] = k_ref[...].astype(F32)

    _gla_chunks(q_ref, v_ref, gate_ref, nw_ref, o_scr.at[slot], st_ref, c_scr, k_scr,
                chunks_per_step, proj[3:])


def _row_cast_spec(n_steps, w):
    nb = _max_split(n_steps, w.shape[0], 16)
    return pl.BlockSpec((w.shape[0] // nb, w.shape[1]), lambda n: (jnp.minimum(n, nb - 1), 0))


def _gla(p_big, p_sm, w2, gate_b, norm_w, x2, w_out_b, chunks_per_step):
    L, D = x2.shape
    R = CHUNK * chunks_per_step
    last = L // R - 1
    cur = lambda n: jnp.minimum(n, last)
    prv = lambda n: jnp.maximum(n - 1, 0)
    kern = functools.partial(_gla_kernel, chunks_per_step=chunks_per_step)
    return pl.pallas_call(
        kern,
        grid=(L // R + 1,),
        in_specs=[
            pl.BlockSpec((R, GLA_QK), lambda n: (cur(n), 0)),
            pl.BlockSpec((R, GLA_QK), lambda n: (cur(n), 1)),
            pl.BlockSpec((R, GLA_V), lambda n: (cur(n), 1)),
            pl.BlockSpec((R, GLA_V), lambda n: (cur(n), 2)),
            pl.BlockSpec((R, LANES), lambda n: (cur(n), 0)),
            pl.BlockSpec((2, LANES, GLA_QK), lambda n: (0, 0, 0)),
            pl.BlockSpec((1, GLA_QK), lambda n: (0, 0)),
            pl.BlockSpec((1, GLA_DV), lambda n: (0, 0)),
            pl.BlockSpec((R, D), lambda n: (prv(n), 0)),
            pl.BlockSpec((GLA_V, D), lambda n: (0, 0)),
        ],
        out_specs=pl.BlockSpec((R, D), lambda n: (prv(n), 0)),
        out_shape=jax.ShapeDtypeStruct((L, D), F32),
        scratch_shapes=[
            pltpu.VMEM((GLA_HEADS, GLA_DV, GLA_DK), F32),
            pltpu.VMEM((R, GLA_QK), F32),
            pltpu.VMEM((R, GLA_QK), F32),
            pltpu.VMEM((2, R, GLA_V), BF16),
        ],
        compiler_params=pltpu.CompilerParams(
            dimension_semantics=("arbitrary",),
            vmem_limit_bytes=VMEM_LIMIT),
        name="gla",
    )(p_big, p_big, p_big, p_big, p_sm, w2, gate_b, norm_w, x2, w_out_b)


def _gdn_chunks(q_ref, k_ref, v_ref, z_ref, nw_ref, o_ref, s_ref, cum_col, cum_row, beta_col,
                chunks_per_step, fillers):
    C = CHUNK
    fillers = list(fillers)

    def fill():
        if fillers:
            fillers.pop(0)()
    ti = lax.broadcasted_iota(jnp.int32, (C, 2 * C), 0)
    li = lax.broadcasted_iota(jnp.int32, (C, 2 * C), 1)
    si = li & (C - 1)
    right = li >= C
    incl = ti >= si
    strict = ti > si
    eye_right = jnp.where((ti == si) & right, 1.0, 0.0)
    tasks = [(ci, h) for ci in range(chunks_per_step) for h in range(GDN_HEADS)]

    wmat, attn, rhs, wq, kd, g_last = {}, {}, {}, {}, {}, {}
    for t in tasks:
        ci, h = t
        r0 = ci * C
        hs = slice(h * GDN_DK, (h + 1) * GDN_DK)
        qnb = q_ref[pl.ds(r0, C), hs]
        knb = k_ref[pl.ds(r0, C), hs]
        kn = knb.astype(F32)
        cv = v_ref[pl.ds(r0, C), hs].astype(F32)
        cc = cum_col[r0:r0 + C, COL_A + h:COL_A + h + 1]
        cr = cum_row[COL_A + h:COL_A + h + 1, 2 * r0:2 * r0 + 2 * C]
        beta = beta_col[r0:r0 + C, COL_B + h:COL_B + h + 1]
        gamma = jnp.exp(jnp.where(incl, cc - cr, -jnp.inf))
        kb = kn * beta
        x = _dot_nt(jnp.concatenate([qnb, kb.astype(BF16)], axis=0),
                    jnp.concatenate([knb, knb], axis=0))
        attn[t] = (x[:C, :C] * gamma[:, :C]).astype(BF16)
        wmat[t] = jnp.where(strict, x[C:] * gamma, 0.0)
        e_c = jnp.exp(cc)
        c_last = cc[C - 1:C, :]
        rhs1 = jnp.concatenate([cv * beta, kb * e_c], axis=1).astype(BF16)
        rhs[t] = jnp.concatenate([rhs1, rhs1], axis=0)
        wq[t] = (qnb.astype(F32) * e_c).astype(BF16)
        kd[t] = (kn * jnp.exp(c_last - cc)).astype(BF16)
        g_last[t] = jnp.exp(c_last)

    fill()
    for t in tasks:
        n2b = wmat[t].astype(BF16)
        wmat[t] = jnp.where(right, eye_right - wmat[t], _dot(n2b[:, :C], n2b))
    fill()
    for _ in range(5):
        for t in tasks:
            wb = wmat[t].astype(BF16)
            wmat[t] = _dot(wb[:, :C], wb) + jnp.where(right, wmat[t], 0.0)
        fill()
    sol = {t: _dot(wmat[t].astype(BF16), rhs[t]) for t in tasks}
    fill()

    for ci in range(chunks_per_step):
        rows = pl.ds(ci * C, C)
        heads = [(ci, h) for h in range(GDN_HEADS)]
        s_old = {t: s_ref[t[1]] for t in heads}
        ws = {t: _dot(jnp.concatenate([sol[t][:, GDN_DV:].astype(BF16), wq[t]], axis=0),
                      s_old[t].astype(BF16)) for t in heads}
        vnb = {t: (sol[t][:, :GDN_DV] - ws[t][:C]).astype(BF16) for t in heads}
        for t in heads:
            h = t[1]
            hs = slice(h * GDN_DK, (h + 1) * GDN_DK)
            o = ws[t][C:] + _dot(attn[t], vnb[t])
            s_ref[h] = g_last[t] * s_old[t] + _dot_tn(kd[t], vnb[t])
            ms = jnp.mean(o * o, axis=-1, keepdims=True)
            on = o * lax.rsqrt(ms + EPS) * nw_ref[...]
            o_ref[rows, hs] = (on * z_ref[rows, hs].astype(F32)).astype(BF16)
    while fillers:
        fill()


def _gdn_kernel(q_ref, k_ref, v_ref, z_ref, psm_ref, psmt_ref, arow_ref,
                acol_ref, nw_ref, base_ref, wproj_ref, *rest, chunks_per_step, n_side):
    side_in = rest[:n_side]
    xo_ref = rest[n_side]
    side_out = rest[n_side + 1:2 * n_side + 1]
    s_ref, o_scr = rest[2 * n_side + 1:]
    R = CHUNK * chunks_per_step
    slot = pl.program_id(0) & 1

    @pl.when(pl.program_id(0) == 0)
    def _():
        s_ref[...] = jnp.zeros_like(s_ref)
        o_scr[1] = jnp.zeros(o_scr.shape[1:], o_scr.dtype)

    for src, dst in zip(side_in, side_out):
        dst[...] = src[...].astype(BF16)

    psm = psm_ref[...]
    g_col = -jnp.exp(arow_ref[0:1, :]) * _softplus(psm + arow_ref[1:2, :])
    beta_col = _sigmoid(psm)
    cum_col = _dot_exact_lhs(_ones_where(_chunk_tril(R)), g_col)
    g_row = -jnp.exp(acol_ref[:, 0:1]) * _softplus(psmt_ref[...] + acol_ref[:, 1:2])
    ji = lax.broadcasted_iota(jnp.int32, (R, 2 * R), 0)
    li = lax.broadcasted_iota(jnp.int32, (R, 2 * R), 1)
    dup = ((ji >> 6) == (li >> 7)) & ((ji & (CHUNK - 1)) <= (li & (CHUNK - 1)))
    cum_row = _dot_exact_rhs(g_row, _ones_where(dup))

    _gdn_chunks(q_ref, k_ref, v_ref, z_ref, nw_ref, o_scr.at[slot], s_ref, cum_col, cum_row,
                beta_col, chunks_per_step,
                _lagged_proj_pieces(o_scr, base_ref, wproj_ref, xo_ref, 8))


def _gdn(p_big, p_sm, p_smt, a_row, a_col, norm_w, x1a, w_out_b, side_weights, chunks_per_step):
    L, D = x1a.shape
    R = CHUNK * chunks_per_step
    base = (GLA_QK * 2 + GLA_V * 2) // GDN_QK
    last = L // R - 1
    cur = lambda n: jnp.minimum(n, last)
    prv = lambda n: jnp.maximum(n - 1, 0)
    kern = functools.partial(_gdn_kernel, chunks_per_step=chunks_per_step,
                             n_side=len(side_weights))
    side_specs = [_row_cast_spec(L // R, w) for w in side_weights]
    outs = pl.pallas_call(
        kern,
        grid=(L // R + 1,),
        in_specs=[
            pl.BlockSpec((R, GDN_QK), lambda n: (cur(n), base)),
            pl.BlockSpec((R, GDN_QK), lambda n: (cur(n), base + 1)),
            pl.BlockSpec((R, GDN_V), lambda n: (cur(n), base + 2)),
            pl.BlockSpec((R, GDN_V), lambda n: (cur(n), base + 3)),
            pl.BlockSpec((R, LANES), lambda n: (cur(n), 0)),
            pl.BlockSpec((LANES, R), lambda n: (0, cur(n))),
            pl.BlockSpec((2, LANES), lambda n: (0, 0)),
            pl.BlockSpec((LANES, 2), lambda n: (0, 0)),
            pl.BlockSpec((1, GDN_DV), lambda n: (0, 0)),
            pl.BlockSpec((R, D), lambda n: (prv(n), 0)),
            pl.BlockSpec((GDN_V, D), lambda n: (GLA_V // GDN_V, 0)),
        ] + side_specs,
        out_specs=[pl.BlockSpec((R, D), lambda n: (prv(n), 0))] + side_specs,
        out_shape=[jax.ShapeDtypeStruct((L, D), F32)]
        + [jax.ShapeDtypeStruct(w.shape, BF16) for w in side_weights],
        scratch_shapes=[pltpu.VMEM((GDN_HEADS, GDN_DK, GDN_DV), F32),
                        pltpu.VMEM((2, R, GDN_V), BF16)],
        compiler_params=pltpu.CompilerParams(
            dimension_semantics=("arbitrary",),
            vmem_limit_bytes=VMEM_LIMIT),
        name="gdn",
    )(p_big, p_big, p_big, p_big, p_sm, p_smt, a_row, a_col, norm_w, x1a, w_out_b,
      *side_weights)
    return outs[0], outs[1:]


def _ffn_kernel(x_ref, fnw_ref, wg_ref, wu_ref, wd_ref, onw_ref, y_ref, hf_ref):
    f = pl.program_id(1)

    @pl.when(f == 0)
    def _():
        x = x_ref[...]
        ms = jnp.mean(x * x, axis=-1, keepdims=True)
        hf_ref[...] = (x * lax.rsqrt(ms + EPS) * fnw_ref[...]).astype(BF16)
        y_ref[...] = x

    hf = hf_ref[...]
    g = _dot(hf, wg_ref[...])
    u = _dot(hf, wu_ref[...])
    y_ref[...] += _dot((_silu(g) * u).astype(BF16), wd_ref[...])

    @pl.when(f == pl.num_programs(1) - 1)
    def _():
        r = y_ref[...]
        ms = jnp.mean(r * r, axis=-1, keepdims=True)
        y_ref[...] = r * lax.rsqrt(ms + EPS) * onw_ref[...]


def _ffn(x1, ffn_norm_w, wg, wu, wd, final_norm_w, tm, tf):
    L, D = x1.shape
    F = wg.shape[1]
    return pl.pallas_call(
        _ffn_kernel,
        grid=(L // tm, F // tf),
        in_specs=[
            pl.BlockSpec((tm, D), lambda i, f: (i, 0)),
            pl.BlockSpec((1, D), lambda i, f: (0, 0)),
            pl.BlockSpec((D, tf), lambda i, f: (0, f)),
            pl.BlockSpec((D, tf), lambda i, f: (0, f)),
            pl.BlockSpec((tf, D), lambda i, f: (f, 0)),
            pl.BlockSpec((1, D), lambda i, f: (0, 0)),
        ],
        out_specs=pl.BlockSpec((tm, D), lambda i, f: (i, 0)),
        out_shape=jax.ShapeDtypeStruct((L, D), F32),
        scratch_shapes=[pltpu.VMEM((tm, D), BF16)],
        compiler_params=pltpu.CompilerParams(
            dimension_semantics=("arbitrary", "arbitrary"),
            vmem_limit_bytes=VMEM_LIMIT_FFN),
        name="ffn",
    )(x1, ffn_norm_w, wg, wu, wd, final_norm_w)


def _hi_lo(w):
    hi = w.astype(BF16)
    lo = (w - hi.astype(F32)).astype(BF16)
    return jnp.stack([hi, lo])


def _pick(n, candidates):
    for c in candidates:
        if n % c == 0:
            return c
    raise ValueError(f"no tile in {candidates} divides {n}")


def kernel(x, attn_norm_w, w_in, gla_gate_w2, gla_gate_b, gla_norm_w, gdn_conv_w,
           gdn_a_log, gdn_dt_bias, gdn_norm_w, w_out, ffn_norm_w, w_gate, w_up,
           w_down, final_norm_w):
    bsz, seq, d_model = x.shape
    assert bsz == 1 and seq % CHUNK == 0
    x2 = x.reshape(seq, d_model).astype(F32)

    sizes = (GLA_QK, GLA_QK, GLA_V, GLA_GATE_RANK, GLA_V,
             GDN_QK, GDN_QK, GDN_V, GDN_HEADS, GDN_HEADS, GDN_V)
    offs = np.concatenate([[0], np.cumsum(sizes)])
    assert w_in.shape == (d_model, offs[-1])
    w_in_t = w_in.astype(F32).T
    row = lambda i: w_in_t[offs[i]:offs[i + 1]]
    n_small = GLA_GATE_RANK + 2 * GDN_HEADS
    w_small_t = jnp.concatenate([row(3), row(8), row(9)], axis=0)
    w_small_t = jnp.pad(w_small_t, ((0, LANES - n_small), (0, 0)))
    w_smt = _hi_lo(w_small_t).reshape(2 * LANES, d_model)

    tm1 = _pick(seq, (1024, 512, 256, 128, 64))
    w_big_t, w_out_b = _wprep(w_in_t, w_out.astype(F32), 512)
    p_big, p_sm, p_smt = _inproj(
        x2, attn_norm_w.reshape(1, -1).astype(F32), w_big_t, w_smt,
        gdn_conv_w.astype(F32), tm1, 1024)

    cps = _pick(seq // CHUNK, (4, 2, 1))

    w2_pad = jnp.pad(gla_gate_w2.astype(F32), ((0, LANES - GLA_GATE_RANK), (0, 0)))
    x1a = _gla(p_big, p_sm, _hi_lo(w2_pad), gla_gate_b.reshape(1, -1).astype(F32),
               gla_norm_w.reshape(1, -1).astype(F32), x2, w_out_b, cps)

    a_log_pad = jnp.zeros((LANES,), F32).at[COL_A:COL_A + GDN_HEADS].set(gdn_a_log.astype(F32))
    dt_pad = jnp.zeros((LANES,), F32).at[COL_A:COL_A + GDN_HEADS].set(gdn_dt_bias.astype(F32))
    a_row = jnp.stack([a_log_pad, dt_pad])
    x1, (w_gate_b, w_up_b, w_down_b) = _gdn(
        p_big, p_sm, p_smt, a_row, a_row.T, gdn_norm_w.reshape(1, -1).astype(F32), x1a, w_out_b,
        [w.astype(F32) for w in (w_gate, w_up, w_down)], cps)

    tm4 = _pick(seq, (1024, 512, 256, 128, 64))

    y = _ffn(x1, ffn_norm_w.reshape(1, -1).astype(F32), w_gate_b, w_up_b, w_down_b,
             final_norm_w.reshape(1, -1).astype(F32), tm4, 512)
    return y.reshape(bsz, seq, d_model).astype(x.dtype)
```

```python
import functools

import numpy as np
import jax
import jax.numpy as jnp
from jax import lax
from jax.experimental import pallas as pl
from jax.experimental.pallas import tpu as pltpu

F32 = jnp.float32
BF16 = jnp.bfloat16

EPS = 1e-6
CHUNK = 64

GLA_HEADS = 4
GLA_DK = 128
GLA_DV = 256
GLA_GATE_RANK = 16
GLA_GATE_TAU = 16.0
GLA_QK = GLA_HEADS * GLA_DK
GLA_V = GLA_HEADS * GLA_DV

GDN_HEADS = 8
GDN_DK = 128
GDN_DV = 128
GDN_CONV = 4
GDN_QK = GDN_HEADS * GDN_DK
GDN_V = GDN_HEADS * GDN_DV

LANES = 128
VMEM_LIMIT = 48 * 1024 * 1024
VMEM_LIMIT_INPROJ = 56 * 1024 * 1024
VMEM_LIMIT_FFN = 60 * 1024 * 1024
N_BIG = 2 * GLA_QK + 2 * GLA_V + 2 * GDN_QK + 2 * GDN_V

COL_A = GLA_GATE_RANK
COL_B = GLA_GATE_RANK + GDN_HEADS


def _dot(a, b):
    return jnp.dot(a, b, preferred_element_type=F32)


def _dot_nt(a, b):
    return lax.dot_general(a, b, (((1,), (1,)), ((), ())), preferred_element_type=F32)


def _dot_tn(a, b):
    return lax.dot_general(a, b, (((0,), (0,)), ((), ())), preferred_element_type=F32)


def _split3(a):
    hi = a.astype(BF16)
    r1 = a - hi.astype(F32)
    mid = r1.astype(BF16)
    lo = (r1 - mid.astype(F32)).astype(BF16)
    return hi, mid, lo


def _dot_exact_rhs(a, b_exact):
    hi, mid, lo = _split3(a)
    return _dot(hi, b_exact) + _dot(mid, b_exact) + _dot(lo, b_exact)


def _dot_exact_lhs(a_exact, b):
    hi, mid, lo = _split3(b)
    return _dot(a_exact, hi) + _dot(a_exact, mid) + _dot(a_exact, lo)


def _sigmoid(x):
    return 1.0 / (1.0 + jnp.exp(-x))


def _silu(x):
    return x * _sigmoid(x)


def _softplus(x):
    return jnp.maximum(x, 0.0) + jnp.log1p(jnp.exp(-jnp.abs(x)))


def _log_sigmoid(x):
    return -_softplus(-x)


def _wprep_kernel(wa_ref, wb_ref, wout_ref, o_ref, wout_b_ref, *, shift_steps):
    j = pl.program_id(0)
    j1, j2 = shift_steps
    wout_b_ref[...] = wout_ref[...].astype(BF16)

    def realigned(delta):
        if delta == 0:
            return wa_ref[...].astype(BF16)
        return jnp.concatenate([wa_ref[pl.ds(delta, wa_ref.shape[0] - delta), :],
                                wb_ref[pl.ds(0, delta), :]], axis=0).astype(BF16)

    @pl.when(j < j1)
    def _():
        o_ref[...] = realigned(0)

    @pl.when((j >= j1) & (j < j2))
    def _():
        o_ref[...] = realigned(GLA_GATE_RANK)

    @pl.when(j >= j2)
    def _():
        o_ref[...] = realigned(GLA_GATE_RANK + 2 * GDN_HEADS)


def _wprep(w_in_t, w_out, tn):
    D = w_in_t.shape[1]
    wout_spec = _row_cast_spec(N_BIG // tn, w_out)
    n_small = GLA_GATE_RANK + 2 * GDN_HEADS
    g1 = GLA_QK * 2 + GLA_V
    g2 = g1 + GLA_V + GDN_QK * 2 + GDN_V
    assert g1 % tn == 0 and g2 % tn == 0 and N_BIG % tn == 0 and tn % n_small == 0
    return pl.pallas_call(
        functools.partial(_wprep_kernel, shift_steps=(g1 // tn, g2 // tn)),
        grid=(N_BIG // tn,),
        in_specs=[
            pl.BlockSpec((tn, D), lambda j: (j, 0)),
            pl.BlockSpec((n_small, D), lambda j: ((tn // n_small) * (j + 1), 0)),
            wout_spec,
        ],
        out_specs=[pl.BlockSpec((tn, D), lambda j: (j, 0)), wout_spec],
        out_shape=[jax.ShapeDtypeStruct((N_BIG, D), BF16),
                   jax.ShapeDtypeStruct(w_out.shape, BF16)],
        compiler_params=pltpu.CompilerParams(
            dimension_semantics=("arbitrary",),
            vmem_limit_bytes=VMEM_LIMIT),
        name="wprep",
    )(w_in_t, w_in_t, w_out)


def _inproj_kernel(x_ref, nw_ref, wbig_ref, wsmt_ref, cw_ref, pbig_ref, psm_ref, psmt_ref,
                   h_ref, ext_scr, tail_scr, *, tile_kinds):
    i = pl.program_id(0)
    j = pl.program_id(1)
    tm = h_ref.shape[0]
    tn = wbig_ref.shape[0]
    piece = min(tm, 256)

    @pl.when(j == 0)
    def _():
        x = x_ref[...]
        ms = jnp.mean(x * x, axis=-1, keepdims=True)
        h = x * lax.rsqrt(ms + EPS) * nw_ref[...]
        hb = h.astype(BF16)
        h_ref[...] = hb
        hl = (h - hb.astype(F32)).astype(BF16)
        r = _dot_nt(wsmt_ref[...], hb)
        pt = r[:LANES] + r[LANES:] + _dot_nt(wsmt_ref[pl.ds(0, LANES), :], hl)
        psmt_ref[...] = pt
        psm_ref[...] = pt.T

    def piece_dot(r0):
        return _dot_nt(h_ref[pl.ds(r0, piece), :], wbig_ref[...])

    def steps_of(*kinds):
        pred = None
        for jj, kind in enumerate(tile_kinds):
            if kind in kinds:
                pred = (j == jj) if pred is None else (pred | (j == jj))
        return pred

    @pl.when(steps_of('plain'))
    def _():
        pbig_ref[...] = _dot_nt(h_ref[...], wbig_ref[...]).astype(BF16)

    @pl.when(steps_of('silu'))
    def _():
        for r0 in range(0, tm, piece):
            pbig_ref[pl.ds(r0, piece), :] = _silu(piece_dot(r0)).astype(BF16)

    conv_kinds = ('conv_norm_q', 'conv_norm_k', 'conv')
    first_conv = min(jj for jj, kind in enumerate(tile_kinds) if kind in conv_kinds)

    def conv_steps(normalise):
        jc = j - first_conv
        ext_scr[pl.ds(0, 8), :] = jnp.where(i == 0, 0.0, tail_scr[jc])
        scale = jnp.where(steps_of('conv_norm_q'), GDN_DK ** -0.5, 1.0) if normalise else None
        for r0 in range(0, tm, piece):
            ext_scr[pl.ds(8 + r0, piece), :] = piece_dot(r0)
            acc = cw_ref[GDN_CONV - 1:GDN_CONV, :] * ext_scr[pl.ds(8 + r0, piece), :]
            for t in range(GDN_CONV - 1):
                acc = acc + cw_ref[t:t + 1, :] * ext_scr[pl.ds(8 - (GDN_CONV - 1) + t + r0, piece), :]
            y = _silu(acc)
            if normalise:
                heads = []
                for hh in range(tn // GDN_DK):
                    yh = y[:, hh * GDN_DK:(hh + 1) * GDN_DK]
                    ssq = jnp.sum(yh * yh, axis=-1, keepdims=True)
                    heads.append(yh * (lax.rsqrt(ssq + EPS) * scale))
                y = jnp.concatenate(heads, axis=1)
            pbig_ref[pl.ds(r0, piece), :] = y.astype(BF16)
        tail_scr[jc] = ext_scr[pl.ds(tm, 8), :]

    @pl.when(steps_of('conv_norm_q', 'conv_norm_k'))
    def _():
        conv_steps(True)

    @pl.when(steps_of('conv'))
    def _():
        conv_steps(False)


def _max_split(n_steps, size, quantum):
    for nb in range(n_steps, 0, -1):
        if size % (nb * quantum) == 0:
            return nb
    raise ValueError(f"{size} is not a multiple of {quantum}")


def _inproj(x2, norm_w, w_big, w_smt, conv_w, tm, tn):
    L, D = x2.shape
    widths = (('plain', 2 * GLA_QK + GLA_V), ('silu', GLA_V), ('conv_norm_q', GDN_QK),
              ('conv_norm_k', GDN_QK), ('conv', GDN_V), ('silu', GDN_V))
    assert all(width % tn == 0 for _, width in widths) and tn % GDN_DK == 0
    tile_kinds = tuple(kind for kind, width in widths for _ in range(width // tn))
    conv_tiles = [jj for jj, kind in enumerate(tile_kinds) if kind.startswith('conv')]
    assert conv_tiles == list(range(conv_tiles[0], conv_tiles[-1] + 1))
    return pl.pallas_call(
        functools.partial(_inproj_kernel, tile_kinds=tile_kinds),
        grid=(L // tm, N_BIG // tn),
        in_specs=[
            pl.BlockSpec((tm, D), lambda i, j: (i, 0)),
            pl.BlockSpec((1, D), lambda i, j: (0, 0)),
            pl.BlockSpec((tn, D), lambda i, j: (j, 0)),
            pl.BlockSpec((2 * LANES, D), lambda i, j: (0, 0)),
            pl.BlockSpec((GDN_CONV, tn),
                         lambda i, j: (0, jnp.clip(j - conv_tiles[0], 0, len(conv_tiles) - 1))),
        ],
        out_specs=[
            pl.BlockSpec((tm, tn), lambda i, j: (i, j)),
            pl.BlockSpec((tm, LANES), lambda i, j: (i, 0)),
            pl.BlockSpec((LANES, tm), lambda i, j: (0, i)),
        ],
        out_shape=[
            jax.ShapeDtypeStruct((L, N_BIG), BF16),
            jax.ShapeDtypeStruct((L, LANES), F32),
            jax.ShapeDtypeStruct((LANES, L), F32),
        ],
        scratch_shapes=[pltpu.VMEM((tm, D), BF16),
                        pltpu.VMEM((tm + 8, tn), F32),
                        pltpu.VMEM((len(conv_tiles), 8, tn), F32)],
        compiler_params=pltpu.CompilerParams(
            dimension_semantics=("arbitrary", "arbitrary"),
            vmem_limit_bytes=VMEM_LIMIT_INPROJ),
        name="inproj",
    )(x2, norm_w, w_big, w_smt, conv_w)


def _ones_where(mask):
    return jnp.where(mask, 1.0, 0.0).astype(BF16)


def _chunk_tril(rows):
    ti = lax.broadcasted_iota(jnp.int32, (rows, rows), 0)
    si = lax.broadcasted_iota(jnp.int32, (rows, rows), 1)
    return ((ti >> 6) == (si >> 6)) & (ti >= si)


def _gla_ref_rows(c, c_scr, r0, ks, blk):
    C = CHUNK
    if 2 * blk >= 8:
        return jnp.concatenate(
            [jnp.broadcast_to(c_scr[pl.ds(r0 + g + blk - 1, 1), ks], (2 * blk, GLA_DK))
             for g in range(0, C, 2 * blk)], axis=0)
    pos = lax.broadcasted_iota(jnp.int32, (C, GLA_DK), 0) & (2 * blk - 1)
    out = c
    for off in range(2 * blk):
        if off != blk - 1:
            shift = (off - (blk - 1)) % C
            out = jnp.where(pos == off, pltpu.roll(c, shift, axis=0), out)
    return out


def _gla_scores(q, k, c, c_scr, r0, ks):
    C = CHUNK
    row = lax.broadcasted_iota(jnp.int32, (C, GLA_DK), 0)
    ti = lax.broadcasted_iota(jnp.int32, (C, C), 0)
    si = lax.broadcasted_iota(jnp.int32, (C, C), 1)
    a = jnp.where(ti == si, jnp.sum(q * k, axis=-1, keepdims=True), 0.0)
    blk = C // 2
    while blk >= 1:
        right = (row & (2 * blk - 1)) >= blk
        d = c - _gla_ref_rows(c, c_scr, r0, ks, blk)
        e = jnp.exp(jnp.where(right, d, -d))
        qt = jnp.where(right, q * e, 0.0).astype(BF16)
        kt = jnp.where(right, 0.0, k * e).astype(BF16)
        part = _dot_nt(qt, kt)
        if 2 * blk < C:
            shift = (2 * blk).bit_length() - 1
            part = jnp.where((ti >> shift) == (si >> shift), part, 0.0)
        a = a + part
        blk //= 2
    return a


def _lagged_proj_pieces(o_scr, base_ref, w_ref, out_ref, n_pieces):
    prev = 1 - (pl.program_id(0) & 1)
    width = w_ref.shape[1] // n_pieces

    def make(p):
        cols = slice(p * width, (p + 1) * width)

        def run():
            out_ref[:, cols] = base_ref[:, cols] + _dot(o_scr[prev], w_ref[:, cols])
        return run
    return [make(p) for p in range(n_pieces)]


def _gla_chunks(q_ref, v_ref, gate_ref, nw_ref, o_ref, st_ref, c_scr, k_scr, chunks_per_step,
                fillers):
    C = CHUNK
    tasks = [(ci, h) for ci in range(chunks_per_step) for h in range(GLA_HEADS)]

    def front(t):
        ci, h = t
        r0 = ci * C
        ks = slice(h * GLA_DK, (h + 1) * GLA_DK)
        q = q_ref[pl.ds(r0, C), ks].astype(F32) * (GLA_DK ** -0.5)
        k = k_scr[pl.ds(r0, C), ks]
        c = c_scr[pl.ds(r0, C), ks]
        a = _gla_scores(q, k, c, c_scr, r0, ks)
        c_last = c_scr[pl.ds(r0 + C - 1, 1), ks]
        qd = (q * jnp.exp(c)).astype(BF16)
        kd = (k * jnp.exp(c_last - c)).astype(BF16)
        return a.astype(BF16), qd, kd, jnp.exp(c_last)

    def back(t, a, qd, kd, g_last):
        ci, h = t
        rows = pl.ds(ci * C, C)
        vs = slice(h * GLA_DV, (h + 1) * GLA_DV)
        v = v_ref[rows, vs]
        st = st_ref[h]
        o = _dot(a, v) + _dot_nt(qd, st.astype(BF16))
        st_ref[h] = g_last * st + _dot_tn(v, kd)
        ms = jnp.mean(o * o, axis=-1, keepdims=True)
        on = o * lax.rsqrt(ms + EPS) * nw_ref[...]
        o_ref[rows, vs] = (on * gate_ref[rows, vs].astype(F32)).astype(BF16)

    lag = 4
    fronts = {}
    for f in fillers:
        f()
    for idx, t in enumerate(tasks):
        fronts[t] = front(t)
        if idx >= lag:
            back(tasks[idx - lag], *fronts.pop(tasks[idx - lag]))
    for t in tasks[-lag:]:
        back(t, *fronts.pop(t))


def _gla_kernel(q_ref, k_ref, v_ref, gate_ref, psm_ref, w2_ref, gb_ref, nw_ref,
                base_ref, wproj_ref, wsrc_ref, xo_ref, wdst_ref, st_ref, c_scr, k_scr, o_scr, *,
                chunks_per_step):
    slot = pl.program_id(0) & 1
    wdst_ref[...] = wsrc_ref[...].astype(BF16)

    @pl.when(pl.program_id(0) == 0)
    def _():
        st_ref[...] = jnp.zeros_like(st_ref)
        o_scr[1] = jnp.zeros(o_scr.shape[1:], o_scr.dtype)

    proj = _lagged_proj_pieces(o_scr, base_ref, wproj_ref, xo_ref, 8)

    a_hi, a_mid, _ = _split3(psm_ref[...])
    z = (_dot(a_hi, w2_ref[0]) + _dot(a_mid, w2_ref[0]) + _dot(a_hi, w2_ref[1])
         + gb_ref[...])
    for f in proj[:3]:
        f()
    log_a = _log_sigmoid(z) * (1.0 / GLA_GATE_TAU)
    tril = _ones_where(_chunk_tril(CHUNK * chunks_per_step))
    c_scr[...] = _dot_exact_lhs(tril, log_a)
    k_scr[...] = k_ref[...].astype(F32)

    _gla_chunks(q_ref, v_ref, gate_ref, nw_ref, o_scr.at[slot], st_ref, c_scr, k_scr,
                chunks_per_step, proj[3:])


def _row_cast_spec(n_steps, w):
    nb = _max_split(n_steps, w.shape[0], 16)
    return pl.BlockSpec((w.shape[0] // nb, w.shape[1]), lambda n: (jnp.minimum(n, nb - 1), 0))


def _gla(p_big, p_sm, w2, gate_b, norm_w, x2, w_out_b, side_w, chunks_per_step):
    L, D = x2.shape
    R = CHUNK * chunks_per_step
    last = L // R - 1
    side_spec = _row_cast_spec(L // R, side_w)
    cur = lambda n: jnp.minimum(n, last)
    prv = lambda n: jnp.maximum(n - 1, 0)
    kern = functools.partial(_gla_kernel, chunks_per_step=chunks_per_step)
    return pl.pallas_call(
        kern,
        grid=(L // R + 1,),
        in_specs=[
            pl.BlockSpec((R, GLA_QK), lambda n: (cur(n), 0)),
            pl.BlockSpec((R, GLA_QK), lambda n: (cur(n), 1)),
            pl.BlockSpec((R, GLA_V), lambda n: (cur(n), 1)),
            pl.BlockSpec((R, GLA_V), lambda n: (cur(n), 2)),
            pl.BlockSpec((R, LANES), lambda n: (cur(n), 0)),
            pl.BlockSpec((2, LANES, GLA_QK), lambda n: (0, 0, 0)),
            pl.BlockSpec((1, GLA_QK), lambda n: (0, 0)),
            pl.BlockSpec((1, GLA_DV), lambda n: (0, 0)),
            pl.BlockSpec((R, D), lambda n: (prv(n), 0)),
            pl.BlockSpec((GLA_V, D), lambda n: (0, 0)),
            side_spec,
        ],
        out_specs=[pl.BlockSpec((R, D), lambda n: (prv(n), 0)), side_spec],
        out_shape=[jax.ShapeDtypeStruct((L, D), F32), jax.ShapeDtypeStruct(side_w.shape, BF16)],
        scratch_shapes=[
            pltpu.VMEM((GLA_HEADS, GLA_DV, GLA_DK), F32),
            pltpu.VMEM((R, GLA_QK), F32),
            pltpu.VMEM((R, GLA_QK), F32),
            pltpu.VMEM((2, R, GLA_V), BF16),
        ],
        compiler_params=pltpu.CompilerParams(
            dimension_semantics=("arbitrary",),
            vmem_limit_bytes=VMEM_LIMIT),
        name="gla",
    )(p_big, p_big, p_big, p_big, p_sm, w2, gate_b, norm_w, x2, w_out_b, side_w)


def _gdn_chunks(q_ref, k_ref, v_ref, z_ref, nw_ref, o_ref, s_ref, cum_col, cum_row, beta_col,
                chunks_per_step, fillers):
    C = CHUNK
    fillers = list(fillers)

    def fill():
        if fillers:
            fillers.pop(0)()
    ti = lax.broadcasted_iota(jnp.int32, (C, 2 * C), 0)
    li = lax.broadcasted_iota(jnp.int32, (C, 2 * C), 1)
    si = li & (C - 1)
    right = li >= C
    incl = ti >= si
    strict = ti > si
    eye_right = jnp.where((ti == si) & right, 1.0, 0.0)
    tasks = [(ci, h) for ci in range(chunks_per_step) for h in range(GDN_HEADS)]

    wmat, attn, rhs, wq, kd, g_last = {}, {}, {}, {}, {}, {}
    for t in tasks:
        ci, h = t
        r0 = ci * C
        hs = slice(h * GDN_DK, (h + 1) * GDN_DK)
        qnb = q_ref[pl.ds(r0, C), hs]
        knb = k_ref[pl.ds(r0, C), hs]
        kn = knb.astype(F32)
        cv = v_ref[pl.ds(r0, C), hs].astype(F32)
        cc = cum_col[r0:r0 + C, COL_A + h:COL_A + h + 1]
        cr = cum_row[COL_A + h:COL_A + h + 1, 2 * r0:2 * r0 + 2 * C]
        beta = beta_col[r0:r0 + C, COL_B + h:COL_B + h + 1]
        gamma = jnp.exp(jnp.where(incl, cc - cr, -jnp.inf))
        kb = kn * beta
        x = _dot_nt(jnp.concatenate([qnb, kb.astype(BF16)], axis=0),
                    jnp.concatenate([knb, knb], axis=0))
        attn[t] = (x[:C, :C] * gamma[:, :C]).astype(BF16)
        wmat[t] = jnp.where(strict, x[C:] * gamma, 0.0)
        e_c = jnp.exp(cc)
        c_last = cc[C - 1:C, :]
        rhs1 = jnp.concatenate([cv * beta, kb * e_c], axis=1).astype(BF16)
        rhs[t] = jnp.concatenate([rhs1, rhs1], axis=0)
        wq[t] = (qnb.astype(F32) * e_c).astype(BF16)
        kd[t] = (kn * jnp.exp(c_last - cc)).astype(BF16)
        g_last[t] = jnp.exp(c_last)

    fill()
    for t in tasks:
        n2b = wmat[t].astype(BF16)
        wmat[t] = jnp.where(right, eye_right - wmat[t], _dot(n2b[:, :C], n2b))
    fill()
    for _ in range(5):
        for t in tasks:
            wb = wmat[t].astype(BF16)
            wmat[t] = _dot(wb[:, :C], wb) + jnp.where(right, wmat[t], 0.0)
        fill()
    sol = {t: _dot(wmat[t].astype(BF16), rhs[t]) for t in tasks}
    fill()

    for ci in range(chunks_per_step):
        rows = pl.ds(ci * C, C)
        heads = [(ci, h) for h in range(GDN_HEADS)]
        s_old = {t: s_ref[t[1]] for t in heads}
        ws = {t: _dot(jnp.concatenate([sol[t][:, GDN_DV:].astype(BF16), wq[t]], axis=0),
                      s_old[t].astype(BF16)) for t in heads}
        vnb = {t: (sol[t][:, :GDN_DV] - ws[t][:C]).astype(BF16) for t in heads}
        for t in heads:
            h = t[1]
            hs = slice(h * GDN_DK, (h + 1) * GDN_DK)
            o = ws[t][C:] + _dot(attn[t], vnb[t])
            s_ref[h] = g_last[t] * s_old[t] + _dot_tn(kd[t], vnb[t])
            ms = jnp.mean(o * o, axis=-1, keepdims=True)
            on = o * lax.rsqrt(ms + EPS) * nw_ref[...]
            o_ref[rows, hs] = (on * z_ref[rows, hs].astype(F32)).astype(BF16)
    while fillers:
        fill()


def _gdn_kernel(q_ref, k_ref, v_ref, z_ref, psm_ref, psmt_ref, arow_ref,
                acol_ref, nw_ref, base_ref, wproj_ref, *rest, chunks_per_step, n_side):
    side_in = rest[:n_side]
    xo_ref = rest[n_side]
    side_out = rest[n_side + 1:2 * n_side + 1]
    s_ref, o_scr = rest[2 * n_side + 1:]
    R = CHUNK * chunks_per_step
    slot = pl.program_id(0) & 1

    @pl.when(pl.program_id(0) == 0)
    def _():
        s_ref[...] = jnp.zeros_like(s_ref)
        o_scr[1] = jnp.zeros(o_scr.shape[1:], o_scr.dtype)

    for src, dst in zip(side_in, side_out):
        dst[...] = src[...].astype(BF16)

    psm = psm_ref[...]
    g_col = -jnp.exp(arow_ref[0:1, :]) * _softplus(psm + arow_ref[1:2, :])
    beta_col = _sigmoid(psm)
    cum_col = _dot_exact_lhs(_ones_where(_chunk_tril(R)), g_col)
    g_row = -jnp.exp(acol_ref[:, 0:1]) * _softplus(psmt_ref[...] + acol_ref[:, 1:2])
    ji = lax.broadcasted_iota(jnp.int32, (R, 2 * R), 0)
    li = lax.broadcasted_iota(jnp.int32, (R, 2 * R), 1)
    dup = ((ji >> 6) == (li >> 7)) & ((ji & (CHUNK - 1)) <= (li & (CHUNK - 1)))
    cum_row = _dot_exact_rhs(g_row, _ones_where(dup))

    _gdn_chunks(q_ref, k_ref, v_ref, z_ref, nw_ref, o_scr.at[slot], s_ref, cum_col, cum_row,
                beta_col, chunks_per_step,
                _lagged_proj_pieces(o_scr, base_ref, wproj_ref, xo_ref, 8))


def _gdn(p_big, p_sm, p_smt, a_row, a_col, norm_w, x1a, w_out_b, side_weights, chunks_per_step):
    L, D = x1a.shape
    R = CHUNK * chunks_per_step
    base = (GLA_QK * 2 + GLA_V * 2) // GDN_QK
    last = L // R - 1
    cur = lambda n: jnp.minimum(n, last)
    prv = lambda n: jnp.maximum(n - 1, 0)
    kern = functools.partial(_gdn_kernel, chunks_per_step=chunks_per_step,
                             n_side=len(side_weights))
    side_specs = [_row_cast_spec(L // R, w) for w in side_weights]
    outs = pl.pallas_call(
        kern,
        grid=(L // R + 1,),
        in_specs=[
            pl.BlockSpec((R, GDN_QK), lambda n: (cur(n), base)),
            pl.BlockSpec((R, GDN_QK), lambda n: (cur(n), base + 1)),
            pl.BlockSpec((R, GDN_V), lambda n: (cur(n), base + 2)),
            pl.BlockSpec((R, GDN_V), lambda n: (cur(n), base + 3)),
            pl.BlockSpec((R, LANES), lambda n: (cur(n), 0)),
            pl.BlockSpec((LANES, R), lambda n: (0, cur(n))),
            pl.BlockSpec((2, LANES), lambda n: (0, 0)),
            pl.BlockSpec((LANES, 2), lambda n: (0, 0)),
            pl.BlockSpec((1, GDN_DV), lambda n: (0, 0)),
            pl.BlockSpec((R, D), lambda n: (prv(n), 0)),
            pl.BlockSpec((GDN_V, D), lambda n: (GLA_V // GDN_V, 0)),
        ] + side_specs,
        out_specs=[pl.BlockSpec((R, D), lambda n: (prv(n), 0))] + side_specs,
        out_shape=[jax.ShapeDtypeStruct((L, D), F32)]
        + [jax.ShapeDtypeStruct(w.shape, BF16) for w in side_weights],
        scratch_shapes=[pltpu.VMEM((GDN_HEADS, GDN_DK, GDN_DV), F32),
                        pltpu.VMEM((2, R, GDN_V), BF16)],
        compiler_params=pltpu.CompilerParams(
            dimension_semantics=("arbitrary",),
            vmem_limit_bytes=VMEM_LIMIT),
        name="gdn",
    )(p_big, p_big, p_big, p_big, p_sm, p_smt, a_row, a_col, norm_w, x1a, w_out_b,
      *side_weights)
    return outs[0], outs[1:]


def _ffn_kernel(x_ref, fnw_ref, wg_ref, wu_ref, wd_ref, onw_ref, y_ref, hf_ref):
    f = pl.program_id(1)

    @pl.when(f == 0)
    def _():
        x = x_ref[...]
        ms = jnp.mean(x * x, axis=-1, keepdims=True)
        hf_ref[...] = (x * lax.rsqrt(ms + EPS) * fnw_ref[...]).astype(BF16)
        y_ref[...] = x

    hf = hf_ref[...]
    g = _dot(hf, wg_ref[...])
    u = _dot(hf, wu_ref[...])
    y_ref[...] += _dot((_silu(g) * u).astype(BF16), wd_ref[...])

    @pl.when(f == pl.num_programs(1) - 1)
    def _():
        r = y_ref[...]
        ms = jnp.mean(r * r, axis=-1, keepdims=True)
        y_ref[...] = r * lax.rsqrt(ms + EPS) * onw_ref[...]


def _ffn(x1, ffn_norm_w, wg, wu, wd, final_norm_w, tm, tf):
    L, D = x1.shape
    F = wg.shape[1]
    return pl.pallas_call(
        _ffn_kernel,
        grid=(L // tm, F // tf),
        in_specs=[
            pl.BlockSpec((tm, D), lambda i, f: (i, 0)),
            pl.BlockSpec((1, D), lambda i, f: (0, 0)),
            pl.BlockSpec((D, tf), lambda i, f: (0, f)),
            pl.BlockSpec((D, tf), lambda i, f: (0, f)),
            pl.BlockSpec((tf, D), lambda i, f: (f, 0)),
            pl.BlockSpec((1, D), lambda i, f: (0, 0)),
        ],
        out_specs=pl.BlockSpec((tm, D), lambda i, f: (i, 0)),
        out_shape=jax.ShapeDtypeStruct((L, D), F32),
        scratch_shapes=[pltpu.VMEM((tm, D), BF16)],
        compiler_params=pltpu.CompilerParams(
            dimension_semantics=("arbitrary", "arbitrary"),
            vmem_limit_bytes=VMEM_LIMIT_FFN),
        name="ffn",
    )(x1, ffn_norm_w, wg, wu, wd, final_norm_w)


def _hi_lo(w):
    hi = w.astype(BF16)
    lo = (w - hi.astype(F32)).astype(BF16)
    return jnp.stack([hi, lo])


def _pick(n, candidates):
    for c in candidates:
        if n % c == 0:
            return c
    raise ValueError(f"no tile in {candidates} divides {n}")


def kernel(x, attn_norm_w, w_in, gla_gate_w2, gla_gate_b, gla_norm_w, gdn_conv_w,
           gdn_a_log, gdn_dt_bias, gdn_norm_w, w_out, ffn_norm_w, w_gate, w_up,
           w_down, final_norm_w):
    bsz, seq, d_model = x.shape
    assert bsz == 1 and seq % CHUNK == 0
    x2 = x.reshape(seq, d_model).astype(F32)

    sizes = (GLA_QK, GLA_QK, GLA_V, GLA_GATE_RANK, GLA_V,
             GDN_QK, GDN_QK, GDN_V, GDN_HEADS, GDN_HEADS, GDN_V)
    offs = np.concatenate([[0], np.cumsum(sizes)])
    assert w_in.shape == (d_model, offs[-1])
    w_in_t = w_in.astype(F32).T
    row = lambda i: w_in_t[offs[i]:offs[i + 1]]
    n_small = GLA_GATE_RANK + 2 * GDN_HEADS
    w_small_t = jnp.concatenate([row(3), row(8), row(9)], axis=0)
    w_small_t = jnp.pad(w_small_t, ((0, LANES - n_small), (0, 0)))
    w_smt = _hi_lo(w_small_t).reshape(2 * LANES, d_model)

    tm1 = _pick(seq, (1024, 512, 256, 128, 64))
    w_big_t, w_out_b = _wprep(w_in_t, w_out.astype(F32), 512)
    p_big, p_sm, p_smt = _inproj(
        x2, attn_norm_w.reshape(1, -1).astype(F32), w_big_t, w_smt,
        gdn_conv_w.astype(F32), tm1, 1024)

    cps = _pick(seq // CHUNK, (4, 2, 1))

    w2_pad = jnp.pad(gla_gate_w2.astype(F32), ((0, LANES - GLA_GATE_RANK), (0, 0)))
    x1a, w_down_b = _gla(p_big, p_sm, _hi_lo(w2_pad), gla_gate_b.reshape(1, -1).astype(F32),
                         gla_norm_w.reshape(1, -1).astype(F32), x2, w_out_b,
                         w_down.astype(F32), cps)

    a_log_pad = jnp.zeros((LANES,), F32).at[COL_A:COL_A + GDN_HEADS].set(gdn_a_log.astype(F32))
    dt_pad = jnp.zeros((LANES,), F32).at[COL_A:COL_A + GDN_HEADS].set(gdn_dt_bias.astype(F32))
    a_row = jnp.stack([a_log_pad, dt_pad])
    x1, (w_gate_b, w_up_b) = _gdn(
        p_big, p_sm, p_smt, a_row, a_row.T, gdn_norm_w.reshape(1, -1).astype(F32), x1a, w_out_b,
        [w.astype(F32) for w in (w_gate, w_up)], cps)

    tm4 = _pick(seq, (1024, 512, 256, 128, 64))

    y = _ffn(x1, ffn_norm_w.reshape(1, -1).astype(F32), w_gate_b, w_up_b, w_down_b,
             final_norm_w.reshape(1, -1).astype(F32), tm4, 512)
    return y.reshape(bsz, seq, d_model).astype(x.dtype)
```

```python
import functools

import numpy as np
import jax
import jax.numpy as jnp
from jax import lax
from jax.experimental import pallas as pl
from jax.experimental.pallas import tpu as pltpu

F32 = jnp.float32
BF16 = jnp.bfloat16

EPS = 1e-6
CHUNK = 64

GLA_HEADS = 4
GLA_DK = 128
GLA_DV = 256
GLA_GATE_RANK = 16
GLA_GATE_TAU = 16.0
GLA_QK = GLA_HEADS * GLA_DK
GLA_V = GLA_HEADS * GLA_DV

GDN_HEADS = 8
GDN_DK = 128
GDN_DV = 128
GDN_CONV = 4
GDN_QK = GDN_HEADS * GDN_DK
GDN_V = GDN_HEADS * GDN_DV

LANES = 128
VMEM_LIMIT = 48 * 1024 * 1024
VMEM_LIMIT_INPROJ = 56 * 1024 * 1024
VMEM_LIMIT_FFN = 60 * 1024 * 1024
N_BIG = 2 * GLA_QK + 2 * GLA_V + 2 * GDN_QK + 2 * GDN_V

COL_A = GLA_GATE_RANK
COL_B = GLA_GATE_RANK + GDN_HEADS


def _dot(a, b):
    return jnp.dot(a, b, preferred_element_type=F32)


def _dot_nt(a, b):
    return lax.dot_general(a, b, (((1,), (1,)), ((), ())), preferred_element_type=F32)


def _dot_tn(a, b):
    return lax.dot_general(a, b, (((0,), (0,)), ((), ())), preferred_element_type=F32)


def _split3(a):
    hi = a.astype(BF16)
    r1 = a - hi.astype(F32)
    mid = r1.astype(BF16)
    lo = (r1 - mid.astype(F32)).astype(BF16)
    return hi, mid, lo


def _dot_exact_rhs(a, b_exact):
    hi, mid, lo = _split3(a)
    return _dot(hi, b_exact) + _dot(mid, b_exact) + _dot(lo, b_exact)


def _dot_exact_lhs(a_exact, b):
    hi, mid, lo = _split3(b)
    return _dot(a_exact, hi) + _dot(a_exact, mid) + _dot(a_exact, lo)


def _sigmoid(x):
    return 1.0 / (1.0 + jnp.exp(-x))


def _silu(x):
    return x * _sigmoid(x)


def _softplus(x):
    return jnp.maximum(x, 0.0) + jnp.log1p(jnp.exp(-jnp.abs(x)))


def _log_sigmoid(x):
    return -_softplus(-x)


def _wprep_kernel(wa_ref, wb_ref, wout_ref, o_ref, wout_b_ref, *, shift_steps):
    j = pl.program_id(0)
    j1, j2 = shift_steps
    wout_b_ref[...] = wout_ref[...].astype(BF16)

    def realigned(delta):
        if delta == 0:
            return wa_ref[...].astype(BF16)
        return jnp.concatenate([wa_ref[pl.ds(delta, wa_ref.shape[0] - delta), :],
                                wb_ref[pl.ds(0, delta), :]], axis=0).astype(BF16)

    @pl.when(j < j1)
    def _():
        o_ref[...] = realigned(0)

    @pl.when((j >= j1) & (j < j2))
    def _():
        o_ref[...] = realigned(GLA_GATE_RANK)

    @pl.when(j >= j2)
    def _():
        o_ref[...] = realigned(GLA_GATE_RANK + 2 * GDN_HEADS)


def _wprep(w_in_t, w_out, tn):
    D = w_in_t.shape[1]
    wout_spec = _row_cast_spec(N_BIG // tn, w_out)
    n_small = GLA_GATE_RANK + 2 * GDN_HEADS
    g1 = GLA_QK * 2 + GLA_V
    g2 = g1 + GLA_V + GDN_QK * 2 + GDN_V
    assert g1 % tn == 0 and g2 % tn == 0 and N_BIG % tn == 0 and tn % n_small == 0
    return pl.pallas_call(
        functools.partial(_wprep_kernel, shift_steps=(g1 // tn, g2 // tn)),
        grid=(N_BIG // tn,),
        in_specs=[
            pl.BlockSpec((tn, D), lambda j: (j, 0)),
            pl.BlockSpec((n_small, D), lambda j: ((tn // n_small) * (j + 1), 0)),
            wout_spec,
        ],
        out_specs=[pl.BlockSpec((tn, D), lambda j: (j, 0)), wout_spec],
        out_shape=[jax.ShapeDtypeStruct((N_BIG, D), BF16),
                   jax.ShapeDtypeStruct(w_out.shape, BF16)],
        compiler_params=pltpu.CompilerParams(
            dimension_semantics=("arbitrary",),
            vmem_limit_bytes=VMEM_LIMIT),
        name="wprep",
    )(w_in_t, w_in_t, w_out)


def _inproj_kernel(x_ref, nw_ref, wbig_ref, wsmt_ref, cw_ref, pbig_ref, psm_ref, psmt_ref,
                   h_ref, ext_scr, tail_scr, *, tile_kinds):
    i = pl.program_id(0)
    j = pl.program_id(1)
    tm = h_ref.shape[0]
    tn = wbig_ref.shape[0]
    piece = min(tm, 256)

    @pl.when(j == 0)
    def _():
        x = x_ref[...]
        ms = jnp.mean(x * x, axis=-1, keepdims=True)
        h = x * lax.rsqrt(ms + EPS) * nw_ref[...]
        hb = h.astype(BF16)
        h_ref[...] = hb
        hl = (h - hb.astype(F32)).astype(BF16)
        r = _dot_nt(wsmt_ref[...], hb)
        pt = r[:LANES] + r[LANES:] + _dot_nt(wsmt_ref[pl.ds(0, LANES), :], hl)
        psmt_ref[...] = pt
        psm_ref[...] = pt.T

    def piece_dot(r0):
        return _dot_nt(h_ref[pl.ds(r0, piece), :], wbig_ref[...])

    def steps_of(*kinds):
        pred = None
        for jj, kind in enumerate(tile_kinds):
            if kind in kinds:
                pred = (j == jj) if pred is None else (pred | (j == jj))
        return pred

    @pl.when(steps_of('plain'))
    def _():
        pbig_ref[...] = _dot_nt(h_ref[...], wbig_ref[...]).astype(BF16)

    @pl.when(steps_of('silu'))
    def _():
        for r0 in range(0, tm, piece):
            pbig_ref[pl.ds(r0, piece), :] = _silu(piece_dot(r0)).astype(BF16)

    conv_kinds = ('conv_norm_q', 'conv_norm_k', 'conv')
    first_conv = min(jj for jj, kind in enumerate(tile_kinds) if kind in conv_kinds)

    def conv_steps(normalise):
        jc = j - first_conv
        ext_scr[pl.ds(0, 8), :] = jnp.where(i == 0, 0.0, tail_scr[jc])
        scale = jnp.where(steps_of('conv_norm_q'), GDN_DK ** -0.5, 1.0) if normalise else None
        for r0 in range(0, tm, piece):
            ext_scr[pl.ds(8 + r0, piece), :] = piece_dot(r0)
            acc = cw_ref[GDN_CONV - 1:GDN_CONV, :] * ext_scr[pl.ds(8 + r0, piece), :]
            for t in range(GDN_CONV - 1):
                acc = acc + cw_ref[t:t + 1, :] * ext_scr[pl.ds(8 - (GDN_CONV - 1) + t + r0, piece), :]
            y = _silu(acc)
            if normalise:
                heads = []
                for hh in range(tn // GDN_DK):
                    yh = y[:, hh * GDN_DK:(hh + 1) * GDN_DK]
                    ssq = jnp.sum(yh * yh, axis=-1, keepdims=True)
                    heads.append(yh * (lax.rsqrt(ssq + EPS) * scale))
                y = jnp.concatenate(heads, axis=1)
            pbig_ref[pl.ds(r0, piece), :] = y.astype(BF16)
        tail_scr[jc] = ext_scr[pl.ds(tm, 8), :]

    @pl.when(steps_of('conv_norm_q', 'conv_norm_k'))
    def _():
        conv_steps(True)

    @pl.when(steps_of('conv'))
    def _():
        conv_steps(False)


def _max_split(n_steps, size, quantum):
    for nb in range(n_steps, 0, -1):
        if size % (nb * quantum) == 0:
            return nb
    raise ValueError(f"{size} is not a multiple of {quantum}")


def _inproj(x2, norm_w, w_big, w_smt, conv_w, tm, tn):
    L, D = x2.shape
    widths = (('plain', 2 * GLA_QK + GLA_V), ('silu', GLA_V), ('conv_norm_q', GDN_QK),
              ('conv_norm_k', GDN_QK), ('conv', GDN_V), ('silu', GDN_V))
    assert all(width % tn == 0 for _, width in widths) and tn % GDN_DK == 0
    tile_kinds = tuple(kind for kind, width in widths for _ in range(width // tn))
    conv_tiles = [jj for jj, kind in enumerate(tile_kinds) if kind.startswith('conv')]
    assert conv_tiles == list(range(conv_tiles[0], conv_tiles[-1] + 1))
    return pl.pallas_call(
        functools.partial(_inproj_kernel, tile_kinds=tile_kinds),
        grid=(L // tm, N_BIG // tn),
        in_specs=[
            pl.BlockSpec((tm, D), lambda i, j: (i, 0)),
            pl.BlockSpec((1, D), lambda i, j: (0, 0)),
            pl.BlockSpec((tn, D), lambda i, j: (j, 0)),
            pl.BlockSpec((2 * LANES, D), lambda i, j: (0, 0)),
            pl.BlockSpec((GDN_CONV, tn),
                         lambda i, j: (0, jnp.clip(j - conv_tiles[0], 0, len(conv_tiles) - 1))),
        ],
        out_specs=[
            pl.BlockSpec((tm, tn), lambda i, j: (i, j)),
            pl.BlockSpec((tm, LANES), lambda i, j: (i, 0)),
            pl.BlockSpec((LANES, tm), lambda i, j: (0, i)),
        ],
        out_shape=[
            jax.ShapeDtypeStruct((L, N_BIG), BF16),
            jax.ShapeDtypeStruct((L, LANES), F32),
            jax.ShapeDtypeStruct((LANES, L), F32),
        ],
        scratch_shapes=[pltpu.VMEM((tm, D), BF16),
                        pltpu.VMEM((tm + 8, tn), F32),
                        pltpu.VMEM((len(conv_tiles), 8, tn), F32)],
        compiler_params=pltpu.CompilerParams(
            dimension_semantics=("arbitrary", "arbitrary"),
            vmem_limit_bytes=VMEM_LIMIT_INPROJ),
        name="inproj",
    )(x2, norm_w, w_big, w_smt, conv_w)


def _ones_where(mask):
    return jnp.where(mask, 1.0, 0.0).astype(BF16)


def _chunk_tril(rows):
    ti = lax.broadcasted_iota(jnp.int32, (rows, rows), 0)
    si = lax.broadcasted_iota(jnp.int32, (rows, rows), 1)
    return ((ti >> 6) == (si >> 6)) & (ti >= si)


def _gla_ref_rows(c, c_scr, r0, ks, blk):
    C = CHUNK
    if 2 * blk >= 8:
        return jnp.concatenate(
            [jnp.broadcast_to(c_scr[pl.ds(r0 + g + blk - 1, 1), ks], (2 * blk, GLA_DK))
             for g in range(0, C, 2 * blk)], axis=0)
    pos = lax.broadcasted_iota(jnp.int32, (C, GLA_DK), 0) & (2 * blk - 1)
    out = c
    for off in range(2 * blk):
        if off != blk - 1:
            shift = (off - (blk - 1)) % C
            out = jnp.where(pos == off, pltpu.roll(c, shift, axis=0), out)
    return out


def _gla_scores(q, k, c, c_scr, r0, ks):
    C = CHUNK
    row = lax.broadcasted_iota(jnp.int32, (C, GLA_DK), 0)
    ti = lax.broadcasted_iota(jnp.int32, (C, C), 0)
    si = lax.broadcasted_iota(jnp.int32, (C, C), 1)
    a = jnp.where(ti == si, jnp.sum(q * k, axis=-1, keepdims=True), 0.0)
    blk = C // 2
    while blk >= 1:
        right = (row & (2 * blk - 1)) >= blk
        d = c - _gla_ref_rows(c, c_scr, r0, ks, blk)
        e = jnp.exp(jnp.where(right, d, -d))
        qt = jnp.where(right, q * e, 0.0).astype(BF16)
        kt = jnp.where(right, 0.0, k * e).astype(BF16)
        part = _dot_nt(qt, kt)
        if 2 * blk < C:
            shift = (2 * blk).bit_length() - 1
            part = jnp.where((ti >> shift) == (si >> shift), part, 0.0)
        a = a + part
        blk //= 2
    return a


def _lagged_proj_pieces(o_scr, base_ref, w_ref, out_ref, n_pieces):
    prev = 1 - (pl.program_id(0) & 1)
    width = w_ref.shape[1] // n_pieces

    def make(p):
        cols = slice(p * width, (p + 1) * width)

        def run():
            out_ref[:, cols] = base_ref[:, cols] + _dot(o_scr[prev], w_ref[:, cols])
        return run
    return [make(p) for p in range(n_pieces)]


def _gla_chunks(q_ref, v_ref, gate_ref, nw_ref, o_ref, st_ref, c_scr, k_scr, chunks_per_step,
                fillers):
    C = CHUNK
    tasks = [(ci, h) for ci in range(chunks_per_step) for h in range(GLA_HEADS)]

    def front(t):
        ci, h = t
        r0 = ci * C
        ks = slice(h * GLA_DK, (h + 1) * GLA_DK)
        q = q_ref[pl.ds(r0, C), ks].astype(F32) * (GLA_DK ** -0.5)
        k = k_scr[pl.ds(r0, C), ks]
        c = c_scr[pl.ds(r0, C), ks]
        a = _gla_scores(q, k, c, c_scr, r0, ks)
        c_last = c_scr[pl.ds(r0 + C - 1, 1), ks]
        qd = (q * jnp.exp(c)).astype(BF16)
        kd = (k * jnp.exp(c_last - c)).astype(BF16)
        return a.astype(BF16), qd, kd, jnp.exp(c_last)

    def back(t, a, qd, kd, g_last):
        ci, h = t
        rows = pl.ds(ci * C, C)
        vs = slice(h * GLA_DV, (h + 1) * GLA_DV)
        v = v_ref[rows, vs]
        st = st_ref[h]
        o = _dot(a, v) + _dot_nt(qd, st.astype(BF16))
        st_ref[h] = g_last * st + _dot_tn(v, kd)
        ms = jnp.mean(o * o, axis=-1, keepdims=True)
        on = o * lax.rsqrt(ms + EPS) * nw_ref[...]
        o_ref[rows, vs] = (on * gate_ref[rows, vs].astype(F32)).astype(BF16)

    lag = 4
    fronts = {}
    for f in fillers:
        f()
    for idx, t in enumerate(tasks):
        fronts[t] = front(t)
        if idx >= lag:
            back(tasks[idx - lag], *fronts.pop(tasks[idx - lag]))
    for t in tasks[-lag:]:
        back(t, *fronts.pop(t))


def _gla_kernel(q_ref, k_ref, v_ref, gate_ref, psm_ref, w2_ref, gb_ref, nw_ref,
                base_ref, wproj_ref, xo_ref, st_ref, c_scr, k_scr, o_scr, *, chunks_per_step):
    slot = pl.program_id(0) & 1

    @pl.when(pl.program_id(0) == 0)
    def _():
        st_ref[...] = jnp.zeros_like(st_ref)
        o_scr[1] = jnp.zeros(o_scr.shape[1:], o_scr.dtype)

    proj = _lagged_proj_pieces(o_scr, base_ref, wproj_ref, xo_ref, 8)

    a_hi, a_mid, _ = _split3(psm_ref[...])
    z = (_dot(a_hi, w2_ref[0]) + _dot(a_mid, w2_ref[0]) + _dot(a_hi, w2_ref[1])
         + gb_ref[...])
    for f in proj[:3]:
        f()
    log_a = _log_sigmoid(z) * (1.0 / GLA_GATE_TAU)
    tril = _ones_where(_chunk_tril(CHUNK * chunks_per_step))
    c_scr[...] = _dot_exact_lhs(tril, log_a)
    k_scr[...] = k_ref[...].astype(F32)

    _gla_chunks(q_ref, v_ref, gate_ref, nw_ref, o_scr.at[slot], st_ref, c_scr, k_scr,
                chunks_per_step, proj[3:])


def _row_cast_spec(n_steps, w):
    nb = _max_split(n_steps, w.shape[0], 16)
    return pl.BlockSpec((w.shape[0] // nb, w.shape[1]), lambda n: (jnp.minimum(n, nb - 1), 0))


def _gla(p_big, p_sm, w2, gate_b, norm_w, x2, w_out_b, chunks_per_step):
    L, D = x2.shape
    R = CHUNK * chunks_per_step
    last = L // R - 1
    cur = lambda n: jnp.minimum(n, last)
    prv = lambda n: jnp.maximum(n - 1, 0)
    kern = functools.partial(_gla_kernel, chunks_per_step=chunks_per_step)
    return pl.pallas_call(
        kern,
        grid=(L // R + 1,),
        in_specs=[
            pl.BlockSpec((R, GLA_QK), lambda n: (cur(n), 0)),
            pl.BlockSpec((R, GLA_QK), lambda n: (cur(n), 1)),
            pl.BlockSpec((R, GLA_V), lambda n: (cur(n), 1)),
            pl.BlockSpec((R, GLA_V), lambda n: (cur(n), 2)),
            pl.BlockSpec((R, LANES), lambda n: (cur(n), 0)),
            pl.BlockSpec((2, LANES, GLA_QK), lambda n: (0, 0, 0)),
            pl.BlockSpec((1, GLA_QK), lambda n: (0, 0)),
            pl.BlockSpec((1, GLA_DV), lambda n: (0, 0)),
            pl.BlockSpec((R, D), lambda n: (prv(n), 0)),
            pl.BlockSpec((GLA_V, D), lambda n: (0, 0)),
        ],
        out_specs=pl.BlockSpec((R, D), lambda n: (prv(n), 0)),
        out_shape=jax.ShapeDtypeStruct((L, D), F32),
        scratch_shapes=[
            pltpu.VMEM((GLA_HEADS, GLA_DV, GLA_DK), F32),
            pltpu.VMEM((R, GLA_QK), F32),
            pltpu.VMEM((R, GLA_QK), F32),
            pltpu.VMEM((2, R, GLA_V), BF16),
        ],
        compiler_params=pltpu.CompilerParams(
            dimension_semantics=("arbitrary",),
            vmem_limit_bytes=VMEM_LIMIT),
        name="gla",
    )(p_big, p_big, p_big, p_big, p_sm, w2, gate_b, norm_w, x2, w_out_b)


def _gdn_chunks(q_ref, k_ref, v_ref, z_ref, nw_ref, o_ref, s_ref, cum_col, cum_row, beta_col,
                chunks_per_step, fillers):
    C = CHUNK
    fillers = list(fillers)

    def fill():
        if fillers:
            fillers.pop(0)()
    ti = lax.broadcasted_iota(jnp.int32, (C, 2 * C), 0)
    li = lax.broadcasted_iota(jnp.int32, (C, 2 * C), 1)
    si = li & (C - 1)
    right = li >= C
    incl = ti >= si
    strict = ti > si
    eye_right = jnp.where((ti == si) & right, 1.0, 0.0)
    tasks = [(ci, h) for ci in range(chunks_per_step) for h in range(GDN_HEADS)]

    wmat, attn, rhs, wq, kd, g_last = {}, {}, {}, {}, {}, {}
    for t in tasks:
        ci, h = t
        r0 = ci * C
        hs = slice(h * GDN_DK, (h + 1) * GDN_DK)
        qnb = q_ref[pl.ds(r0, C), hs]
        knb = k_ref[pl.ds(r0, C), hs]
        kn = knb.astype(F32)
        cv = v_ref[pl.ds(r0, C), hs].astype(F32)
        cc = cum_col[r0:r0 + C, COL_A + h:COL_A + h + 1]
        cr = cum_row[COL_A + h:COL_A + h + 1, 2 * r0:2 * r0 + 2 * C]
        beta = beta_col[r0:r0 + C, COL_B + h:COL_B + h + 1]
        gamma = jnp.exp(jnp.where(incl, cc - cr, -jnp.inf))
        kb = kn * beta
        x = _dot_nt(jnp.concatenate([qnb, kb.astype(BF16)], axis=0),
                    jnp.concatenate([knb, knb], axis=0))
        attn[t] = (x[:C, :C] * gamma[:, :C]).astype(BF16)
        wmat[t] = jnp.where(strict, x[C:] * gamma, 0.0)
        e_c = jnp.exp(cc)
        c_last = cc[C - 1:C, :]
        rhs1 = jnp.concatenate([cv * beta, kb * e_c], axis=1).astype(BF16)
        rhs[t] = jnp.concatenate([rhs1, rhs1], axis=0)
        wq[t] = (qnb.astype(F32) * e_c).astype(BF16)
        kd[t] = (kn * jnp.exp(c_last - cc)).astype(BF16)
        g_last[t] = jnp.exp(c_last)

    fill()
    for t in tasks:
        n2b = wmat[t].astype(BF16)
        wmat[t] = jnp.where(right, eye_right - wmat[t], _dot(n2b[:, :C], n2b))
    fill()
    for _ in range(5):
        for t in tasks:
            wb = wmat[t].astype(BF16)
            wmat[t] = _dot(wb[:, :C], wb) + jnp.where(right, wmat[t], 0.0)
        fill()
    sol = {t: _dot(wmat[t].astype(BF16), rhs[t]) for t in tasks}
    fill()

    for ci in range(chunks_per_step):
        rows = pl.ds(ci * C, C)
        heads = [(ci, h) for h in range(GDN_HEADS)]
        s_old = {t: s_ref[t[1]] for t in heads}
        ws = {t: _dot(jnp.concatenate([sol[t][:, GDN_DV:].astype(BF16), wq[t]], axis=0),
                      s_old[t].astype(BF16)) for t in heads}
        vnb = {t: (sol[t][:, :GDN_DV] - ws[t][:C]).astype(BF16) for t in heads}
        for t in heads:
            h = t[1]
            hs = slice(h * GDN_DK, (h + 1) * GDN_DK)
            o = ws[t][C:] + _dot(attn[t], vnb[t])
            s_ref[h] = g_last[t] * s_old[t] + _dot_tn(kd[t], vnb[t])
            ms = jnp.mean(o * o, axis=-1, keepdims=True)
            on = o * lax.rsqrt(ms + EPS) * nw_ref[...]
            o_ref[rows, hs] = (on * z_ref[rows, hs].astype(F32)).astype(BF16)
    while fillers:
        fill()


def _gdn_kernel(q_ref, k_ref, v_ref, z_ref, psm_ref, psmt_ref, arow_ref,
                acol_ref, nw_ref, base_ref, wproj_ref, *rest, chunks_per_step, n_side):
    side_in = rest[:n_side]
    xo_ref = rest[n_side]
    side_out = rest[n_side + 1:2 * n_side + 1]
    s_ref, o_scr = rest[2 * n_side + 1:]
    R = CHUNK * chunks_per_step
    slot = pl.program_id(0) & 1

    @pl.when(pl.program_id(0) == 0)
    def _():
        s_ref[...] = jnp.zeros_like(s_ref)
        o_scr[1] = jnp.zeros(o_scr.shape[1:], o_scr.dtype)

    for src, dst in zip(side_in, side_out):
        dst[...] = src[...].astype(BF16)

    psm = psm_ref[...]
    g_col = -jnp.exp(arow_ref[0:1, :]) * _softplus(psm + arow_ref[1:2, :])
    beta_col = _sigmoid(psm)
    cum_col = _dot_exact_lhs(_ones_where(_chunk_tril(R)), g_col)
    g_row = -jnp.exp(acol_ref[:, 0:1]) * _softplus(psmt_ref[...] + acol_ref[:, 1:2])
    ji = lax.broadcasted_iota(jnp.int32, (R, 2 * R), 0)
    li = lax.broadcasted_iota(jnp.int32, (R, 2 * R), 1)
    dup = ((ji >> 6) == (li >> 7)) & ((ji & (CHUNK - 1)) <= (li & (CHUNK - 1)))
    cum_row = _dot_exact_rhs(g_row, _ones_where(dup))

    _gdn_chunks(q_ref, k_ref, v_ref, z_ref, nw_ref, o_scr.at[slot], s_ref, cum_col, cum_row,
                beta_col, chunks_per_step,
                _lagged_proj_pieces(o_scr, base_ref, wproj_ref, xo_ref, 8))


def _gdn(p_big, p_sm, p_smt, a_row, a_col, norm_w, x1a, w_out_b, side_weights, chunks_per_step):
    L, D = x1a.shape
    R = CHUNK * chunks_per_step
    base = (GLA_QK * 2 + GLA_V * 2) // GDN_QK
    last = L // R - 1
    cur = lambda n: jnp.minimum(n, last)
    prv = lambda n: jnp.maximum(n - 1, 0)
    kern = functools.partial(_gdn_kernel, chunks_per_step=chunks_per_step,
                             n_side=len(side_weights))
    side_specs = [_row_cast_spec(L // R, w) for w in side_weights]
    outs = pl.pallas_call(
        kern,
        grid=(L // R + 1,),
        in_specs=[
            pl.BlockSpec((R, GDN_QK), lambda n: (cur(n), base)),
            pl.BlockSpec((R, GDN_QK), lambda n: (cur(n), base + 1)),
            pl.BlockSpec((R, GDN_V), lambda n: (cur(n), base + 2)),
            pl.BlockSpec((R, GDN_V), lambda n: (cur(n), base + 3)),
            pl.BlockSpec((R, LANES), lambda n: (cur(n), 0)),
            pl.BlockSpec((LANES, R), lambda n: (0, cur(n))),
            pl.BlockSpec((2, LANES), lambda n: (0, 0)),
            pl.BlockSpec((LANES, 2), lambda n: (0, 0)),
            pl.BlockSpec((1, GDN_DV), lambda n: (0, 0)),
            pl.BlockSpec((R, D), lambda n: (prv(n), 0)),
            pl.BlockSpec((GDN_V, D), lambda n: (GLA_V // GDN_V, 0)),
        ] + side_specs,
        out_specs=[pl.BlockSpec((R, D), lambda n: (prv(n), 0))] + side_specs,
        out_shape=[jax.ShapeDtypeStruct((L, D), F32)]
        + [jax.ShapeDtypeStruct(w.shape, BF16) for w in side_weights],
        scratch_shapes=[pltpu.VMEM((GDN_HEADS, GDN_DK, GDN_DV), F32),
                        pltpu.VMEM((2, R, GDN_V), BF16)],
        compiler_params=pltpu.CompilerParams(
            dimension_semantics=("arbitrary",),
            vmem_limit_bytes=VMEM_LIMIT),
        name="gdn",
    )(p_big, p_big, p_big, p_big, p_sm, p_smt, a_row, a_col, norm_w, x1a, w_out_b,
      *side_weights)
    return outs[0], outs[1:]


def _ffn_kernel(x_ref, fnw_ref, wg_hbm, wu_hbm, wd_hbm, onw_ref, y_ref,
                hf_ref, wg_buf, wu_buf, wd_buf, sem, *, tf, n_tiles):
    i = pl.program_id(0)
    n_rows = pl.num_programs(0)

    def tile_copies(f, slot):
        c0 = pl.multiple_of(f * tf, tf)
        return (pltpu.make_async_copy(wg_hbm.at[:, pl.ds(c0, tf)], wg_buf.at[slot], sem.at[0, slot]),
                pltpu.make_async_copy(wu_hbm.at[:, pl.ds(c0, tf)], wu_buf.at[slot], sem.at[1, slot]),
                pltpu.make_async_copy(wd_hbm.at[pl.ds(c0, tf), :], wd_buf.at[slot], sem.at[2, slot]))

    @pl.when(i == 0)
    def _():
        for c in tile_copies(0, 0):
            c.start()

    x = x_ref[...]
    ms = jnp.mean(x * x, axis=-1, keepdims=True)
    hf_ref[...] = (x * lax.rsqrt(ms + EPS) * fnw_ref[...]).astype(BF16)
    y_ref[...] = x

    def step(f, carry):
        g = i * n_tiles + f
        slot = g & 1
        for c in tile_copies(f, slot):
            c.wait()

        @pl.when(g + 1 < n_rows * n_tiles)
        def _():
            for c in tile_copies(jnp.where(f + 1 == n_tiles, 0, f + 1), 1 - slot):
                c.start()

        hf = hf_ref[...]
        gate = _dot(hf, wg_buf[slot])
        up = _dot(hf, wu_buf[slot])
        y_ref[...] += _dot((_silu(gate) * up).astype(BF16), wd_buf[slot])
        return carry

    lax.fori_loop(0, n_tiles, step, 0)

    r = y_ref[...]
    ms = jnp.mean(r * r, axis=-1, keepdims=True)
    y_ref[...] = r * lax.rsqrt(ms + EPS) * onw_ref[...]


def _ffn(x1, ffn_norm_w, wg, wu, wd, final_norm_w, tm, tf):
    L, D = x1.shape
    F = wg.shape[1]
    return pl.pallas_call(
        functools.partial(_ffn_kernel, tf=tf, n_tiles=F // tf),
        grid=(L // tm,),
        in_specs=[
            pl.BlockSpec((tm, D), lambda i: (i, 0)),
            pl.BlockSpec((1, D), lambda i: (0, 0)),
            pl.BlockSpec(memory_space=pl.ANY),
            pl.BlockSpec(memory_space=pl.ANY),
            pl.BlockSpec(memory_space=pl.ANY),
            pl.BlockSpec((1, D), lambda i: (0, 0)),
        ],
        out_specs=pl.BlockSpec((tm, D), lambda i: (i, 0)),
        out_shape=jax.ShapeDtypeStruct((L, D), F32),
        scratch_shapes=[pltpu.VMEM((tm, D), BF16),
                        pltpu.VMEM((2, D, tf), BF16),
                        pltpu.VMEM((2, D, tf), BF16),
                        pltpu.VMEM((2, tf, D), BF16),
                        pltpu.SemaphoreType.DMA((3, 2))],
        compiler_params=pltpu.CompilerParams(
            dimension_semantics=("arbitrary",),
            vmem_limit_bytes=VMEM_LIMIT_FFN),
        name="ffn",
    )(x1, ffn_norm_w, wg, wu, wd, final_norm_w)


def _hi_lo(w):
    hi = w.astype(BF16)
    lo = (w - hi.astype(F32)).astype(BF16)
    return jnp.stack([hi, lo])


def _pick(n, candidates):
    for c in candidates:
        if n % c == 0:
            return c
    raise ValueError(f"no tile in {candidates} divides {n}")


def kernel(x, attn_norm_w, w_in, gla_gate_w2, gla_gate_b, gla_norm_w, gdn_conv_w,
           gdn_a_log, gdn_dt_bias, gdn_norm_w, w_out, ffn_norm_w, w_gate, w_up,
           w_down, final_norm_w):
    bsz, seq, d_model = x.shape
    assert bsz == 1 and seq % CHUNK == 0
    x2 = x.reshape(seq, d_model).astype(F32)

    sizes = (GLA_QK, GLA_QK, GLA_V, GLA_GATE_RANK, GLA_V,
             GDN_QK, GDN_QK, GDN_V, GDN_HEADS, GDN_HEADS, GDN_V)
    offs = np.concatenate([[0], np.cumsum(sizes)])
    assert w_in.shape == (d_model, offs[-1])
    w_in_t = w_in.astype(F32).T
    row = lambda i: w_in_t[offs[i]:offs[i + 1]]
    n_small = GLA_GATE_RANK + 2 * GDN_HEADS
    w_small_t = jnp.concatenate([row(3), row(8), row(9)], axis=0)
    w_small_t = jnp.pad(w_small_t, ((0, LANES - n_small), (0, 0)))
    w_smt = _hi_lo(w_small_t).reshape(2 * LANES, d_model)

    tm1 = _pick(seq, (1024, 512, 256, 128, 64))
    w_big_t, w_out_b = _wprep(w_in_t, w_out.astype(F32), 512)
    p_big, p_sm, p_smt = _inproj(
        x2, attn_norm_w.reshape(1, -1).astype(F32), w_big_t, w_smt,
        gdn_conv_w.astype(F32), tm1, 1024)

    cps = _pick(seq // CHUNK, (4, 2, 1))

    w2_pad = jnp.pad(gla_gate_w2.astype(F32), ((0, LANES - GLA_GATE_RANK), (0, 0)))
    x1a = _gla(p_big, p_sm, _hi_lo(w2_pad), gla_gate_b.reshape(1, -1).astype(F32),
               gla_norm_w.reshape(1, -1).astype(F32), x2, w_out_b, cps)

    a_log_pad = jnp.zeros((LANES,), F32).at[COL_A:COL_A + GDN_HEADS].set(gdn_a_log.astype(F32))
    dt_pad = jnp.zeros((LANES,), F32).at[COL_A:COL_A + GDN_HEADS].set(gdn_dt_bias.astype(F32))
    a_row = jnp.stack([a_log_pad, dt_pad])
    x1, (w_gate_b, w_up_b, w_down_b) = _gdn(
        p_big, p_sm, p_smt, a_row, a_row.T, gdn_norm_w.reshape(1, -1).astype(F32), x1a, w_out_b,
        [w.astype(F32) for w in (w_gate, w_up, w_down)], cps)

    tm4 = _pick(seq, (1024, 512, 256, 128, 64))

    y = _ffn(x1, ffn_norm_w.reshape(1, -1).astype(F32), w_gate_b, w_up_b, w_down_b,
             final_norm_w.reshape(1, -1).astype(F32), tm4, 512)
    return y.reshape(bsz, seq, d_model).astype(x.dtype)
```

```python
import functools

import numpy as np
import jax
import jax.numpy as jnp
from jax import lax
from jax.experimental import pallas as pl
from jax.experimental.pallas import tpu as pltpu

F32 = jnp.float32
BF16 = jnp.bfloat16

EPS = 1e-6
CHUNK = 64

GLA_HEADS = 4
GLA_DK = 128
GLA_DV = 256
GLA_GATE_RANK = 16
GLA_GATE_TAU = 16.0
GLA_QK = GLA_HEADS * GLA_DK
GLA_V = GLA_HEADS * GLA_DV

GDN_HEADS = 8
GDN_DK = 128
GDN_DV = 128
GDN_CONV = 4
GDN_QK = GDN_HEADS * GDN_DK
GDN_V = GDN_HEADS * GDN_DV

LANES = 128
VMEM_LIMIT = 48 * 1024 * 1024
VMEM_LIMIT_INPROJ = 56 * 1024 * 1024
VMEM_LIMIT_FFN = 62 * 1024 * 1024
FFN_RING = 3
N_BIG = 2 * GLA_QK + 2 * GLA_V + 2 * GDN_QK + 2 * GDN_V

COL_A = GLA_GATE_RANK
COL_B = GLA_GATE_RANK + GDN_HEADS


def _dot(a, b):
    return jnp.dot(a, b, preferred_element_type=F32)


def _dot_nt(a, b):
    return lax.dot_general(a, b, (((1,), (1,)), ((), ())), preferred_element_type=F32)


def _dot_tn(a, b):
    return lax.dot_general(a, b, (((0,), (0,)), ((), ())), preferred_element_type=F32)


def _split3(a):
    hi = a.astype(BF16)
    r1 = a - hi.astype(F32)
    mid = r1.astype(BF16)
    lo = (r1 - mid.astype(F32)).astype(BF16)
    return hi, mid, lo


def _dot_exact_rhs(a, b_exact):
    hi, mid, lo = _split3(a)
    return _dot(hi, b_exact) + _dot(mid, b_exact) + _dot(lo, b_exact)


def _dot_exact_lhs(a_exact, b):
    hi, mid, lo = _split3(b)
    return _dot(a_exact, hi) + _dot(a_exact, mid) + _dot(a_exact, lo)


def _sigmoid(x):
    return 1.0 / (1.0 + jnp.exp(-x))


def _silu(x):
    return x * _sigmoid(x)


def _softplus(x):
    return jnp.maximum(x, 0.0) + jnp.log1p(jnp.exp(-jnp.abs(x)))


def _log_sigmoid(x):
    return -_softplus(-x)


def _wprep_kernel(wa_ref, wb_ref, wout_ref, o_ref, wout_b_ref, *, shift_steps):
    j = pl.program_id(0)
    j1, j2 = shift_steps
    wout_b_ref[...] = wout_ref[...].astype(BF16)

    def realigned(delta):
        if delta == 0:
            return wa_ref[...].astype(BF16)
        return jnp.concatenate([wa_ref[pl.ds(delta, wa_ref.shape[0] - delta), :],
                                wb_ref[pl.ds(0, delta), :]], axis=0).astype(BF16)

    @pl.when(j < j1)
    def _():
        o_ref[...] = realigned(0)

    @pl.when((j >= j1) & (j < j2))
    def _():
        o_ref[...] = realigned(GLA_GATE_RANK)

    @pl.when(j >= j2)
    def _():
        o_ref[...] = realigned(GLA_GATE_RANK + 2 * GDN_HEADS)


def _wprep(w_in_t, w_out, tn):
    D = w_in_t.shape[1]
    wout_spec = _row_cast_spec(N_BIG // tn, w_out)
    n_small = GLA_GATE_RANK + 2 * GDN_HEADS
    g1 = GLA_QK * 2 + GLA_V
    g2 = g1 + GLA_V + GDN_QK * 2 + GDN_V
    assert g1 % tn == 0 and g2 % tn == 0 and N_BIG % tn == 0 and tn % n_small == 0
    return pl.pallas_call(
        functools.partial(_wprep_kernel, shift_steps=(g1 // tn, g2 // tn)),
        grid=(N_BIG // tn,),
        in_specs=[
            pl.BlockSpec((tn, D), lambda j: (j, 0)),
            pl.BlockSpec((n_small, D), lambda j: ((tn // n_small) * (j + 1), 0)),
            wout_spec,
        ],
        out_specs=[pl.BlockSpec((tn, D), lambda j: (j, 0)), wout_spec],
        out_shape=[jax.ShapeDtypeStruct((N_BIG, D), BF16),
                   jax.ShapeDtypeStruct(w_out.shape, BF16)],
        compiler_params=pltpu.CompilerParams(
            dimension_semantics=("arbitrary",),
            vmem_limit_bytes=VMEM_LIMIT),
        name="wprep",
    )(w_in_t, w_in_t, w_out)


def _inproj_kernel(x_ref, nw_ref, wbig_ref, wsmt_ref, cw_ref, pbig_ref, psm_ref, psmt_ref,
                   h_ref, ext_scr, tail_scr, *, tile_kinds):
    i = pl.program_id(0)
    j = pl.program_id(1)
    tm = h_ref.shape[0]
    tn = wbig_ref.shape[0]
    piece = min(tm, 256)

    @pl.when(j == 0)
    def _():
        x = x_ref[...]
        ms = jnp.mean(x * x, axis=-1, keepdims=True)
        h = x * lax.rsqrt(ms + EPS) * nw_ref[...]
        hb = h.astype(BF16)
        h_ref[...] = hb
        hl = (h - hb.astype(F32)).astype(BF16)
        r = _dot_nt(wsmt_ref[...], hb)
        pt = r[:LANES] + r[LANES:] + _dot_nt(wsmt_ref[pl.ds(0, LANES), :], hl)
        psmt_ref[...] = pt
        psm_ref[...] = pt.T

    def piece_dot(r0):
        return _dot_nt(h_ref[pl.ds(r0, piece), :], wbig_ref[...])

    def steps_of(*kinds):
        pred = None
        for jj, kind in enumerate(tile_kinds):
            if kind in kinds:
                pred = (j == jj) if pred is None else (pred | (j == jj))
        return pred

    @pl.when(steps_of('plain'))
    def _():
        pbig_ref[...] = _dot_nt(h_ref[...], wbig_ref[...]).astype(BF16)

    @pl.when(steps_of('silu'))
    def _():
        for r0 in range(0, tm, piece):
            pbig_ref[pl.ds(r0, piece), :] = _silu(piece_dot(r0)).astype(BF16)

    conv_kinds = ('conv_norm_q', 'conv_norm_k', 'conv')
    first_conv = min(jj for jj, kind in enumerate(tile_kinds) if kind in conv_kinds)

    def conv_steps(normalise):
        jc = j - first_conv
        ext_scr[pl.ds(0, 8), :] = jnp.where(i == 0, 0.0, tail_scr[jc])
        scale = jnp.where(steps_of('conv_norm_q'), GDN_DK ** -0.5, 1.0) if normalise else None
        for r0 in range(0, tm, piece):
            ext_scr[pl.ds(8 + r0, piece), :] = piece_dot(r0)
            acc = cw_ref[GDN_CONV - 1:GDN_CONV, :] * ext_scr[pl.ds(8 + r0, piece), :]
            for t in range(GDN_CONV - 1):
                acc = acc + cw_ref[t:t + 1, :] * ext_scr[pl.ds(8 - (GDN_CONV - 1) + t + r0, piece), :]
            y = _silu(acc)
            if normalise:
                heads = []
                for hh in range(tn // GDN_DK):
                    yh = y[:, hh * GDN_DK:(hh + 1) * GDN_DK]
                    ssq = jnp.sum(yh * yh, axis=-1, keepdims=True)
                    heads.append(yh * (lax.rsqrt(ssq + EPS) * scale))
                y = jnp.concatenate(heads, axis=1)
            pbig_ref[pl.ds(r0, piece), :] = y.astype(BF16)
        tail_scr[jc] = ext_scr[pl.ds(tm, 8), :]

    @pl.when(steps_of('conv_norm_q', 'conv_norm_k'))
    def _():
        conv_steps(True)

    @pl.when(steps_of('conv'))
    def _():
        conv_steps(False)


def _max_split(n_steps, size, quantum):
    for nb in range(n_steps, 0, -1):
        if size % (nb * quantum) == 0:
            return nb
    raise ValueError(f"{size} is not a multiple of {quantum}")


def _inproj(x2, norm_w, w_big, w_smt, conv_w, tm, tn):
    L, D = x2.shape
    widths = (('plain', 2 * GLA_QK + GLA_V), ('silu', GLA_V), ('conv_norm_q', GDN_QK),
              ('conv_norm_k', GDN_QK), ('conv', GDN_V), ('silu', GDN_V))
    assert all(width % tn == 0 for _, width in widths) and tn % GDN_DK == 0
    tile_kinds = tuple(kind for kind, width in widths for _ in range(width // tn))
    conv_tiles = [jj for jj, kind in enumerate(tile_kinds) if kind.startswith('conv')]
    assert conv_tiles == list(range(conv_tiles[0], conv_tiles[-1] + 1))
    return pl.pallas_call(
        functools.partial(_inproj_kernel, tile_kinds=tile_kinds),
        grid=(L // tm, N_BIG // tn),
        in_specs=[
            pl.BlockSpec((tm, D), lambda i, j: (i, 0)),
            pl.BlockSpec((1, D), lambda i, j: (0, 0)),
            pl.BlockSpec((tn, D), lambda i, j: (j, 0)),
            pl.BlockSpec((2 * LANES, D), lambda i, j: (0, 0)),
            pl.BlockSpec((GDN_CONV, tn),
                         lambda i, j: (0, jnp.clip(j - conv_tiles[0], 0, len(conv_tiles) - 1))),
        ],
        out_specs=[
            pl.BlockSpec((tm, tn), lambda i, j: (i, j)),
            pl.BlockSpec((tm, LANES), lambda i, j: (i, 0)),
            pl.BlockSpec((LANES, tm), lambda i, j: (0, i)),
        ],
        out_shape=[
            jax.ShapeDtypeStruct((L, N_BIG), BF16),
            jax.ShapeDtypeStruct((L, LANES), F32),
            jax.ShapeDtypeStruct((LANES, L), F32),
        ],
        scratch_shapes=[pltpu.VMEM((tm, D), BF16),
                        pltpu.VMEM((tm + 8, tn), F32),
                        pltpu.VMEM((len(conv_tiles), 8, tn), F32)],
        compiler_params=pltpu.CompilerParams(
            dimension_semantics=("arbitrary", "arbitrary"),
            vmem_limit_bytes=VMEM_LIMIT_INPROJ),
        name="inproj",
    )(x2, norm_w, w_big, w_smt, conv_w)


def _ones_where(mask):
    return jnp.where(mask, 1.0, 0.0).astype(BF16)


def _chunk_tril(rows):
    ti = lax.broadcasted_iota(jnp.int32, (rows, rows), 0)
    si = lax.broadcasted_iota(jnp.int32, (rows, rows), 1)
    return ((ti >> 6) == (si >> 6)) & (ti >= si)


def _gla_ref_rows(c, c_scr, r0, ks, blk):
    C = CHUNK
    if 2 * blk >= 8:
        return jnp.concatenate(
            [jnp.broadcast_to(c_scr[pl.ds(r0 + g + blk - 1, 1), ks], (2 * blk, GLA_DK))
             for g in range(0, C, 2 * blk)], axis=0)
    pos = lax.broadcasted_iota(jnp.int32, (C, GLA_DK), 0) & (2 * blk - 1)
    out = c
    for off in range(2 * blk):
        if off != blk - 1:
            shift = (off - (blk - 1)) % C
            out = jnp.where(pos == off, pltpu.roll(c, shift, axis=0), out)
    return out


def _gla_scores(q, k, c, c_scr, r0, ks):
    C = CHUNK
    row = lax.broadcasted_iota(jnp.int32, (C, GLA_DK), 0)
    ti = lax.broadcasted_iota(jnp.int32, (C, C), 0)
    si = lax.broadcasted_iota(jnp.int32, (C, C), 1)
    a = jnp.where(ti == si, jnp.sum(q * k, axis=-1, keepdims=True), 0.0)
    blk = C // 2
    while blk >= 1:
        right = (row & (2 * blk - 1)) >= blk
        d = c - _gla_ref_rows(c, c_scr, r0, ks, blk)
        e = jnp.exp(jnp.where(right, d, -d))
        qt = jnp.where(right, q * e, 0.0).astype(BF16)
        kt = jnp.where(right, 0.0, k * e).astype(BF16)
        part = _dot_nt(qt, kt)
        if 2 * blk < C:
            shift = (2 * blk).bit_length() - 1
            part = jnp.where((ti >> shift) == (si >> shift), part, 0.0)
        a = a + part
        blk //= 2
    return a


def _lagged_proj_pieces(o_scr, base_ref, w_ref, out_ref, n_pieces):
    prev = 1 - (pl.program_id(0) & 1)
    width = w_ref.shape[1] // n_pieces

    def make(p):
        cols = slice(p * width, (p + 1) * width)

        def run():
            out_ref[:, cols] = base_ref[:, cols] + _dot(o_scr[prev], w_ref[:, cols])
        return run
    return [make(p) for p in range(n_pieces)]


def _gla_chunks(q_ref, v_ref, gate_ref, nw_ref, o_ref, st_ref, c_scr, k_scr, chunks_per_step,
                fillers):
    C = CHUNK
    tasks = [(ci, h) for ci in range(chunks_per_step) for h in range(GLA_HEADS)]

    def front(t):
        ci, h = t
        r0 = ci * C
        ks = slice(h * GLA_DK, (h + 1) * GLA_DK)
        q = q_ref[pl.ds(r0, C), ks].astype(F32) * (GLA_DK ** -0.5)
        k = k_scr[pl.ds(r0, C), ks]
        c = c_scr[pl.ds(r0, C), ks]
        a = _gla_scores(q, k, c, c_scr, r0, ks)
        c_last = c_scr[pl.ds(r0 + C - 1, 1), ks]
        qd = (q * jnp.exp(c)).astype(BF16)
        kd = (k * jnp.exp(c_last - c)).astype(BF16)
        return a.astype(BF16), qd, kd, jnp.exp(c_last)

    def back(t, a, qd, kd, g_last):
        ci, h = t
        rows = pl.ds(ci * C, C)
        vs = slice(h * GLA_DV, (h + 1) * GLA_DV)
        v = v_ref[rows, vs]
        st = st_ref[h]
        o = _dot(a, v) + _dot_nt(qd, st.astype(BF16))
        st_ref[h] = g_last * st + _dot_tn(v, kd)
        ms = jnp.mean(o * o, axis=-1, keepdims=True)
        on = o * lax.rsqrt(ms + EPS) * nw_ref[...]
        o_ref[rows, vs] = (on * gate_ref[rows, vs].astype(F32)).astype(BF16)

    lag = 4
    fronts = {}
    for f in fillers:
        f()
    for idx, t in enumerate(tasks):
        fronts[t] = front(t)
        if idx >= lag:
            back(tasks[idx - lag], *fronts.pop(tasks[idx - lag]))
    for t in tasks[-lag:]:
        back(t, *fronts.pop(t))


def _gla_kernel(q_ref, k_ref, v_ref, gate_ref, psm_ref, w2_ref, gb_ref, nw_ref,
                base_ref, wproj_ref, xo_ref, st_ref, c_scr, k_scr, o_scr, *, chunks_per_step):
    slot = pl.program_id(0) & 1

    @pl.when(pl.program_id(0) == 0)
    def _():
        st_ref[...] = jnp.zeros_like(st_ref)
        o_scr[1] = jnp.zeros(o_scr.shape[1:], o_scr.dtype)

    proj = _lagged_proj_pieces(o_scr, base_ref, wproj_ref, xo_ref, 8)

    a_hi, a_mid, _ = _split3(psm_ref[...])
    z = (_dot(a_hi, w2_ref[0]) + _dot(a_mid, w2_ref[0]) + _dot(a_hi, w2_ref[1])
         + gb_ref[...])
    for f in proj[:3]:
        f()
    log_a = _log_sigmoid(z) * (1.0 / GLA_GATE_TAU)
    tril = _ones_where(_chunk_tril(CHUNK * chunks_per_step))
    c_scr[...] = _dot_exact_lhs(tril, log_a)
    k_scr[...] = k_ref[...].astype(F32)

    _gla_chunks(q_ref, v_ref, gate_ref, nw_ref, o_scr.at[slot], st_ref, c_scr, k_scr,
                chunks_per_step, proj[3:])


def _row_cast_spec(n_steps, w):
    nb = _max_split(n_steps, w.shape[0], 16)
    return pl.BlockSpec((w.shape[0] // nb, w.shape[1]), lambda n: (jnp.minimum(n, nb - 1), 0))


def _gla(p_big, p_sm, w2, gate_b, norm_w, x2, w_out_b, chunks_per_step):
    L, D = x2.shape
    R = CHUNK * chunks_per_step
    last = L // R - 1
    cur = lambda n: jnp.minimum(n, last)
    prv = lambda n: jnp.maximum(n - 1, 0)
    kern = functools.partial(_gla_kernel, chunks_per_step=chunks_per_step)
    return pl.pallas_call(
        kern,
        grid=(L // R + 1,),
        in_specs=[
            pl.BlockSpec((R, GLA_QK), lambda n: (cur(n), 0)),
            pl.BlockSpec((R, GLA_QK), lambda n: (cur(n), 1)),
            pl.BlockSpec((R, GLA_V), lambda n: (cur(n), 1)),
            pl.BlockSpec((R, GLA_V), lambda n: (cur(n), 2)),
            pl.BlockSpec((R, LANES), lambda n: (cur(n), 0)),
            pl.BlockSpec((2, LANES, GLA_QK), lambda n: (0, 0, 0)),
            pl.BlockSpec((1, GLA_QK), lambda n: (0, 0)),
            pl.BlockSpec((1, GLA_DV), lambda n: (0, 0)),
            pl.BlockSpec((R, D), lambda n: (prv(n), 0)),
            pl.BlockSpec((GLA_V, D), lambda n: (0, 0)),
        ],
        out_specs=pl.BlockSpec((R, D), lambda n: (prv(n), 0)),
        out_shape=jax.ShapeDtypeStruct((L, D), F32),
        scratch_shapes=[
            pltpu.VMEM((GLA_HEADS, GLA_DV, GLA_DK), F32),
            pltpu.VMEM((R, GLA_QK), F32),
            pltpu.VMEM((R, GLA_QK), F32),
            pltpu.VMEM((2, R, GLA_V), BF16),
        ],
        compiler_params=pltpu.CompilerParams(
            dimension_semantics=("arbitrary",),
            vmem_limit_bytes=VMEM_LIMIT),
        name="gla",
    )(p_big, p_big, p_big, p_big, p_sm, w2, gate_b, norm_w, x2, w_out_b)


def _gdn_chunks(q_ref, k_ref, v_ref, z_ref, nw_ref, o_ref, s_ref, cum_col, cum_row, beta_col,
                chunks_per_step, fillers):
    C = CHUNK
    fillers = list(fillers)

    def fill():
        if fillers:
            fillers.pop(0)()
    ti = lax.broadcasted_iota(jnp.int32, (C, 2 * C), 0)
    li = lax.broadcasted_iota(jnp.int32, (C, 2 * C), 1)
    si = li & (C - 1)
    right = li >= C
    incl = ti >= si
    strict = ti > si
    eye_right = jnp.where((ti == si) & right, 1.0, 0.0)
    tasks = [(ci, h) for ci in range(chunks_per_step) for h in range(GDN_HEADS)]

    wmat, attn, rhs, wq, kd, g_last = {}, {}, {}, {}, {}, {}
    for t in tasks:
        ci, h = t
        r0 = ci * C
        hs = slice(h * GDN_DK, (h + 1) * GDN_DK)
        qnb = q_ref[pl.ds(r0, C), hs]
        knb = k_ref[pl.ds(r0, C), hs]
        kn = knb.astype(F32)
        cv = v_ref[pl.ds(r0, C), hs].astype(F32)
        cc = cum_col[r0:r0 + C, COL_A + h:COL_A + h + 1]
        cr = cum_row[COL_A + h:COL_A + h + 1, 2 * r0:2 * r0 + 2 * C]
        beta = beta_col[r0:r0 + C, COL_B + h:COL_B + h + 1]
        gamma = jnp.exp(jnp.where(incl, cc - cr, -jnp.inf))
        kb = kn * beta
        x = _dot_nt(jnp.concatenate([qnb, kb.astype(BF16)], axis=0),
                    jnp.concatenate([knb, knb], axis=0))
        attn[t] = (x[:C, :C] * gamma[:, :C]).astype(BF16)
        wmat[t] = jnp.where(strict, x[C:] * gamma, 0.0)
        e_c = jnp.exp(cc)
        c_last = cc[C - 1:C, :]
        rhs1 = jnp.concatenate([cv * beta, kb * e_c], axis=1).astype(BF16)
        rhs[t] = jnp.concatenate([rhs1, rhs1], axis=0)
        wq[t] = (qnb.astype(F32) * e_c).astype(BF16)
        kd[t] = (kn * jnp.exp(c_last - cc)).astype(BF16)
        g_last[t] = jnp.exp(c_last)

    fill()
    for t in tasks:
        n2b = wmat[t].astype(BF16)
        wmat[t] = jnp.where(right, eye_right - wmat[t], _dot(n2b[:, :C], n2b))
    fill()
    for _ in range(5):
        for t in tasks:
            wb = wmat[t].astype(BF16)
            wmat[t] = _dot(wb[:, :C], wb) + jnp.where(right, wmat[t], 0.0)
        fill()
    sol = {t: _dot(wmat[t].astype(BF16), rhs[t]) for t in tasks}
    fill()

    for ci in range(chunks_per_step):
        rows = pl.ds(ci * C, C)
        heads = [(ci, h) for h in range(GDN_HEADS)]
        s_old = {t: s_ref[t[1]] for t in heads}
        ws = {t: _dot(jnp.concatenate([sol[t][:, GDN_DV:].astype(BF16), wq[t]], axis=0),
                      s_old[t].astype(BF16)) for t in heads}
        vnb = {t: (sol[t][:, :GDN_DV] - ws[t][:C]).astype(BF16) for t in heads}
        for t in heads:
            h = t[1]
            hs = slice(h * GDN_DK, (h + 1) * GDN_DK)
            o = ws[t][C:] + _dot(attn[t], vnb[t])
            s_ref[h] = g_last[t] * s_old[t] + _dot_tn(kd[t], vnb[t])
            ms = jnp.mean(o * o, axis=-1, keepdims=True)
            on = o * lax.rsqrt(ms + EPS) * nw_ref[...]
            o_ref[rows, hs] = (on * z_ref[rows, hs].astype(F32)).astype(BF16)
    while fillers:
        fill()


def _gdn_kernel(q_ref, k_ref, v_ref, z_ref, psm_ref, psmt_ref, arow_ref,
                acol_ref, nw_ref, base_ref, wproj_ref, *rest, chunks_per_step, n_side):
    side_in = rest[:n_side]
    xo_ref = rest[n_side]
    side_out = rest[n_side + 1:2 * n_side + 1]
    s_ref, o_scr = rest[2 * n_side + 1:]
    R = CHUNK * chunks_per_step
    slot = pl.program_id(0) & 1

    @pl.when(pl.program_id(0) == 0)
    def _():
        s_ref[...] = jnp.zeros_like(s_ref)
        o_scr[1] = jnp.zeros(o_scr.shape[1:], o_scr.dtype)

    for src, dst in zip(side_in, side_out):
        dst[...] = src[...].astype(BF16)

    psm = psm_ref[...]
    g_col = -jnp.exp(arow_ref[0:1, :]) * _softplus(psm + arow_ref[1:2, :])
    beta_col = _sigmoid(psm)
    cum_col = _dot_exact_lhs(_ones_where(_chunk_tril(R)), g_col)
    g_row = -jnp.exp(acol_ref[:, 0:1]) * _softplus(psmt_ref[...] + acol_ref[:, 1:2])
    ji = lax.broadcasted_iota(jnp.int32, (R, 2 * R), 0)
    li = lax.broadcasted_iota(jnp.int32, (R, 2 * R), 1)
    dup = ((ji >> 6) == (li >> 7)) & ((ji & (CHUNK - 1)) <= (li & (CHUNK - 1)))
    cum_row = _dot_exact_rhs(g_row, _ones_where(dup))

    _gdn_chunks(q_ref, k_ref, v_ref, z_ref, nw_ref, o_scr.at[slot], s_ref, cum_col, cum_row,
                beta_col, chunks_per_step,
                _lagged_proj_pieces(o_scr, base_ref, wproj_ref, xo_ref, 8))


def _gdn(p_big, p_sm, p_smt, a_row, a_col, norm_w, x1a, w_out_b, side_weights, chunks_per_step):
    L, D = x1a.shape
    R = CHUNK * chunks_per_step
    base = (GLA_QK * 2 + GLA_V * 2) // GDN_QK
    last = L // R - 1
    cur = lambda n: jnp.minimum(n, last)
    prv = lambda n: jnp.maximum(n - 1, 0)
    kern = functools.partial(_gdn_kernel, chunks_per_step=chunks_per_step,
                             n_side=len(side_weights))
    side_specs = [_row_cast_spec(L // R, w) for w in side_weights]
    outs = pl.pallas_call(
        kern,
        grid=(L // R + 1,),
        in_specs=[
            pl.BlockSpec((R, GDN_QK), lambda n: (cur(n), base)),
            pl.BlockSpec((R, GDN_QK), lambda n: (cur(n), base + 1)),
            pl.BlockSpec((R, GDN_V), lambda n: (cur(n), base + 2)),
            pl.BlockSpec((R, GDN_V), lambda n: (cur(n), base + 3)),
            pl.BlockSpec((R, LANES), lambda n: (cur(n), 0)),
            pl.BlockSpec((LANES, R), lambda n: (0, cur(n))),
            pl.BlockSpec((2, LANES), lambda n: (0, 0)),
            pl.BlockSpec((LANES, 2), lambda n: (0, 0)),
            pl.BlockSpec((1, GDN_DV), lambda n: (0, 0)),
            pl.BlockSpec((R, D), lambda n: (prv(n), 0)),
            pl.BlockSpec((GDN_V, D), lambda n: (GLA_V // GDN_V, 0)),
        ] + side_specs,
        out_specs=[pl.BlockSpec((R, D), lambda n: (prv(n), 0))] + side_specs,
        out_shape=[jax.ShapeDtypeStruct((L, D), F32)]
        + [jax.ShapeDtypeStruct(w.shape, BF16) for w in side_weights],
        scratch_shapes=[pltpu.VMEM((GDN_HEADS, GDN_DK, GDN_DV), F32),
                        pltpu.VMEM((2, R, GDN_V), BF16)],
        compiler_params=pltpu.CompilerParams(
            dimension_semantics=("arbitrary",),
            vmem_limit_bytes=VMEM_LIMIT),
        name="gdn",
    )(p_big, p_big, p_big, p_big, p_sm, p_smt, a_row, a_col, norm_w, x1a, w_out_b,
      *side_weights)
    return outs[0], outs[1:]


def _ffn_kernel(x_ref, fnw_ref, wg_hbm, wu_hbm, wd_hbm, onw_ref, y_ref,
                hf_ref, wg_buf, wu_buf, wd_buf, sem, *, tf, n_tiles):
    i = pl.program_id(0)
    n_rows = pl.num_programs(0)

    def tile_copies(f, slot):
        c0 = pl.multiple_of(f * tf, tf)
        return (pltpu.make_async_copy(wg_hbm.at[:, pl.ds(c0, tf)], wg_buf.at[slot], sem.at[0, slot]),
                pltpu.make_async_copy(wu_hbm.at[:, pl.ds(c0, tf)], wu_buf.at[slot], sem.at[1, slot]),
                pltpu.make_async_copy(wd_hbm.at[pl.ds(c0, tf), :], wd_buf.at[slot], sem.at[2, slot]))

    ahead = FFN_RING - 1
    assert n_tiles > ahead

    @pl.when(i == 0)
    def _():
        for t in range(ahead):
            for c in tile_copies(t, t):
                c.start()

    x = x_ref[...]
    ms = jnp.mean(x * x, axis=-1, keepdims=True)
    hf_ref[...] = (x * lax.rsqrt(ms + EPS) * fnw_ref[...]).astype(BF16)
    y_ref[...] = x

    def step(f, carry):
        g = i * n_tiles + f
        slot = lax.rem(g, FFN_RING)
        for c in tile_copies(f, slot):
            c.wait()

        @pl.when(g + ahead < n_rows * n_tiles)
        def _():
            nxt = f + ahead
            for c in tile_copies(jnp.where(nxt >= n_tiles, nxt - n_tiles, nxt),
                                 lax.rem(g + ahead, FFN_RING)):
                c.start()

        hf = hf_ref[...]
        gate = _dot(hf, wg_buf[slot])
        up = _dot(hf, wu_buf[slot])
        y_ref[...] += _dot((_silu(gate) * up).astype(BF16), wd_buf[slot])
        return carry

    lax.fori_loop(0, n_tiles, step, 0)

    r = y_ref[...]
    ms = jnp.mean(r * r, axis=-1, keepdims=True)
    y_ref[...] = r * lax.rsqrt(ms + EPS) * onw_ref[...]


def _ffn(x1, ffn_norm_w, wg, wu, wd, final_norm_w, tm, tf):
    L, D = x1.shape
    F = wg.shape[1]
    return pl.pallas_call(
        functools.partial(_ffn_kernel, tf=tf, n_tiles=F // tf),
        grid=(L // tm,),
        in_specs=[
            pl.BlockSpec((tm, D), lambda i: (i, 0)),
            pl.BlockSpec((1, D), lambda i: (0, 0)),
            pl.BlockSpec(memory_space=pl.ANY),
            pl.BlockSpec(memory_space=pl.ANY),
            pl.BlockSpec(memory_space=pl.ANY),
            pl.BlockSpec((1, D), lambda i: (0, 0)),
        ],
        out_specs=pl.BlockSpec((tm, D), lambda i: (i, 0)),
        out_shape=jax.ShapeDtypeStruct((L, D), F32),
        scratch_shapes=[pltpu.VMEM((tm, D), BF16),
                        pltpu.VMEM((FFN_RING, D, tf), BF16),
                        pltpu.VMEM((FFN_RING, D, tf), BF16),
                        pltpu.VMEM((FFN_RING, tf, D), BF16),
                        pltpu.SemaphoreType.DMA((3, FFN_RING))],
        compiler_params=pltpu.CompilerParams(
            dimension_semantics=("arbitrary",),
            vmem_limit_bytes=VMEM_LIMIT_FFN),
        name="ffn",
    )(x1, ffn_norm_w, wg, wu, wd, final_norm_w)


def _hi_lo(w):
    hi = w.astype(BF16)
    lo = (w - hi.astype(F32)).astype(BF16)
    return jnp.stack([hi, lo])


def _pick(n, candidates):
    for c in candidates:
        if n % c == 0:
            return c
    raise ValueError(f"no tile in {candidates} divides {n}")


def kernel(x, attn_norm_w, w_in, gla_gate_w2, gla_gate_b, gla_norm_w, gdn_conv_w,
           gdn_a_log, gdn_dt_bias, gdn_norm_w, w_out, ffn_norm_w, w_gate, w_up,
           w_down, final_norm_w):
    bsz, seq, d_model = x.shape
    assert bsz == 1 and seq % CHUNK == 0
    x2 = x.reshape(seq, d_model).astype(F32)

    sizes = (GLA_QK, GLA_QK, GLA_V, GLA_GATE_RANK, GLA_V,
             GDN_QK, GDN_QK, GDN_V, GDN_HEADS, GDN_HEADS, GDN_V)
    offs = np.concatenate([[0], np.cumsum(sizes)])
    assert w_in.shape == (d_model, offs[-1])
    w_in_t = w_in.astype(F32).T
    row = lambda i: w_in_t[offs[i]:offs[i + 1]]
    n_small = GLA_GATE_RANK + 2 * GDN_HEADS
    w_small_t = jnp.concatenate([row(3), row(8), row(9)], axis=0)
    w_small_t = jnp.pad(w_small_t, ((0, LANES - n_small), (0, 0)))
    w_smt = _hi_lo(w_small_t).reshape(2 * LANES, d_model)

    tm1 = _pick(seq, (1024, 512, 256, 128, 64))
    w_big_t, w_out_b = _wprep(w_in_t, w_out.astype(F32), 512)
    p_big, p_sm, p_smt = _inproj(
        x2, attn_norm_w.reshape(1, -1).astype(F32), w_big_t, w_smt,
        gdn_conv_w.astype(F32), tm1, 1024)

    cps = _pick(seq // CHUNK, (4, 2, 1))

    w2_pad = jnp.pad(gla_gate_w2.astype(F32), ((0, LANES - GLA_GATE_RANK), (0, 0)))
    x1a = _gla(p_big, p_sm, _hi_lo(w2_pad), gla_gate_b.reshape(1, -1).astype(F32),
               gla_norm_w.reshape(1, -1).astype(F32), x2, w_out_b, cps)

    a_log_pad = jnp.zeros((LANES,), F32).at[COL_A:COL_A + GDN_HEADS].set(gdn_a_log.astype(F32))
    dt_pad = jnp.zeros((LANES,), F32).at[COL_A:COL_A + GDN_HEADS].set(gdn_dt_bias.astype(F32))
    a_row = jnp.stack([a_log_pad, dt_pad])
    x1, (w_gate_b, w_up_b, w_down_b) = _gdn(
        p_big, p_sm, p_smt, a_row, a_row.T, gdn_norm_w.reshape(1, -1).astype(F32), x1a, w_out_b,
        [w.astype(F32) for w in (w_gate, w_up, w_down)], cps)

    tm4 = _pick(seq, (1024, 512, 256, 128, 64))

    y = _ffn(x1, ffn_norm_w.reshape(1, -1).astype(F32), w_gate_b, w_up_b, w_down_b,
             final_norm_w.reshape(1, -1).astype(F32), tm4, 512)
    return y.reshape(bsz, seq, d_model).astype(x.dtype)
```
